```python
import math
import jax, jax.numpy as jnp
from jax import lax
import numpy as np


D_MODEL = 2048
BATCH = 1
SEQ = 8192
DEPTH = 2

CHUNK = 64
Q_BLOCK = 128
N_MIXERS = 2
N_ATTN_LAYERS = (DEPTH + 1) // 2
N_GDN_LAYERS = DEPTH // 2
RMS_EPS = 1e-6

DIFF_HEADS = D_MODEL // 256
DIFF_HEAD_DIM = 128
DIFF_QK_DIM = 2 * DIFF_HEADS * DIFF_HEAD_DIM
DIFF_V_DIM = DIFF_HEADS * 2 * DIFF_HEAD_DIM
LAMBDA_STD = 0.1

GDN_HEAD_DIM = 128
GDN_QK_HEADS = D_MODEL // 128
GDN_V_HEADS = 2 * GDN_QK_HEADS
GDN_REPEAT = GDN_V_HEADS // GDN_QK_HEADS
GDN_KEY_DIM = GDN_QK_HEADS * GDN_HEAD_DIM
GDN_VALUE_DIM = GDN_V_HEADS * GDN_HEAD_DIM
GDN_CONV = 4
GDN_IN_DIM = 2 * GDN_KEY_DIM + 2 * GDN_VALUE_DIM + 2 * GDN_V_HEADS

D_FF = ((8 * D_MODEL // 3 + 127) // 128) * 128
FFN_CONV = 3

kernel_name = 'hybrid_diffattn_gdn_convffn'


def rms_norm(x, g):
    xf = x.astype(jnp.float32)
    y = xf * lax.rsqrt(jnp.mean(xf * xf, axis=-1, keepdims=True) + RMS_EPS)
    return (y * g.astype(jnp.float32)).astype(x.dtype)


def l2_norm(x):
    return x * lax.rsqrt(jnp.sum(x * x, axis=-1, keepdims=True) + RMS_EPS)


def causal_dwconv(u, w):
    k_width = w.shape[0]
    s = u.shape[1]
    up = jnp.pad(u, ((0, 0), (k_width - 1, 0), (0, 0)))
    out = up[:, 0:s] * w[0]
    for j in range(1, k_width):
        out = out + up[:, j:j + s] * w[j]
    return out


def alibi_slopes(n_heads):
    return jnp.exp2(-8.0 * jnp.arange(1, n_heads + 1, dtype=jnp.float32) / n_heads)


def diff_attention(h, w_qkv, q_norm, k_norm, lq1, lk1, lq2, lk2, subln, w_o, lambda_init):
    b_sz, s_len, _ = h.shape
    n_blocks = s_len // Q_BLOCK
    qkv = h @ w_qkv
    q, k, v = jnp.split(qkv, [DIFF_QK_DIM, 2 * DIFF_QK_DIM], axis=-1)
    q = rms_norm(q.reshape(b_sz, s_len, 2 * DIFF_HEADS, DIFF_HEAD_DIM), q_norm) * (DIFF_HEAD_DIM ** -0.5)
    k = rms_norm(k.reshape(b_sz, s_len, 2 * DIFF_HEADS, DIFF_HEAD_DIM), k_norm)
    v = v.reshape(b_sz, s_len, DIFF_HEADS, 2 * DIFF_HEAD_DIM)
    lam = (jnp.exp(jnp.sum(lq1.astype(jnp.float32) * lk1.astype(jnp.float32)))
           - jnp.exp(jnp.sum(lq2.astype(jnp.float32) * lk2.astype(jnp.float32))) + lambda_init)
    q_blocks = q.reshape(b_sz, n_blocks, Q_BLOCK, 2 * DIFF_HEADS, DIFF_HEAD_DIM).transpose(1, 0, 3, 2, 4)
    k_t = k.transpose(0, 2, 1, 3)
    v_t = v.transpose(0, 2, 1, 3)
    k_pos = jnp.arange(s_len, dtype=jnp.int32)
    k_chunk = k_pos // CHUNK
    slopes = jnp.repeat(alibi_slopes(DIFF_HEADS), 2)

    def block(args):
        q_blk, blk = args
        q_pos = blk * Q_BLOCK + jnp.arange(Q_BLOCK, dtype=jnp.int32)
        s = jnp.einsum('bhqd,bhkd->bhqk', q_blk, k_t).astype(jnp.float32)
        dist = jnp.abs(q_pos[:, None] - k_pos[None, :]).astype(jnp.float32)
        s = s - slopes[:, None, None] * dist
        allowed = k_chunk[None, :] <= (q_pos // CHUNK)[:, None]
        s = jnp.where(allowed, s, -jnp.inf)
        p = jax.nn.softmax(s, axis=-1).reshape(b_sz, DIFF_HEADS, 2, Q_BLOCK, s_len)
        a = p[:, :, 0] - lam * p[:, :, 1]
        return jnp.einsum('bhqk,bhke->bhqe', a.astype(v_t.dtype), v_t)

    o = lax.map(block, (q_blocks, jnp.arange(n_blocks, dtype=jnp.int32)))
    o = o.transpose(1, 0, 3, 2, 4).reshape(b_sz, s_len, DIFF_HEADS, 2 * DIFF_HEAD_DIM)
    o = rms_norm(o, subln) * (1.0 - lambda_init)
    return o.reshape(b_sz, s_len, DIFF_V_DIM) @ w_o


def gated_delta_rule(q, k, v, g, beta):
    b_sz, s_len, n_h, dk = q.shape
    dv = v.shape[-1]
    n_c = s_len // CHUNK
    q = q * (dk ** -0.5)

    def to_chunks(t):
        return t.reshape(b_sz, n_c, CHUNK, n_h, -1).transpose(0, 3, 1, 2, 4)

    q, k, v = to_chunks(q), to_chunks(k), to_chunks(v)
    beta_c = beta.reshape(b_sz, n_c, CHUNK, n_h).transpose(0, 3, 1, 2)
    g_c = jnp.cumsum(g.reshape(b_sz, n_c, CHUNK, n_h).transpose(0, 3, 1, 2), axis=-1)
    k_beta = k * beta_c[..., None]
    v_beta = v * beta_c[..., None]
    tril = jnp.tril(jnp.ones((CHUNK, CHUNK), dtype=bool))
    strict = jnp.tril(jnp.ones((CHUNK, CHUNK), dtype=bool), -1)
    decay = jnp.exp(jnp.where(tril, g_c[..., :, None] - g_c[..., None, :], -jnp.inf))
    low = jnp.where(strict, jnp.einsum('bhncd,bhnkd->bhnck', k_beta, k) * decay, 0.0)
    a_mat = low + jnp.eye(CHUNK, dtype=low.dtype)
    u = lax.linalg.triangular_solve(a_mat, v_beta, left_side=True, lower=True, unit_diagonal=True)
    w = lax.linalg.triangular_solve(a_mat, k_beta * jnp.exp(g_c)[..., None], left_side=True, lower=True, unit_diagonal=True)
    qk = jnp.where(tril, jnp.einsum('bhncd,bhnkd->bhnck', q, k) * decay, 0.0)
    q_dec = q * jnp.exp(g_c)[..., None]
    k_dec = k * jnp.exp(g_c[..., -1:] - g_c)[..., None]
    g_last = jnp.exp(g_c[..., -1])

    def step(state, xs):
        qk_n, u_n, w_n, qd_n, kd_n, gl_n = xs
        v_new = u_n - jnp.einsum('bhcd,bhde->bhce', w_n, state)
        o = jnp.einsum('bhcd,bhde->bhce', qd_n, state) + jnp.einsum('bhck,bhke->bhce', qk_n, v_new)
        state = state * gl_n[..., None, None] + jnp.einsum('bhcd,bhce->bhde', kd_n, v_new)
        return state, o

    xs = tuple(t.transpose(2, 0, 1, 3, 4) for t in (qk, u, w, q_dec, k_dec)) + (g_last.transpose(2, 0, 1),)
    state0 = jnp.zeros((b_sz, n_h, dk, dv), dtype=jnp.float32)
    _, o = lax.scan(step, state0, xs)
    return o.transpose(1, 0, 3, 2, 4).reshape(b_sz, s_len, n_h, dv)


def gated_deltanet(h, w_in, conv_w, a_log, dt_bias, norm_g, w_out):
    b_sz, s_len, _ = h.shape
    proj = h @ w_in
    qkv, z, b_gate, a_gate = jnp.split(
        proj, [2 * GDN_KEY_DIM + GDN_VALUE_DIM, 2 * GDN_KEY_DIM + 2 * GDN_VALUE_DIM,
               2 * GDN_KEY_DIM + 2 * GDN_VALUE_DIM + GDN_V_HEADS], axis=-1)
    qkv = jax.nn.silu(causal_dwconv(qkv, conv_w))
    q, k, v = jnp.split(qkv, [GDN_KEY_DIM, 2 * GDN_KEY_DIM], axis=-1)
    q = l2_norm(q.reshape(b_sz, s_len, GDN_QK_HEADS, GDN_HEAD_DIM).astype(jnp.float32))
    k = l2_norm(k.reshape(b_sz, s_len, GDN_QK_HEADS, GDN_HEAD_DIM).astype(jnp.float32))
    q = jnp.repeat(q, GDN_REPEAT, axis=2)
    k = jnp.repeat(k, GDN_REPEAT, axis=2)
    v = v.reshape(b_sz, s_len, GDN_V_HEADS, GDN_HEAD_DIM).astype(jnp.float32)
    beta = jax.nn.sigmoid(b_gate.astype(jnp.float32))
    g = -jnp.exp(a_log.astype(jnp.float32)) * jax.nn.softplus(a_gate.astype(jnp.float32) + dt_bias.astype(jnp.float32))
    o = gated_delta_rule(q, k, v, g, beta)
    o = rms_norm(o, norm_g) * jax.nn.silu(z.reshape(b_sz, s_len, GDN_V_HEADS, GDN_HEAD_DIM).astype(jnp.float32))
    return o.reshape(b_sz, s_len, GDN_VALUE_DIM).astype(h.dtype) @ w_out


def conv_ffn(h, w_up, conv_w, conv_b, w_down):
    u = causal_dwconv(h @ w_up, conv_w) + conv_b
    val, gate = jnp.split(u, 2, axis=-1)
    return (jax.nn.silu(gate) * val) @ w_down


def setup_inputs(seed: int = 0) -> dict:
    key = jax.random.key(seed)
    ks = jax.random.split(key, 24)
    f32 = jnp.float32

    def nrm(k, shape, scale):
        return jax.random.normal(k, shape, dtype=f32) * scale

    def gain(k, shape):
        return 1.0 + 0.02 * jax.random.normal(k, shape, dtype=f32)

    na, nb = N_ATTN_LAYERS, N_GDN_LAYERS
    dt = jnp.exp(jax.random.uniform(ks[16], (nb, GDN_V_HEADS), dtype=f32,
                                    minval=math.log(0.001), maxval=math.log(0.1)))
    return {
        'x': nrm(ks[0], (BATCH, SEQ, D_MODEL), 1.0),
        'mixer_norm': gain(ks[1], (DEPTH, D_MODEL)),
        'ffn_norm': gain(ks[2], (DEPTH, D_MODEL)),
        'diff_w_qkv': nrm(ks[3], (na, D_MODEL, 2 * DIFF_QK_DIM + DIFF_V_DIM), D_MODEL ** -0.5),
        'diff_q_norm': gain(ks[4], (na, DIFF_HEAD_DIM)),
        'diff_k_norm': gain(ks[5], (na, DIFF_HEAD_DIM)),
        'diff_lambda_q1': nrm(ks[6], (na, DIFF_HEAD_DIM), LAMBDA_STD),
        'diff_lambda_k1': nrm(ks[7], (na, DIFF_HEAD_DIM), LAMBDA_STD),
        'diff_lambda_q2': nrm(ks[8], (na, DIFF_HEAD_DIM), LAMBDA_STD),
        'diff_lambda_k2': nrm(ks[9], (na, DIFF_HEAD_DIM), LAMBDA_STD),
        'diff_subln': gain(ks[10], (na, 2 * DIFF_HEAD_DIM)),
        'diff_w_o': nrm(ks[11], (na, DIFF_V_DIM, D_MODEL), DIFF_V_DIM ** -0.5),
        'gdn_w_in': nrm(ks[12], (nb, D_MODEL, GDN_IN_DIM), D_MODEL ** -0.5),
        'gdn_conv_w': nrm(ks[13], (nb, GDN_CONV, 2 * GDN_KEY_DIM + GDN_VALUE_DIM), GDN_CONV ** -0.5),
        'gdn_A_log': jnp.log(jax.random.uniform(ks[14], (nb, GDN_V_HEADS), dtype=f32, minval=1.0, maxval=16.0)),
        'gdn_dt_bias': dt + jnp.log(-jnp.expm1(-dt)),
        'gdn_norm': gain(ks[15], (nb, GDN_HEAD_DIM)),
        'gdn_w_out': nrm(ks[17], (nb, GDN_VALUE_DIM, D_MODEL), GDN_VALUE_DIM ** -0.5),
        'ffn_w_up': nrm(ks[18], (DEPTH, D_MODEL, 2 * D_FF), D_MODEL ** -0.5),
        'ffn_conv_w': nrm(ks[19], (DEPTH, FFN_CONV, 2 * D_FF), FFN_CONV ** -0.5),
        'ffn_conv_b': nrm(ks[20], (DEPTH, 2 * D_FF), 0.02),
        'ffn_w_down': nrm(ks[21], (DEPTH, D_FF, D_MODEL), D_FF ** -0.5),
    }


def reference(x, mixer_norm, ffn_norm, diff_w_qkv, diff_q_norm, diff_k_norm, diff_lambda_q1,
              diff_lambda_k1, diff_lambda_q2, diff_lambda_k2, diff_subln, diff_w_o, gdn_w_in,
              gdn_conv_w, gdn_A_log, gdn_dt_bias, gdn_norm, gdn_w_out, ffn_w_up, ffn_conv_w,
              ffn_conv_b, ffn_w_down):
    for i in range(DEPTH):
        h = rms_norm(x, mixer_norm[i])
        j = i // N_MIXERS
        if i % N_MIXERS == 0:
            lambda_init = 0.8 - 0.6 * math.exp(-0.3 * i)
            x = x + diff_attention(h, diff_w_qkv[j], diff_q_norm[j], diff_k_norm[j], diff_lambda_q1[j],
                                   diff_lambda_k1[j], diff_lambda_q2[j], diff_lambda_k2[j],
                                   diff_subln[j], diff_w_o[j], lambda_init)
        else:
            x = x + gated_deltanet(h, gdn_w_in[j], gdn_conv_w[j], gdn_A_log[j], gdn_dt_bias[j],
                                   gdn_norm[j], gdn_w_out[j])
        x = x + conv_ffn(rms_norm(x, ffn_norm[i]), ffn_w_up[i], ffn_conv_w[i], ffn_conv_b[i], ffn_w_down[i])
    return x
```

```python
import functools
import math

import jax
import jax.numpy as jnp
from jax import lax
from jax.experimental import pallas as pl
from jax.experimental.pallas import tpu as pltpu

F32 = jnp.float32
BF16 = jnp.bfloat16

RMS_EPS = 1e-6
CHUNK = 64
V7X_VMEM_BYTES = 64 * 1024 * 1024
V7X_LANES = 128
V7X_SUBLANES = 8
COMPILER_TEMP_BYTES = 12 * 1024 * 1024

ROW_TILE = 1024
COL_TILE = 512
ATTN_TILE = 512
DELTA_PAIRS = 4
DELTA_CHUNKS = 8
GATE_ROWS = 256

_NT = (((1,), (1,)), ((), ()))
_TN = (((0,), (0,)), ((), ()))


def _params(semantics, block_bytes):
    limit = min(int(block_bytes) + COMPILER_TEMP_BYTES, V7X_VMEM_BYTES - (4 << 20))
    return pltpu.CompilerParams(dimension_semantics=semantics, vmem_limit_bytes=limit)


def _rms_rows(x, gain):
    return x * lax.rsqrt(jnp.mean(x * x, axis=-1, keepdims=True) + RMS_EPS) * gain


def _silu(x):
    return x * jax.nn.sigmoid(x)


def _causal_conv(u, carry, w_ref):
    kw = w_ref.shape[0]
    out = u * w_ref[kw - 1:kw, :]
    top16 = jnp.concatenate([carry, u[0:V7X_SUBLANES]], axis=0)
    for s in range(1, kw):
        rolled = pltpu.roll(u, s, axis=0)
        top = pltpu.roll(top16, s, axis=0)[V7X_SUBLANES:]
        shifted = jnp.concatenate([top, rolled[V7X_SUBLANES:]], axis=0)
        out = out + shifted * w_ref[kw - 1 - s:kw - s, :]
    return out


def _normalize_once(x_ref, g_ref, h_scr):
    @pl.when(pl.program_id(1) == 0)
    def _():
        h_scr[...] = _rms_rows(x_ref[...], g_ref[...]).astype(BF16)


def _qkv_kernel(x_ref, g_ref, w_ref, qn_ref, kn_ref, o_ref, h_scr, *, n_q_tiles, n_k_tiles, head_dim, q_scale):
    j = pl.program_id(1)
    _normalize_once(x_ref, g_ref, h_scr)
    acc = jnp.dot(h_scr[...], w_ref[...].astype(BF16), preferred_element_type=F32)

    def head_norm(gain_ref, scale):
        for c in range(acc.shape[1] // head_dim):
            cols = slice(c * head_dim, (c + 1) * head_dim)
            o_ref[:, cols] = (_rms_rows(acc[:, cols], gain_ref[...]) * scale).astype(o_ref.dtype)

    @pl.when(j < n_q_tiles)
    def _():
        head_norm(qn_ref, q_scale)

    @pl.when(jnp.logical_and(j >= n_q_tiles, j < n_q_tiles + n_k_tiles))
    def _():
        head_norm(kn_ref, 1.0)

    @pl.when(j >= n_q_tiles + n_k_tiles)
    def _():
        o_ref[...] = acc.astype(o_ref.dtype)


def _plain_proj_kernel(x_ref, g_ref, w_ref, o_ref, h_scr):
    _normalize_once(x_ref, g_ref, h_scr)
    o_ref[...] = jnp.dot(h_scr[...], w_ref[...].astype(BF16), preferred_element_type=F32).astype(o_ref.dtype)


def _ffn_up_kernel(x_ref, g_ref, w_ref, cw_ref, cb_ref, o_ref, h_scr, carry_scr):
    i, j = pl.program_id(0), pl.program_id(1)
    _normalize_once(x_ref, g_ref, h_scr)

    @pl.when(i == 0)
    def _():
        carry_scr[j] = jnp.zeros(carry_scr.shape[1:], F32)

    u = jnp.dot(h_scr[...], w_ref[...].astype(BF16), preferred_element_type=F32)
    y = _causal_conv(u, carry_scr[j], cw_ref) + cb_ref[...]
    carry_scr[j] = u[u.shape[0] - V7X_SUBLANES:]
    o_ref[...] = y.astype(o_ref.dtype)


def _gdn_in_kernel(x_ref, g_ref, w_ref, cw_ref, o_ref, h_scr, carry_scr, *, n_q_tiles, n_qk_tiles, n_conv_tiles,
                   head_dim, q_scale):
    i, j = pl.program_id(0), pl.program_id(1)
    _normalize_once(x_ref, g_ref, h_scr)
    u = jnp.dot(h_scr[...], w_ref[...].astype(BF16), preferred_element_type=F32)

    def conv_silu():
        @pl.when(i == 0)
        def _():
            carry_scr[j] = jnp.zeros(carry_scr.shape[1:], F32)
        y = _causal_conv(u, carry_scr[j], cw_ref)
        carry_scr[j] = u[u.shape[0] - V7X_SUBLANES:]
        return _silu(y)

    def store_l2(y, scale):
        for c in range(y.shape[1] // head_dim):
            cols = slice(c * head_dim, (c + 1) * head_dim)
            blk = y[:, cols]
            inv = lax.rsqrt(jnp.sum(blk * blk, axis=-1, keepdims=True) + RMS_EPS)
            o_ref[:, cols] = (blk * inv * scale).astype(o_ref.dtype)

    @pl.when(j < n_q_tiles)
    def _():
        store_l2(conv_silu(), q_scale)

    @pl.when(jnp.logical_and(j >= n_q_tiles, j < n_qk_tiles))
    def _():
        store_l2(conv_silu(), 1.0)

    @pl.when(jnp.logical_and(j >= n_qk_tiles, j < n_conv_tiles))
    def _():
        o_ref[...] = conv_silu().astype(o_ref.dtype)

    @pl.when(j >= n_conv_tiles)
    def _():
        o_ref[...] = u.astype(o_ref.dtype)


def _proj_call(name, kernel, x, gain, w3, layer, n_cols, extra_inputs, extra_specs, out_dtype, scratch,
               col_tile=COL_TILE):
    s_len, d = x.shape
    tm, tn = min(ROW_TILE, s_len), col_tile
    grid = (s_len // tm, pl.cdiv(n_cols, tn))
    in_specs = [
        pl.BlockSpec((tm, d), lambda i, j: (i, 0)),
        pl.BlockSpec((1, d), lambda i, j: (0, 0)),
        pl.BlockSpec((None, d, tn), lambda i, j: (layer, 0, j)),
    ] + list(extra_specs)
    out_bytes = jnp.dtype(out_dtype).itemsize
    block_bytes = 2 * tm * d * 4 + tm * d * 2 + 2 * d * tn * 4 + d * tn * 2 + 2 * tm * tn * out_bytes + 3 * tm * tn * 4
    return pl.pallas_call(
        kernel,
        name=name,
        out_shape=jax.ShapeDtypeStruct((s_len, n_cols), out_dtype),
        grid=grid,
        in_specs=in_specs,
        out_specs=pl.BlockSpec((tm, tn), lambda i, j: (i, j)),
        scratch_shapes=[pltpu.VMEM((tm, d), BF16)] + list(scratch),
        compiler_params=_params(("arbitrary", "arbitrary"), block_bytes),
    )(x, gain.reshape(1, d), w3, *extra_inputs)


def _mm_res_kernel(a_ref, w_ref, r_ref, o_ref):
    o_ref[...] = r_ref[...] + jnp.dot(a_ref[...], w_ref[...].astype(BF16), preferred_element_type=F32)


def _mm_res_call(a, w3, layer, res):
    s_len, k = a.shape
    n = w3.shape[2]
    tm, tn = min(ROW_TILE, s_len), COL_TILE
    block_bytes = 2 * tm * k * 2 + 2 * k * tn * 4 + k * tn * 2 + 4 * tm * tn * 4 + tm * tn * 4
    return pl.pallas_call(
        _mm_res_kernel,
        name="out_proj_residual",
        out_shape=jax.ShapeDtypeStruct((s_len, n), F32),
        grid=(s_len // tm, n // tn),
        in_specs=[
            pl.BlockSpec((tm, k), lambda i, j: (i, 0)),
            pl.BlockSpec((None, k, tn), lambda i, j: (layer, 0, j)),
            pl.BlockSpec((tm, tn), lambda i, j: (i, j)),
        ],
        out_specs=pl.BlockSpec((tm, tn), lambda i, j: (i, j)),
        compiler_params=_params(("arbitrary", "arbitrary"), block_bytes),
    )(a, w3, res)


def _ffn_down_kernel(uv_ref, ug_ref, w_ref, r_ref, o_ref, g_scr):
    @pl.when(pl.program_id(1) == 0)
    def _():
        d_ff = g_scr.shape[1]
        for c0 in range(0, d_ff, COL_TILE):
            cols = slice(c0, min(c0 + COL_TILE, d_ff))
            g_scr[:, cols] = (_silu(ug_ref[:, cols].astype(F32)) * uv_ref[:, cols].astype(F32)).astype(BF16)

    o_ref[...] = r_ref[...] + jnp.dot(g_scr[...], w_ref[...], preferred_element_type=F32)


def _ffn_down_call(u, w_down, res):
    s_len = u.shape[0]
    d_ff, n = w_down.shape
    tm, tn = min(ROW_TILE // 2, s_len), COL_TILE
    block_bytes = 4 * tm * d_ff * 2 + tm * d_ff * 2 + 2 * d_ff * tn * 2 + 4 * tm * tn * 4
    return pl.pallas_call(
        _ffn_down_kernel,
        name="ffn_down_residual",
        out_shape=jax.ShapeDtypeStruct((s_len, n), F32),
        grid=(s_len // tm, n // tn),
        in_specs=[
            pl.BlockSpec((tm, d_ff), lambda i, j: (i, 0)),
            pl.BlockSpec((tm, d_ff), lambda i, j: (i, 1)),
            pl.BlockSpec((d_ff, tn), lambda i, j: (0, j)),
            pl.BlockSpec((tm, tn), lambda i, j: (i, j)),
        ],
        out_specs=pl.BlockSpec((tm, tn), lambda i, j: (i, j)),
        scratch_shapes=[pltpu.VMEM((tm, d_ff), BF16)],
        compiler_params=_params(("arbitrary", "arbitrary"), block_bytes),
    )(u, u, w_down, res)


def _attn_kernel(slopes_ref, q_ref, k_ref, v_ref, lq1_ref, lk1_ref, lq2_ref, lk2_ref, subln_ref, o_ref,
                 m_scr, l_scr, acc_scr, *, tile, head_dim, lambda_init):
    h, qi = pl.program_id(0), pl.program_id(1)
    slope = slopes_ref[h]
    q0 = pl.multiple_of(qi * tile, tile)

    row = lax.broadcasted_iota(jnp.int32, (tile, tile), 0)
    col = lax.broadcasted_iota(jnp.int32, (tile, tile), 1)
    dist = jnp.abs(row - col).astype(F32)
    allowed = (col // CHUNK) <= (row // CHUNK)
    k_diag = k_ref[pl.ds(q0, tile), :]
    v_diag = v_ref[pl.ds(q0, tile), :]
    for m in range(2):
        cols = slice(m * head_dim, (m + 1) * head_dim)
        s = lax.dot_general(q_ref[:, cols], k_diag[:, cols], _NT, preferred_element_type=F32)
        z = jnp.where(allowed, s - slope * dist, -jnp.inf)
        mx = jnp.max(z, axis=-1, keepdims=True)
        p = jnp.exp(z - mx)
        m_scr[m] = mx
        l_scr[m] = jnp.sum(p, axis=-1, keepdims=True)
        acc_scr[m] = jnp.dot(p.astype(BF16), v_diag, preferred_element_type=F32)

    col_bias = slope * lax.broadcasted_iota(jnp.int32, (1, tile), 1).astype(F32)
    q_pos = (lax.broadcasted_iota(jnp.int32, (tile, 1), 0) + q0).astype(F32)

    def kv_step(kj, carry):
        k0 = pl.multiple_of(kj * tile, tile)
        k_blk = k_ref[pl.ds(k0, tile), :]
        v_blk = v_ref[pl.ds(k0, tile), :]
        c = -slope * (q_pos - k0.astype(F32))
        for m in range(2):
            cols = slice(m * head_dim, (m + 1) * head_dim)
            s = lax.dot_general(q_ref[:, cols], k_blk[:, cols], _NT, preferred_element_type=F32)
            z = s + col_bias
            m_old = m_scr[m]
            m_new = jnp.maximum(m_old, jnp.max(z, axis=-1, keepdims=True) + c)
            p = jnp.exp(z - (m_new - c))
            alpha = jnp.exp(m_old - m_new)
            l_scr[m] = alpha * l_scr[m] + jnp.sum(p, axis=-1, keepdims=True)
            acc_scr[m] = alpha * acc_scr[m] + jnp.dot(p.astype(BF16), v_blk, preferred_element_type=F32)
            m_scr[m] = m_new
        return carry

    lax.fori_loop(0, qi, kv_step, 0)

    lam = (jnp.exp(jnp.sum(lq1_ref[...] * lk1_ref[...], axis=-1, keepdims=True))
           - jnp.exp(jnp.sum(lq2_ref[...] * lk2_ref[...], axis=-1, keepdims=True)) + lambda_init)
    o = acc_scr[0] / l_scr[0] - lam * (acc_scr[1] / l_scr[1])
    o_ref[...] = (_rms_rows(o, subln_ref[...]) * (1.0 - lambda_init)).astype(o_ref.dtype)


def _attn_call(qkv, slopes, lq1, lk1, lq2, lk2, subln, *, n_heads, head_dim, lambda_init):
    s_len = qkv.shape[0]
    tile = min(ATTN_TILE, s_len)
    hw = 2 * head_dim
    vec = lambda a: a.reshape(1, -1)
    vec_spec = lambda n: pl.BlockSpec((1, n), lambda h, qi: (0, 0))
    block_bytes = 2 * (2 * s_len * hw * 2) + 4 * tile * hw * 2 + 2 * tile * hw * 4 + 4 * tile * V7X_LANES * 4 \
        + 6 * tile * tile * 4
    return pl.pallas_call(
        functools.partial(_attn_kernel, tile=tile, head_dim=head_dim, lambda_init=lambda_init),
        name="diff_attention",
        out_shape=jax.ShapeDtypeStruct((s_len, n_heads * hw), BF16),
        grid=(n_heads, s_len // tile),
        in_specs=[
            pl.BlockSpec(memory_space=pltpu.SMEM),
            pl.BlockSpec((tile, hw), lambda h, qi: (qi, h)),
            pl.BlockSpec((s_len, hw), lambda h, qi: (0, n_heads + h)),
            pl.BlockSpec((s_len, hw), lambda h, qi: (0, 2 * n_heads + h)),
            vec_spec(head_dim), vec_spec(head_dim), vec_spec(head_dim), vec_spec(head_dim), vec_spec(hw),
        ],
        out_specs=pl.BlockSpec((tile, hw), lambda h, qi: (qi, h)),
        scratch_shapes=[
            pltpu.VMEM((2, tile, 1), F32),
            pltpu.VMEM((2, tile, 1), F32),
            pltpu.VMEM((2, tile, hw), F32),
        ],
        compiler_params=_params(("arbitrary", "arbitrary"), block_bytes),
    )(slopes, qkv, qkv, qkv, vec(lq1), vec(lk1), vec(lq2), vec(lk2), vec(subln))


def _gates_kernel(ba_ref, alog_ref, dtb_ref, beta_ref, gc_ref):
    rows = ba_ref.shape[0]
    beta_ref[...] = jax.nn.sigmoid(ba_ref[:, :V7X_LANES])
    g = -jnp.exp(alog_ref[...]) * jax.nn.softplus(ba_ref[:, V7X_LANES:] + dtb_ref[...])
    r = lax.broadcasted_iota(jnp.int32, (rows, rows), 0)
    c = lax.broadcasted_iota(jnp.int32, (rows, rows), 1)
    tri = jnp.where(r // CHUNK == c // CHUNK, jnp.where(c <= r, 1.0, 0.0), 0.0).astype(F32)
    gc_ref[...] = jnp.dot(tri, g, preferred_element_type=F32, precision=lax.Precision.HIGHEST)


def _gates_call(ba, a_log_row, dt_bias_row):
    s_len = ba.shape[0]
    tm = min(GATE_ROWS, s_len)
    out = jax.ShapeDtypeStruct((s_len, V7X_LANES), F32)
    return pl.pallas_call(
        _gates_kernel,
        name="gdn_gates",
        out_shape=(out, out),
        grid=(s_len // tm,),
        in_specs=[
            pl.BlockSpec((tm, 2 * V7X_LANES), lambda i: (i, 0)),
            pl.BlockSpec((1, V7X_LANES), lambda i: (0, 0)),
            pl.BlockSpec((1, V7X_LANES), lambda i: (0, 0)),
        ],
        out_specs=(pl.BlockSpec((tm, V7X_LANES), lambda i: (i, 0)), pl.BlockSpec((tm, V7X_LANES), lambda i: (i, 0))),
        compiler_params=_params(("arbitrary",), 8 * tm * 2 * V7X_LANES * 4 + 4 * tm * tm * 4),
    )(ba, a_log_row, dt_bias_row)


def _split_bf16(x):
    hi = x.astype(BF16)
    return hi, (x - hi.astype(F32)).astype(BF16)


def _delta_kernel(q_ref, k_ref, v_ref, z_ref, beta_ref, gc_ref, gct_ref, ng_ref, o_ref, state_scr, *,
                  n_pairs, n_chunks, head_dim):
    hd = head_dim

    @pl.when(pl.program_id(1) == 0)
    def _():
        state_scr[...] = jnp.zeros(state_scr.shape, F32)

    ri = lax.broadcasted_iota(jnp.int32, (CHUNK, 2 * CHUNK), 0)
    lane = lax.broadcasted_iota(jnp.int32, (CHUNK, 2 * CHUNK), 1)
    left = lane < CHUNK
    ci = jnp.where(left, lane, lane - CHUNK)
    tril = ci <= ri
    strict = ci < ri
    eye = jnp.where(ci == ri, 1.0, 0.0).astype(F32)
    state_left = lax.broadcasted_iota(jnp.int32, (hd, 2 * hd), 1) < hd

    def block_diag(y, keep_left):
        zero = jnp.zeros_like(y)
        return jnp.concatenate([jnp.where(keep_left, y, zero), jnp.where(keep_left, zero, y)], axis=0)

    def pair_matmul(x, y):
        xh, xl = _split_bf16(x)
        yh, yl = _split_bf16(y)
        bh = block_diag(yh, left)
        main = jnp.dot(jnp.concatenate([xh, xl], axis=1), jnp.concatenate([bh, bh], axis=0),
                       preferred_element_type=F32)
        return main + jnp.dot(xh, block_diag(yl, left), preferred_element_type=F32)

    def unit_lower_inverse(lmat):
        x = pair_matmul(lmat, lmat)
        p = eye - lmat
        power = 2
        while power * 2 < CHUNK:
            both = pair_matmul(jnp.concatenate([p, x], axis=0), x)
            p = p + both[:CHUNK]
            x = both[CHUNK:]
            power *= 2
        return p + pair_matmul(p, x)

    def chunk_step(c, carry):
        rows = pl.ds(pl.multiple_of(c * CHUNK, CHUNK), CHUNK)
        beta = beta_ref[rows, :]
        g_col = gc_ref[rows, :]
        g_rows = gct_ref[c]
        exp_g = jnp.exp(g_col)
        zeros_cd = jnp.zeros((CHUNK, hd), F32)
        for p in range(n_pairs):
            pair_cols = slice(p * hd, (p + 1) * hd)
            wide_cols = slice(2 * p * hd, 2 * (p + 1) * hd)
            ha, hb = 2 * p, 2 * p + 1
            q = q_ref[rows, pair_cols]
            k = k_ref[rows, pair_cols]
            kq = lax.dot_general(jnp.concatenate([k, q], axis=0), jnp.concatenate([k, k], axis=0), _NT,
                                 preferred_element_type=F32)
            kf, qf = k.astype(F32), q.astype(F32)
            g_row = g_rows[p:p + 1, :]
            b_pk = jnp.where(left, beta[:, ha:ha + 1], beta[:, hb:hb + 1])
            g_pk = jnp.where(left, g_col[:, ha:ha + 1], g_col[:, hb:hb + 1])
            decay = jnp.exp(jnp.where(tril, g_pk - g_row, -jnp.inf))
            lmat = jnp.where(strict, kq[:CHUNK] * b_pk * decay, 0.0)
            qk_decay = kq[CHUNK:] * decay
            tmat = unit_lower_inverse(lmat).astype(BF16)

            b_a, b_b = beta[:, ha:ha + 1], beta[:, hb:hb + 1]
            e_a, e_b = exp_g[:, ha:ha + 1], exp_g[:, hb:hb + 1]
            v_pk = v_ref[rows, wide_cols].astype(F32)
            vb_a, vb_b = v_pk[:, :hd] * b_a, v_pk[:, hd:] * b_b
            kbg_a, kbg_b = kf * (b_a * e_a), kf * (b_b * e_b)
            rhs = jnp.concatenate([jnp.concatenate([vb_a, zeros_cd, kbg_a, zeros_cd], axis=1),
                                   jnp.concatenate([zeros_cd, vb_b, zeros_cd, kbg_b], axis=1)], axis=0)
            uw = jnp.dot(tmat, rhs.astype(BF16), preferred_element_type=F32)
            u_pk, w_pk = uw[:, :2 * hd], uw[:, 2 * hd:]

            s_old = state_scr[p]
            q_dec = jnp.concatenate([qf * e_a, qf * e_b], axis=1)
            wq = jnp.dot(jnp.concatenate([w_pk, q_dec], axis=0).astype(BF16),
                         block_diag(s_old.astype(BF16), state_left), preferred_element_type=F32)
            v_new = u_pk - wq[:CHUNK]
            vn = v_new.astype(BF16)
            zeros_b = jnp.zeros((CHUNK, hd), BF16)
            vn_diag = jnp.concatenate([jnp.concatenate([vn[:, :hd], zeros_b], axis=1),
                                       jnp.concatenate([zeros_b, vn[:, hd:]], axis=1)], axis=0)
            o_pk = wq[CHUNK:] + jnp.dot(qk_decay.astype(BF16), vn_diag, preferred_element_type=F32)

            ge_a, ge_b = g_row[:, CHUNK - 1:CHUNK], g_row[:, 2 * CHUNK - 1:2 * CHUNK]
            kd_a = (kf * jnp.exp(ge_a - g_col[:, ha:ha + 1])).astype(BF16)
            kd_b = (kf * jnp.exp(ge_b - g_col[:, hb:hb + 1])).astype(BF16)
            ds_a = lax.dot_general(kd_a, vn[:, :hd], _TN, preferred_element_type=F32)
            ds_b = lax.dot_general(kd_b, vn[:, hd:], _TN, preferred_element_type=F32)
            state_scr[p] = jnp.concatenate([s_old[:, :hd] * jnp.exp(ge_a) + ds_a,
                                            s_old[:, hd:] * jnp.exp(ge_b) + ds_b], axis=1)

            z_pk = z_ref[rows, wide_cols].astype(F32)
            gated = [_rms_rows(o_pk[:, h * hd:(h + 1) * hd], ng_ref[...]) * _silu(z_pk[:, h * hd:(h + 1) * hd])
                     for h in range(2)]
            o_ref[rows, wide_cols] = jnp.concatenate(gated, axis=1).astype(o_ref.dtype)
        return carry

    lax.fori_loop(0, n_chunks, chunk_step, 0)


def _delta_call(qkvz, beta_g, gc_g, gct_g, norm_g, *, n_qk_heads, head_dim):
    s_len = qkvz.shape[0]
    n_groups = n_qk_heads // DELTA_PAIRS
    heads = 2 * DELTA_PAIRS
    rows = DELTA_CHUNKS * CHUNK
    qk_w, v_w = DELTA_PAIRS * head_dim, heads * head_dim
    key_dim, value_dim = n_qk_heads * head_dim, 2 * n_qk_heads * head_dim
    k_off, v_off, z_off = key_dim // qk_w, 2 * key_dim // v_w, (2 * key_dim + value_dim) // v_w
    block_bytes = 2 * rows * (2 * qk_w + 3 * v_w) * 2 + 6 * rows * V7X_LANES * 4 + heads * head_dim * head_dim * 4
    return pl.pallas_call(
        functools.partial(_delta_kernel, n_pairs=DELTA_PAIRS, n_chunks=DELTA_CHUNKS, head_dim=head_dim),
        name="gdn_delta_rule",
        out_shape=jax.ShapeDtypeStruct((s_len, value_dim), BF16),
        grid=(n_groups, s_len // rows),
        in_specs=[
            pl.BlockSpec((rows, qk_w), lambda g, b: (b, g)),
            pl.BlockSpec((rows, qk_w), lambda g, b: (b, k_off + g)),
            pl.BlockSpec((rows, v_w), lambda g, b: (b, v_off + g)),
            pl.BlockSpec((rows, v_w), lambda g, b: (b, z_off + g)),
            pl.BlockSpec((None, rows, heads), lambda g, b: (g, b, 0)),
            pl.BlockSpec((None, rows, heads), lambda g, b: (g, b, 0)),
            pl.BlockSpec((None, DELTA_CHUNKS, DELTA_PAIRS, 2 * CHUNK), lambda g, b: (g, b, 0, 0)),
            pl.BlockSpec((1, head_dim), lambda g, b: (0, 0)),
        ],
        out_specs=pl.BlockSpec((rows, v_w), lambda g, b: (b, g)),
        scratch_shapes=[pltpu.VMEM((DELTA_PAIRS, head_dim, 2 * head_dim), F32)],
        compiler_params=_params(("arbitrary", "arbitrary"), block_bytes),
    )(qkvz, qkvz, qkvz, qkvz, beta_g, gc_g, gct_g, norm_g.reshape(1, head_dim))


def _diff_attention_layer(x, gain, w_qkv, layer, q_norm, k_norm, lq1, lk1, lq2, lk2, subln, w_o, lambda_init):
    d = x.shape[1]
    head_dim = q_norm.shape[0]
    n_heads = w_o.shape[1] // subln.shape[0]
    qk_dim = 2 * n_heads * head_dim
    n_cols = w_qkv.shape[2]
    vec_spec = pl.BlockSpec((1, head_dim), lambda i, j: (0, 0))
    qkv = _proj_call(
        "attn_qkv_proj", functools.partial(_qkv_kernel, n_q_tiles=qk_dim // COL_TILE, n_k_tiles=qk_dim // COL_TILE,
                          head_dim=head_dim, q_scale=head_dim ** -0.5),
        x, gain, w_qkv, layer, n_cols, (q_norm.reshape(1, -1), k_norm.reshape(1, -1)), (vec_spec, vec_spec), BF16, ())
    slopes = jnp.exp2(-8.0 * jnp.arange(1, n_heads + 1, dtype=F32) / n_heads)
    o = _attn_call(qkv, slopes, lq1, lk1, lq2, lk2, subln, n_heads=n_heads, head_dim=head_dim,
                   lambda_init=lambda_init)
    return _mm_res_call(o, w_o, layer, x)


def _gated_deltanet_layer(x, gain, w_in, layer, conv_w, a_log, dt_bias, norm_g, w_out):
    s_len, d = x.shape
    head_dim = norm_g.shape[0]
    n_v_heads = a_log.shape[0]
    n_qk_heads = n_v_heads // 2
    key_dim, value_dim = n_qk_heads * head_dim, n_v_heads * head_dim
    conv_dim = 2 * key_dim + value_dim
    main_cols = conv_dim + value_dim
    n_conv_tiles = conv_dim // COL_TILE
    cw_spec = pl.BlockSpec((None, conv_w.shape[1], COL_TILE), lambda i, j: (layer, 0, jnp.minimum(j, n_conv_tiles - 1)))
    qkvz = _proj_call(
        "gdn_in_proj", functools.partial(_gdn_in_kernel, n_q_tiles=key_dim // COL_TILE, n_qk_tiles=2 * key_dim // COL_TILE,
                          n_conv_tiles=n_conv_tiles, head_dim=head_dim, q_scale=head_dim ** -0.5),
        x, gain, w_in, layer, main_cols, (conv_w,), (cw_spec,), BF16,
        (pltpu.VMEM((n_conv_tiles, V7X_SUBLANES, COL_TILE), F32),))

    pad = V7X_LANES - n_v_heads
    w_b = jnp.pad(w_in[layer][:, main_cols:main_cols + n_v_heads], ((0, 0), (0, pad)))
    w_a = jnp.pad(w_in[layer][:, main_cols + n_v_heads:], ((0, 0), (0, pad)))
    w_ba = jnp.concatenate([w_b, w_a], axis=1)[None]
    ba = _proj_call("gdn_gate_proj", _plain_proj_kernel, x, gain, w_ba, 0, 2 * V7X_LANES, (), (), F32, (), col_tile=2 * V7X_LANES)
    beta, gc = _gates_call(ba, jnp.pad(a_log, (0, pad)).reshape(1, -1), jnp.pad(dt_bias, (0, pad)).reshape(1, -1))

    heads = 2 * DELTA_PAIRS
    n_groups = n_v_heads // heads
    beta_g = beta[:, :n_v_heads].reshape(s_len, n_groups, heads).transpose(1, 0, 2)
    gc_g = gc[:, :n_v_heads].reshape(s_len, n_groups, heads).transpose(1, 0, 2)
    gct_g = gc[:, :n_v_heads].reshape(s_len // CHUNK, CHUNK, n_groups, DELTA_PAIRS, 2).transpose(2, 0, 3, 4, 1)
    gct_g = gct_g.reshape(n_groups, s_len // CHUNK, DELTA_PAIRS, 2 * CHUNK)
    o = _delta_call(qkvz, beta_g, gc_g, gct_g, norm_g, n_qk_heads=n_qk_heads, head_dim=head_dim)
    return _mm_res_call(o, w_out, layer, x)


def _conv_ffn_layer(x, gain, w_up, conv_w, conv_b, w_down, layer):
    n_cols = w_up.shape[2]
    n_tiles = pl.cdiv(n_cols, COL_TILE)
    cw_spec = pl.BlockSpec((None, conv_w.shape[1], COL_TILE), lambda i, j: (layer, 0, j))
    cb_spec = pl.BlockSpec((None, 1, COL_TILE), lambda i, j: (layer, 0, j))
    u = _proj_call("ffn_up_proj", _ffn_up_kernel, x, gain, w_up, layer, n_cols, (conv_w, conv_b[:, None, :]), (cw_spec, cb_spec),
                   BF16, (pltpu.VMEM((n_tiles, V7X_SUBLANES, COL_TILE), F32),))
    return _ffn_down_call(u, w_down[layer].astype(BF16), x)


def kernel(x, mixer_norm, ffn_norm, diff_w_qkv, diff_q_norm, diff_k_norm, diff_lambda_q1, diff_lambda_k1,
           diff_lambda_q2, diff_lambda_k2, diff_subln, diff_w_o, gdn_w_in, gdn_conv_w, gdn_A_log, gdn_dt_bias,
           gdn_norm, gdn_w_out, ffn_w_up, ffn_conv_w, ffn_conv_b, ffn_w_down):
    b_sz, s_len, d = x.shape
    depth = mixer_norm.shape[0]
    outs = []
    for b in range(b_sz):
        xb = x.reshape(s_len, d) if b_sz == 1 else x[b]
        for i in range(depth):
            j = i // 2
            if i % 2 == 0:
                lambda_init = 0.8 - 0.6 * math.exp(-0.3 * i)
                xb = _diff_attention_layer(xb, mixer_norm[i], diff_w_qkv, j, diff_q_norm[j], diff_k_norm[j],
                                           diff_lambda_q1[j], diff_lambda_k1[j], diff_lambda_q2[j],
                                           diff_lambda_k2[j], diff_subln[j], diff_w_o, lambda_init)
            else:
                xb = _gated_deltanet_layer(xb, mixer_norm[i], gdn_w_in, j, gdn_conv_w, gdn_A_log[j], gdn_dt_bias[j],
                                           gdn_norm[j], gdn_w_out)
            xb = _conv_ffn_layer(xb, ffn_norm[i], ffn_w_up, ffn_conv_w, ffn_conv_b, ffn_w_down, i)
        outs.append(xb)
    return outs[0].reshape(x.shape) if b_sz == 1 else jnp.stack(outs, axis=0)
```

```python
import functools
import math

import jax
import jax.numpy as jnp
from jax import lax
from jax.experimental import pallas as pl
from jax.experimental.pallas import tpu as pltpu

F32 = jnp.float32
BF16 = jnp.bfloat16

RMS_EPS = 1e-6
CHUNK = 64
V7X_VMEM_BYTES = 64 * 1024 * 1024
V7X_LANES = 128
V7X_SUBLANES = 8
COMPILER_TEMP_BYTES = 12 * 1024 * 1024

ROW_TILE = 1024
COL_TILE = 512
ATTN_TILE = 512
DELTA_PAIRS = 16
DELTA_CHUNKS = 4
GATE_ROWS = 256

_NT = (((1,), (1,)), ((), ()))
_TN = (((0,), (0,)), ((), ()))


def _params(semantics, block_bytes):
    limit = min(int(block_bytes) + COMPILER_TEMP_BYTES, V7X_VMEM_BYTES - (4 << 20))
    return pltpu.CompilerParams(dimension_semantics=semantics, vmem_limit_bytes=limit)


def _rms_rows(x, gain):
    return x * lax.rsqrt(jnp.mean(x * x, axis=-1, keepdims=True) + RMS_EPS) * gain


def _silu(x):
    return x * jax.nn.sigmoid(x)


def _causal_conv(u, carry, w_ref):
    kw = w_ref.shape[0]
    out = u * w_ref[kw - 1:kw, :]
    top16 = jnp.concatenate([carry, u[0:V7X_SUBLANES]], axis=0)
    for s in range(1, kw):
        rolled = pltpu.roll(u, s, axis=0)
        top = pltpu.roll(top16, s, axis=0)[V7X_SUBLANES:]
        shifted = jnp.concatenate([top, rolled[V7X_SUBLANES:]], axis=0)
        out = out + shifted * w_ref[kw - 1 - s:kw - s, :]
    return out


def _normalize_once(x_ref, g_ref, h_scr):
    @pl.when(pl.program_id(1) == 0)
    def _():
        h_scr[...] = _rms_rows(x_ref[...], g_ref[...]).astype(BF16)


def _qkv_kernel(x_ref, g_ref, w_ref, qn_ref, kn_ref, o_ref, h_scr, *, n_q_tiles, n_k_tiles, head_dim, q_scale):
    j = pl.program_id(1)
    _normalize_once(x_ref, g_ref, h_scr)
    acc = jnp.dot(h_scr[...], w_ref[...].astype(BF16), preferred_element_type=F32)

    def head_norm(gain_ref, scale):
        for c in range(acc.shape[1] // head_dim):
            cols = slice(c * head_dim, (c + 1) * head_dim)
            o_ref[:, cols] = (_rms_rows(acc[:, cols], gain_ref[...]) * scale).astype(o_ref.dtype)

    @pl.when(j < n_q_tiles)
    def _():
        head_norm(qn_ref, q_scale)

    @pl.when(jnp.logical_and(j >= n_q_tiles, j < n_q_tiles + n_k_tiles))
    def _():
        head_norm(kn_ref, 1.0)

    @pl.when(j >= n_q_tiles + n_k_tiles)
    def _():
        o_ref[...] = acc.astype(o_ref.dtype)


def _plain_proj_kernel(x_ref, g_ref, w_ref, o_ref, h_scr):
    _normalize_once(x_ref, g_ref, h_scr)
    o_ref[...] = jnp.dot(h_scr[...], w_ref[...].astype(BF16), preferred_element_type=F32).astype(o_ref.dtype)


def _ffn_up_kernel(x_ref, g_ref, w_ref, cw_ref, cb_ref, o_ref, h_scr, carry_scr):
    i, j = pl.program_id(0), pl.program_id(1)
    _normalize_once(x_ref, g_ref, h_scr)

    @pl.when(i == 0)
    def _():
        carry_scr[j] = jnp.zeros(carry_scr.shape[1:], F32)

    u = jnp.dot(h_scr[...], w_ref[...].astype(BF16), preferred_element_type=F32)
    y = _causal_conv(u, carry_scr[j], cw_ref) + cb_ref[...]
    carry_scr[j] = u[u.shape[0] - V7X_SUBLANES:]
    o_ref[...] = y.astype(o_ref.dtype)


def _gdn_in_kernel(x_ref, g_ref, w_ref, cw_ref, o_ref, h_scr, carry_scr, *, n_q_tiles, n_qk_tiles, n_conv_tiles,
                   head_dim, q_scale):
    i, j = pl.program_id(0), pl.program_id(1)
    _normalize_once(x_ref, g_ref, h_scr)
    u = jnp.dot(h_scr[...], w_ref[...].astype(BF16), preferred_element_type=F32)

    def conv_silu():
        @pl.when(i == 0)
        def _():
            carry_scr[j] = jnp.zeros(carry_scr.shape[1:], F32)
        y = _causal_conv(u, carry_scr[j], cw_ref)
        carry_scr[j] = u[u.shape[0] - V7X_SUBLANES:]
        return _silu(y)

    def store_l2(y, scale):
        for c in range(y.shape[1] // head_dim):
            cols = slice(c * head_dim, (c + 1) * head_dim)
            blk = y[:, cols]
            inv = lax.rsqrt(jnp.sum(blk * blk, axis=-1, keepdims=True) + RMS_EPS)
            o_ref[:, cols] = (blk * inv * scale).astype(o_ref.dtype)

    @pl.when(j < n_q_tiles)
    def _():
        store_l2(conv_silu(), q_scale)

    @pl.when(jnp.logical_and(j >= n_q_tiles, j < n_qk_tiles))
    def _():
        store_l2(conv_silu(), 1.0)

    @pl.when(jnp.logical_and(j >= n_qk_tiles, j < n_conv_tiles))
    def _():
        o_ref[...] = conv_silu().astype(o_ref.dtype)

    @pl.when(j >= n_conv_tiles)
    def _():
        o_ref[...] = u.astype(o_ref.dtype)


def _proj_call(name, kernel, x, gain, w3, layer, n_cols, extra_inputs, extra_specs, out_dtype, scratch,
               col_tile=COL_TILE):
    s_len, d = x.shape
    tm, tn = min(ROW_TILE, s_len), col_tile
    grid = (s_len // tm, pl.cdiv(n_cols, tn))
    in_specs = [
        pl.BlockSpec((tm, d), lambda i, j: (i, 0)),
        pl.BlockSpec((1, d), lambda i, j: (0, 0)),
        pl.BlockSpec((None, d, tn), lambda i, j: (layer, 0, j)),
    ] + list(extra_specs)
    out_bytes = jnp.dtype(out_dtype).itemsize
    block_bytes = 2 * tm * d * 4 + tm * d * 2 + 2 * d * tn * 4 + d * tn * 2 + 2 * tm * tn * out_bytes + 3 * tm * tn * 4
    return pl.pallas_call(
        kernel,
        name=name,
        out_shape=jax.ShapeDtypeStruct((s_len, n_cols), out_dtype),
        grid=grid,
        in_specs=in_specs,
        out_specs=pl.BlockSpec((tm, tn), lambda i, j: (i, j)),
        scratch_shapes=[pltpu.VMEM((tm, d), BF16)] + list(scratch),
        compiler_params=_params(("arbitrary", "arbitrary"), block_bytes),
    )(x, gain.reshape(1, d), w3, *extra_inputs)


def _mm_res_kernel(a_ref, w_ref, r_ref, o_ref):
    o_ref[...] = r_ref[...] + jnp.dot(a_ref[...], w_ref[...].astype(BF16), preferred_element_type=F32)


def _mm_res_call(a, w3, layer, res):
    s_len, k = a.shape
    n = w3.shape[2]
    tm, tn = min(ROW_TILE, s_len), COL_TILE
    block_bytes = 2 * tm * k * 2 + 2 * k * tn * 4 + k * tn * 2 + 4 * tm * tn * 4 + tm * tn * 4
    return pl.pallas_call(
        _mm_res_kernel,
        name="out_proj_residual",
        out_shape=jax.ShapeDtypeStruct((s_len, n), F32),
        grid=(s_len // tm, n // tn),
        in_specs=[
            pl.BlockSpec((tm, k), lambda i, j: (i, 0)),
            pl.BlockSpec((None, k, tn), lambda i, j: (layer, 0, j)),
            pl.BlockSpec((tm, tn), lambda i, j: (i, j)),
        ],
        out_specs=pl.BlockSpec((tm, tn), lambda i, j: (i, j)),
        compiler_params=_params(("arbitrary", "arbitrary"), block_bytes),
    )(a, w3, res)


def _ffn_down_kernel(uv_ref, ug_ref, w_ref, r_ref, o_ref, g_scr):
    @pl.when(pl.program_id(1) == 0)
    def _():
        d_ff = g_scr.shape[1]
        for c0 in range(0, d_ff, COL_TILE):
            cols = slice(c0, min(c0 + COL_TILE, d_ff))
            g_scr[:, cols] = (_silu(ug_ref[:, cols].astype(F32)) * uv_ref[:, cols].astype(F32)).astype(BF16)

    o_ref[...] = r_ref[...] + jnp.dot(g_scr[...], w_ref[...], preferred_element_type=F32)


def _ffn_down_call(u, w_down, res):
    s_len = u.shape[0]
    d_ff, n = w_down.shape
    tm, tn = min(ROW_TILE // 2, s_len), COL_TILE
    block_bytes = 4 * tm * d_ff * 2 + tm * d_ff * 2 + 2 * d_ff * tn * 2 + 4 * tm * tn * 4
    return pl.pallas_call(
        _ffn_down_kernel,
        name="ffn_down_residual",
        out_shape=jax.ShapeDtypeStruct((s_len, n), F32),
        grid=(s_len // tm, n // tn),
        in_specs=[
            pl.BlockSpec((tm, d_ff), lambda i, j: (i, 0)),
            pl.BlockSpec((tm, d_ff), lambda i, j: (i, 1)),
            pl.BlockSpec((d_ff, tn), lambda i, j: (0, j)),
            pl.BlockSpec((tm, tn), lambda i, j: (i, j)),
        ],
        out_specs=pl.BlockSpec((tm, tn), lambda i, j: (i, j)),
        scratch_shapes=[pltpu.VMEM((tm, d_ff), BF16)],
        compiler_params=_params(("arbitrary", "arbitrary"), block_bytes),
    )(u, u, w_down, res)


def _attn_kernel(slopes_ref, q_ref, k_ref, v_ref, lq1_ref, lk1_ref, lq2_ref, lk2_ref, subln_ref, o_ref,
                 m_scr, l_scr, acc_scr, *, tile, head_dim, lambda_init):
    h, qi = pl.program_id(0), pl.program_id(1)
    slope = slopes_ref[h]
    q0 = pl.multiple_of(qi * tile, tile)

    row = lax.broadcasted_iota(jnp.int32, (tile, tile), 0)
    col = lax.broadcasted_iota(jnp.int32, (tile, tile), 1)
    dist = jnp.abs(row - col).astype(F32)
    allowed = (col // CHUNK) <= (row // CHUNK)
    k_diag = k_ref[pl.ds(q0, tile), :]
    v_diag = v_ref[pl.ds(q0, tile), :]
    for m in range(2):
        cols = slice(m * head_dim, (m + 1) * head_dim)
        s = lax.dot_general(q_ref[:, cols], k_diag[:, cols], _NT, preferred_element_type=F32)
        z = jnp.where(allowed, s - slope * dist, -jnp.inf)
        mx = jnp.max(z, axis=-1, keepdims=True)
        p = jnp.exp(z - mx)
        m_scr[m] = mx
        l_scr[m] = jnp.sum(p, axis=-1, keepdims=True)
        acc_scr[m] = jnp.dot(p.astype(BF16), v_diag, preferred_element_type=F32)

    col_bias = slope * lax.broadcasted_iota(jnp.int32, (1, tile), 1).astype(F32)
    q_pos = (lax.broadcasted_iota(jnp.int32, (tile, 1), 0) + q0).astype(F32)

    def kv_step(kj, carry):
        k0 = pl.multiple_of(kj * tile, tile)
        k_blk = k_ref[pl.ds(k0, tile), :]
        v_blk = v_ref[pl.ds(k0, tile), :]
        c = -slope * (q_pos - k0.astype(F32))
        for m in range(2):
            cols = slice(m * head_dim, (m + 1) * head_dim)
            s = lax.dot_general(q_ref[:, cols], k_blk[:, cols], _NT, preferred_element_type=F32)
            z = s + col_bias
            m_old = m_scr[m]
            m_new = jnp.maximum(m_old, jnp.max(z, axis=-1, keepdims=True) + c)
            p = jnp.exp(z - (m_new - c))
            alpha = jnp.exp(m_old - m_new)
            l_scr[m] = alpha * l_scr[m] + jnp.sum(p, axis=-1, keepdims=True)
            acc_scr[m] = alpha * acc_scr[m] + jnp.dot(p.astype(BF16), v_blk, preferred_element_type=F32)
            m_scr[m] = m_new
        return carry

    lax.fori_loop(0, qi, kv_step, 0)

    lam = (jnp.exp(jnp.sum(lq1_ref[...] * lk1_ref[...], axis=-1, keepdims=True))
           - jnp.exp(jnp.sum(lq2_ref[...] * lk2_ref[...], axis=-1, keepdims=True)) + lambda_init)
    o = acc_scr[0] / l_scr[0] - lam * (acc_scr[1] / l_scr[1])
    o_ref[...] = (_rms_rows(o, subln_ref[...]) * (1.0 - lambda_init)).astype(o_ref.dtype)


def _attn_call(qkv, slopes, lq1, lk1, lq2, lk2, subln, *, n_heads, head_dim, lambda_init):
    s_len = qkv.shape[0]
    tile = min(ATTN_TILE, s_len)
    hw = 2 * head_dim
    vec = lambda a: a.reshape(1, -1)
    vec_spec = lambda n: pl.BlockSpec((1, n), lambda h, qi: (0, 0))
    block_bytes = 2 * (2 * s_len * hw * 2) + 4 * tile * hw * 2 + 2 * tile * hw * 4 + 4 * tile * V7X_LANES * 4 \
        + 6 * tile * tile * 4
    return pl.pallas_call(
        functools.partial(_attn_kernel, tile=tile, head_dim=head_dim, lambda_init=lambda_init),
        name="diff_attention",
        out_shape=jax.ShapeDtypeStruct((s_len, n_heads * hw), BF16),
        grid=(n_heads, s_len // tile),
        in_specs=[
            pl.BlockSpec(memory_space=pltpu.SMEM),
            pl.BlockSpec((tile, hw), lambda h, qi: (qi, h)),
            pl.BlockSpec((s_len, hw), lambda h, qi: (0, n_heads + h)),
            pl.BlockSpec((s_len, hw), lambda h, qi: (0, 2 * n_heads + h)),
            vec_spec(head_dim), vec_spec(head_dim), vec_spec(head_dim), vec_spec(head_dim), vec_spec(hw),
        ],
        out_specs=pl.BlockSpec((tile, hw), lambda h, qi: (qi, h)),
        scratch_shapes=[
            pltpu.VMEM((2, tile, 1), F32),
            pltpu.VMEM((2, tile, 1), F32),
            pltpu.VMEM((2, tile, hw), F32),
        ],
        compiler_params=_params(("arbitrary", "arbitrary"), block_bytes),
    )(slopes, qkv, qkv, qkv, vec(lq1), vec(lk1), vec(lq2), vec(lk2), vec(subln))


def _gates_kernel(ba_ref, alog_ref, dtb_ref, beta_ref, gc_ref):
    rows = ba_ref.shape[0]
    beta_ref[...] = jax.nn.sigmoid(ba_ref[:, :V7X_LANES])
    g = -jnp.exp(alog_ref[...]) * jax.nn.softplus(ba_ref[:, V7X_LANES:] + dtb_ref[...])
    r = lax.broadcasted_iota(jnp.int32, (rows, rows), 0)
    c = lax.broadcasted_iota(jnp.int32, (rows, rows), 1)
    tri = jnp.where(r // CHUNK == c // CHUNK, jnp.where(c <= r, 1.0, 0.0), 0.0).astype(F32)
    gc_ref[...] = jnp.dot(tri, g, preferred_element_type=F32, precision=lax.Precision.HIGHEST)


def _gates_call(ba, a_log_row, dt_bias_row):
    s_len = ba.shape[0]
    tm = min(GATE_ROWS, s_len)
    out = jax.ShapeDtypeStruct((s_len, V7X_LANES), F32)
    return pl.pallas_call(
        _gates_kernel,
        name="gdn_gates",
        out_shape=(out, out),
        grid=(s_len // tm,),
        in_specs=[
            pl.BlockSpec((tm, 2 * V7X_LANES), lambda i: (i, 0)),
            pl.BlockSpec((1, V7X_LANES), lambda i: (0, 0)),
            pl.BlockSpec((1, V7X_LANES), lambda i: (0, 0)),
        ],
        out_specs=(pl.BlockSpec((tm, V7X_LANES), lambda i: (i, 0)), pl.BlockSpec((tm, V7X_LANES), lambda i: (i, 0))),
        compiler_params=_params(("arbitrary",), 8 * tm * 2 * V7X_LANES * 4 + 4 * tm * tm * 4),
    )(ba, a_log_row, dt_bias_row)


def _split_bf16(x):
    hi = x.astype(BF16)
    return hi, (x - hi.astype(F32)).astype(BF16)


def _delta_kernel(q_ref, k_ref, v_ref, z_ref, beta_ref, gc_ref, gct_ref, ng_ref, o_ref, state_scr, *,
                  n_pairs, n_chunks, head_dim):
    hd = head_dim
    pairs = range(n_pairs)

    @pl.when(pl.program_id(1) == 0)
    def _():
        state_scr[...] = jnp.zeros(state_scr.shape, F32)

    ri = lax.broadcasted_iota(jnp.int32, (CHUNK, 2 * CHUNK), 0)
    lane = lax.broadcasted_iota(jnp.int32, (CHUNK, 2 * CHUNK), 1)
    left = lane < CHUNK
    ci = jnp.where(left, lane, lane - CHUNK)
    tril = ci <= ri
    strict = ci < ri
    eye = jnp.where(ci == ri, 1.0, 0.0).astype(F32)
    state_left = lax.broadcasted_iota(jnp.int32, (hd, 2 * hd), 1) < hd

    def block_diag(y, keep_left):
        zero = jnp.zeros_like(y)
        return jnp.concatenate([jnp.where(keep_left, y, zero), jnp.where(keep_left, zero, y)], axis=0)

    def pair_matmul(xh, xl, yh, yl):
        bh, bl = block_diag(yh, left), block_diag(yl, left)
        rhs = jnp.concatenate([jnp.concatenate([bh, bl], axis=1),
                               jnp.concatenate([bh, jnp.zeros_like(bl)], axis=1)], axis=0)
        out = jnp.dot(jnp.concatenate([xh, xl], axis=1), rhs, preferred_element_type=F32)
        return out[:, :2 * CHUNK] + out[:, 2 * CHUNK:]

    def chunk_step(c, carry):
        rows = pl.ds(pl.multiple_of(c * CHUNK, CHUNK), CHUNK)
        beta = beta_ref[rows, :]
        g_col = gc_ref[rows, :]
        g_rows = gct_ref[0, c]
        b_rows = gct_ref[1, c]
        lanes = lambda a, h: jnp.broadcast_to(a[:, h:h + 1], (CHUNK, hd))

        k = [k_ref[rows, p * hd:(p + 1) * hd] for p in pairs]
        q = [q_ref[rows, p * hd:(p + 1) * hd] for p in pairs]
        kq = [lax.dot_general(jnp.concatenate([k[p], q[p]], axis=0), jnp.concatenate([k[p], k[p]], axis=0), _NT,
                              preferred_element_type=F32) for p in pairs]
        g_row = [g_rows[p:p + 1, :] for p in pairs]
        lmat, qk_decay, q_dec, k_dec = [], [], [], []
        for p in pairs:
            g_a, g_b = lanes(g_col, 2 * p), lanes(g_col, 2 * p + 1)
            b_pk = jnp.where(left, lanes(beta, 2 * p), lanes(beta, 2 * p + 1))
            decay = jnp.exp(jnp.where(tril, jnp.where(left, g_a, g_b) - g_row[p], -jnp.inf))
            lmat.append(jnp.where(strict, kq[p][:CHUNK] * b_pk * decay, 0.0))
            qk_decay.append((kq[p][CHUNK:] * decay).astype(BF16))
            qf, kf = q[p].astype(F32), k[p].astype(F32)
            q_dec.append(jnp.concatenate([qf * jnp.exp(g_a), qf * jnp.exp(g_b)], axis=1).astype(BF16))
            ge_a, ge_b = g_row[p][:, CHUNK - 1:CHUNK], g_row[p][:, 2 * CHUNK - 1:2 * CHUNK]
            k_dec.append(((kf * jnp.exp(ge_a - g_a)).astype(BF16), (kf * jnp.exp(ge_b - g_b)).astype(BF16)))
        lsplit = [_split_bf16(lmat[p]) for p in pairs]
        x = [pair_matmul(*lsplit[p], *lsplit[p]) for p in pairs]
        acc = [eye - lmat[p] for p in pairs]
        power = 2
        while power * 2 < CHUNK:
            for p in pairs:
                sh, sl = _split_bf16(jnp.concatenate([acc[p], x[p]], axis=0))
                both = pair_matmul(sh, sl, sh[CHUNK:], sl[CHUNK:])
                acc[p] = acc[p] + both[:CHUNK]
                x[p] = both[CHUNK:]
            power *= 2
        tmat = [acc[p] + pair_matmul(*_split_bf16(acc[p]), *_split_bf16(x[p])) for p in pairs]
        zb = jnp.zeros((CHUNK, hd), BF16)
        uw = []
        for p in pairs:
            b_row = b_rows[p:p + 1, :]
            t_scaled = jnp.concatenate([tmat[p] * b_row, tmat[p] * (b_row * jnp.exp(g_row[p]))], axis=1)
            v_pk = v_ref[rows, 2 * p * hd:2 * (p + 1) * hd]
            rhs = jnp.concatenate([jnp.concatenate([v_pk[:, :hd], zb, zb, zb], axis=1),
                                   jnp.concatenate([zb, v_pk[:, hd:], zb, zb], axis=1),
                                   jnp.concatenate([zb, zb, k[p], zb], axis=1),
                                   jnp.concatenate([zb, zb, zb, k[p]], axis=1)], axis=0)
            uw.append(jnp.dot(t_scaled.astype(BF16), rhs, preferred_element_type=F32))
        s_old = [state_scr[p] for p in pairs]
        wq = []
        for p in pairs:
            wq.append(jnp.dot(jnp.concatenate([uw[p][:, 2 * hd:].astype(BF16), q_dec[p]], axis=0),
                              block_diag(s_old[p].astype(BF16), state_left), preferred_element_type=F32))
        o_pk = []
        for p in pairs:
            vn = (uw[p][:, :2 * hd] - wq[p][:CHUNK]).astype(BF16)
            vn_diag = jnp.concatenate([jnp.concatenate([vn[:, :hd], zb], axis=1),
                                       jnp.concatenate([zb, vn[:, hd:]], axis=1)], axis=0)
            o_pk.append(wq[p][CHUNK:] + jnp.dot(qk_decay[p], vn_diag, preferred_element_type=F32))
            ge_a, ge_b = g_row[p][:, CHUNK - 1:CHUNK], g_row[p][:, 2 * CHUNK - 1:2 * CHUNK]
            ds_a = lax.dot_general(k_dec[p][0], vn[:, :hd], _TN, preferred_element_type=F32)
            ds_b = lax.dot_general(k_dec[p][1], vn[:, hd:], _TN, preferred_element_type=F32)
            state_scr[p] = jnp.concatenate([s_old[p][:, :hd] * jnp.exp(ge_a) + ds_a,
                                            s_old[p][:, hd:] * jnp.exp(ge_b) + ds_b], axis=1)
        for p in pairs:
            wide_cols = slice(2 * p * hd, 2 * (p + 1) * hd)
            z_pk = z_ref[rows, wide_cols].astype(F32)
            gated = [_rms_rows(o_pk[p][:, h * hd:(h + 1) * hd], ng_ref[...]) * _silu(z_pk[:, h * hd:(h + 1) * hd])
                     for h in range(2)]
            o_ref[rows, wide_cols] = jnp.concatenate(gated, axis=1).astype(o_ref.dtype)
        return carry

    lax.fori_loop(0, n_chunks, chunk_step, 0)


def _delta_call(qkvz, beta_g, gc_g, gct_g, norm_g, *, n_qk_heads, head_dim):
    s_len = qkvz.shape[0]
    n_groups = n_qk_heads // DELTA_PAIRS
    heads = 2 * DELTA_PAIRS
    rows = DELTA_CHUNKS * CHUNK
    qk_w, v_w = DELTA_PAIRS * head_dim, heads * head_dim
    key_dim, value_dim = n_qk_heads * head_dim, 2 * n_qk_heads * head_dim
    k_off, v_off, z_off = key_dim // qk_w, 2 * key_dim // v_w, (2 * key_dim + value_dim) // v_w
    block_bytes = 2 * rows * (2 * qk_w + 3 * v_w) * 2 + 6 * rows * V7X_LANES * 4 + heads * head_dim * head_dim * 4
    return pl.pallas_call(
        functools.partial(_delta_kernel, n_pairs=DELTA_PAIRS, n_chunks=DELTA_CHUNKS, head_dim=head_dim),
        name="gdn_delta_rule",
        out_shape=jax.ShapeDtypeStruct((s_len, value_dim), BF16),
        grid=(n_groups, s_len // rows),
        in_specs=[
            pl.BlockSpec((rows, qk_w), lambda g, b: (b, g)),
            pl.BlockSpec((rows, qk_w), lambda g, b: (b, k_off + g)),
            pl.BlockSpec((rows, v_w), lambda g, b: (b, v_off + g)),
            pl.BlockSpec((rows, v_w), lambda g, b: (b, z_off + g)),
            pl.BlockSpec((None, rows, heads), lambda g, b: (g, b, 0)),
            pl.BlockSpec((None, rows, heads), lambda g, b: (g, b, 0)),
            pl.BlockSpec((None, 2, DELTA_CHUNKS, DELTA_PAIRS, 2 * CHUNK), lambda g, b: (g, 0, b, 0, 0)),
            pl.BlockSpec((1, head_dim), lambda g, b: (0, 0)),
        ],
        out_specs=pl.BlockSpec((rows, v_w), lambda g, b: (b, g)),
        scratch_shapes=[pltpu.VMEM((DELTA_PAIRS, head_dim, 2 * head_dim), F32)],
        compiler_params=_params(("arbitrary", "arbitrary"), block_bytes),
    )(qkvz, qkvz, qkvz, qkvz, beta_g, gc_g, gct_g, norm_g.reshape(1, head_dim))


def _diff_attention_layer(x, gain, w_qkv, layer, q_norm, k_norm, lq1, lk1, lq2, lk2, subln, w_o, lambda_init):
    d = x.shape[1]
    head_dim = q_norm.shape[0]
    n_heads = w_o.shape[1] // subln.shape[0]
    qk_dim = 2 * n_heads * head_dim
    n_cols = w_qkv.shape[2]
    vec_spec = pl.BlockSpec((1, head_dim), lambda i, j: (0, 0))
    qkv = _proj_call(
        "attn_qkv_proj", functools.partial(_qkv_kernel, n_q_tiles=qk_dim // COL_TILE, n_k_tiles=qk_dim // COL_TILE,
                          head_dim=head_dim, q_scale=head_dim ** -0.5),
        x, gain, w_qkv, layer, n_cols, (q_norm.reshape(1, -1), k_norm.reshape(1, -1)), (vec_spec, vec_spec), BF16, ())
    slopes = jnp.exp2(-8.0 * jnp.arange(1, n_heads + 1, dtype=F32) / n_heads)
    o = _attn_call(qkv, slopes, lq1, lk1, lq2, lk2, subln, n_heads=n_heads, head_dim=head_dim,
                   lambda_init=lambda_init)
    return _mm_res_call(o, w_o, layer, x)


def _gated_deltanet_layer(x, gain, w_in, layer, conv_w, a_log, dt_bias, norm_g, w_out):
    s_len, d = x.shape
    head_dim = norm_g.shape[0]
    n_v_heads = a_log.shape[0]
    n_qk_heads = n_v_heads // 2
    key_dim, value_dim = n_qk_heads * head_dim, n_v_heads * head_dim
    conv_dim = 2 * key_dim + value_dim
    main_cols = conv_dim + value_dim
    n_conv_tiles = conv_dim // COL_TILE
    cw_spec = pl.BlockSpec((None, conv_w.shape[1], COL_TILE), lambda i, j: (layer, 0, jnp.minimum(j, n_conv_tiles - 1)))
    qkvz = _proj_call(
        "gdn_in_proj", functools.partial(_gdn_in_kernel, n_q_tiles=key_dim // COL_TILE, n_qk_tiles=2 * key_dim // COL_TILE,
                          n_conv_tiles=n_conv_tiles, head_dim=head_dim, q_scale=head_dim ** -0.5),
        x, gain, w_in, layer, main_cols, (conv_w,), (cw_spec,), BF16,
        (pltpu.VMEM((n_conv_tiles, V7X_SUBLANES, COL_TILE), F32),))

    pad = V7X_LANES - n_v_heads
    w_b = jnp.pad(w_in[layer][:, main_cols:main_cols + n_v_heads], ((0, 0), (0, pad)))
    w_a = jnp.pad(w_in[layer][:, main_cols + n_v_heads:], ((0, 0), (0, pad)))
    w_ba = jnp.concatenate([w_b, w_a], axis=1)[None]
    ba = _proj_call("gdn_gate_proj", _plain_proj_kernel, x, gain, w_ba, 0, 2 * V7X_LANES, (), (), F32, (), col_tile=2 * V7X_LANES)
    beta, gc = _gates_call(ba, jnp.pad(a_log, (0, pad)).reshape(1, -1), jnp.pad(dt_bias, (0, pad)).reshape(1, -1))

    heads = 2 * DELTA_PAIRS
    n_groups = n_v_heads // heads
    beta_g = beta[:, :n_v_heads].reshape(s_len, n_groups, heads).transpose(1, 0, 2)
    gc_g = gc[:, :n_v_heads].reshape(s_len, n_groups, heads).transpose(1, 0, 2)
    rows_of = lambda a: a[:, :n_v_heads].reshape(s_len // CHUNK, CHUNK, n_groups, DELTA_PAIRS, 2).transpose(
        2, 0, 3, 4, 1).reshape(n_groups, s_len // CHUNK, DELTA_PAIRS, 2 * CHUNK)
    gct_g = jnp.stack([rows_of(gc), rows_of(beta)], axis=1)
    o = _delta_call(qkvz, beta_g, gc_g, gct_g, norm_g, n_qk_heads=n_qk_heads, head_dim=head_dim)
    return _mm_res_call(o, w_out, layer, x)


def _conv_ffn_layer(x, gain, w_up, conv_w, conv_b, w_down, layer):
    n_cols = w_up.shape[2]
    n_tiles = pl.cdiv(n_cols, COL_TILE)
    cw_spec = pl.BlockSpec((None, conv_w.shape[1], COL_TILE), lambda i, j: (layer, 0, j))
    cb_spec = pl.BlockSpec((None, 1, COL_TILE), lambda i, j: (layer, 0, j))
    u = _proj_call("ffn_up_proj", _ffn_up_kernel, x, gain, w_up, layer, n_cols, (conv_w, conv_b[:, None, :]), (cw_spec, cb_spec),
                   BF16, (pltpu.VMEM((n_tiles, V7X_SUBLANES, COL_TILE), F32),))
    return _ffn_down_call(u, w_down[layer].astype(BF16), x)


def kernel(x, mixer_norm, ffn_norm, diff_w_qkv, diff_q_norm, diff_k_norm, diff_lambda_q1, diff_lambda_k1,
           diff_lambda_q2, diff_lambda_k2, diff_subln, diff_w_o, gdn_w_in, gdn_conv_w, gdn_A_log, gdn_dt_bias,
           gdn_norm, gdn_w_out, ffn_w_up, ffn_conv_w, ffn_conv_b, ffn_w_down):
    b_sz, s_len, d = x.shape
    depth = mixer_norm.shape[0]
    outs = []
    for b in range(b_sz):
        xb = x.reshape(s_len, d) if b_sz == 1 else x[b]
        for i in range(depth):
            j = i // 2
            if i % 2 == 0:
                lambda_init = 0.8 - 0.6 * math.exp(-0.3 * i)
                xb = _diff_attention_layer(xb, mixer_norm[i], diff_w_qkv, j, diff_q_norm[j], diff_k_norm[j],
                                           diff_lambda_q1[j], diff_lambda_k1[j], diff_lambda_q2[j],
                                           diff_lambda_k2[j], diff_subln[j], diff_w_o, lambda_init)
            else:
                xb = _gated_deltanet_layer(xb, mixer_norm[i], gdn_w_in, j, gdn_conv_w, gdn_A_log[j], gdn_dt_bias[j],
                                           gdn_norm[j], gdn_w_out)
            xb = _conv_ffn_layer(xb, ffn_norm[i], ffn_w_up, ffn_conv_w, ffn_conv_b, ffn_w_down, i)
        outs.append(xb)
    return outs[0].reshape(x.shape) if b_sz == 1 else jnp.stack(outs, axis=0)
```

```python
import functools
import math

import jax
import jax.numpy as jnp
from jax import lax
from jax.experimental import pallas as pl
from jax.experimental.pallas import tpu as pltpu

F32 = jnp.float32
BF16 = jnp.bfloat16

RMS_EPS = 1e-6
CHUNK = 64
V7X_VMEM_BYTES = 64 * 1024 * 1024
V7X_LANES = 128
V7X_SUBLANES = 8
COMPILER_TEMP_BYTES = 12 * 1024 * 1024

ROW_TILE = 1024
COL_TILE = 512
ATTN_TILE = 512
DELTA_PAIRS = 16
DELTA_CHUNKS = 4
GATE_ROWS = 256

_NT = (((1,), (1,)), ((), ()))
_TN = (((0,), (0,)), ((), ()))


def _params(semantics, block_bytes):
    limit = min(int(block_bytes) + COMPILER_TEMP_BYTES, V7X_VMEM_BYTES - (4 << 20))
    return pltpu.CompilerParams(dimension_semantics=semantics, vmem_limit_bytes=limit)


def _rms_rows(x, gain):
    return x * lax.rsqrt(jnp.mean(x * x, axis=-1, keepdims=True) + RMS_EPS) * gain


def _silu(x):
    return x * jax.nn.sigmoid(x)


def _causal_conv(u, carry, w_ref):
    kw = w_ref.shape[0]
    out = u * w_ref[kw - 1:kw, :]
    top16 = jnp.concatenate([carry, u[0:V7X_SUBLANES]], axis=0)
    for s in range(1, kw):
        rolled = pltpu.roll(u, s, axis=0)
        top = pltpu.roll(top16, s, axis=0)[V7X_SUBLANES:]
        shifted = jnp.concatenate([top, rolled[V7X_SUBLANES:]], axis=0)
        out = out + shifted * w_ref[kw - 1 - s:kw - s, :]
    return out


def _normalize_once(x_ref, g_ref, h_scr):
    @pl.when(pl.program_id(1) == 0)
    def _():
        h_scr[...] = _rms_rows(x_ref[...], g_ref[...]).astype(BF16)


def _head_proj_kernel(x_ref, g_ref, w_ref, gain_ref, o_ref, h_scr, *, head_dim, scale, normalize, transpose, tile):
    _normalize_once(x_ref, g_ref, h_scr)
    acc = jnp.dot(h_scr[...], w_ref[...].astype(BF16), preferred_element_type=F32)
    if normalize:
        acc = jnp.concatenate(
            [_rms_rows(acc[:, c:c + head_dim], gain_ref[...]) * scale for c in range(0, acc.shape[1], head_dim)], axis=1)
    if transpose:
        acc_t = acc.T
        for t in range(acc.shape[0] // tile):
            o_ref[t] = acc_t[:, t * tile:(t + 1) * tile].astype(o_ref.dtype)
    else:
        o_ref[...] = acc.astype(o_ref.dtype)


def _plain_proj_kernel(x_ref, g_ref, w_ref, o_ref, h_scr):
    _normalize_once(x_ref, g_ref, h_scr)
    o_ref[...] = jnp.dot(h_scr[...], w_ref[...].astype(BF16), preferred_element_type=F32).astype(o_ref.dtype)


def _ffn_up_kernel(x_ref, g_ref, w_ref, cw_ref, cb_ref, o_ref, h_scr, carry_scr):
    i, j = pl.program_id(0), pl.program_id(1)
    _normalize_once(x_ref, g_ref, h_scr)

    @pl.when(i == 0)
    def _():
        carry_scr[j] = jnp.zeros(carry_scr.shape[1:], F32)

    u = jnp.dot(h_scr[...], w_ref[...].astype(BF16), preferred_element_type=F32)
    y = _causal_conv(u, carry_scr[j], cw_ref) + cb_ref[...]
    carry_scr[j] = u[u.shape[0] - V7X_SUBLANES:]
    o_ref[...] = y.astype(o_ref.dtype)


def _gdn_in_kernel(x_ref, g_ref, w_ref, cw_ref, o_ref, h_scr, carry_scr, *, n_q_tiles, n_qk_tiles, n_conv_tiles,
                   head_dim, q_scale):
    i, j = pl.program_id(0), pl.program_id(1)
    _normalize_once(x_ref, g_ref, h_scr)
    u = jnp.dot(h_scr[...], w_ref[...].astype(BF16), preferred_element_type=F32)

    def conv_silu():
        @pl.when(i == 0)
        def _():
            carry_scr[j] = jnp.zeros(carry_scr.shape[1:], F32)
        y = _causal_conv(u, carry_scr[j], cw_ref)
        carry_scr[j] = u[u.shape[0] - V7X_SUBLANES:]
        return _silu(y)

    def store_l2(y, scale):
        for c in range(y.shape[1] // head_dim):
            cols = slice(c * head_dim, (c + 1) * head_dim)
            blk = y[:, cols]
            inv = lax.rsqrt(jnp.sum(blk * blk, axis=-1, keepdims=True) + RMS_EPS)
            o_ref[:, cols] = (blk * inv * scale).astype(o_ref.dtype)

    @pl.when(j < n_q_tiles)
    def _():
        store_l2(conv_silu(), q_scale)

    @pl.when(jnp.logical_and(j >= n_q_tiles, j < n_qk_tiles))
    def _():
        store_l2(conv_silu(), 1.0)

    @pl.when(jnp.logical_and(j >= n_qk_tiles, j < n_conv_tiles))
    def _():
        o_ref[...] = conv_silu().astype(o_ref.dtype)

    @pl.when(j >= n_conv_tiles)
    def _():
        o_ref[...] = u.astype(o_ref.dtype)


def _proj_call(name, kernel, x, gain, w3, layer, n_cols, extra_inputs, extra_specs, out_dtype, scratch,
               col_tile=COL_TILE, first_col_tile=0, transposed_tile=None):
    s_len, d = x.shape
    tm, tn = min(ROW_TILE, s_len), col_tile
    grid = (s_len // tm, pl.cdiv(n_cols, tn))
    in_specs = [
        pl.BlockSpec((tm, d), lambda i, j: (i, 0)),
        pl.BlockSpec((1, d), lambda i, j: (0, 0)),
        pl.BlockSpec((None, d, tn), lambda i, j: (layer, 0, first_col_tile + j)),
    ] + list(extra_specs)
    if transposed_tile is None:
        out_shape = jax.ShapeDtypeStruct((s_len, n_cols), out_dtype)
        out_spec = pl.BlockSpec((tm, tn), lambda i, j: (i, j))
    else:
        out_shape = jax.ShapeDtypeStruct((s_len // transposed_tile, n_cols, transposed_tile), out_dtype)
        out_spec = pl.BlockSpec((tm // transposed_tile, tn, transposed_tile), lambda i, j: (i, j, 0))
    out_bytes = jnp.dtype(out_dtype).itemsize
    block_bytes = 2 * tm * d * 4 + tm * d * 2 + 2 * d * tn * 4 + d * tn * 2 + 2 * tm * tn * out_bytes + 3 * tm * tn * 4
    return pl.pallas_call(
        kernel,
        name=name,
        out_shape=out_shape,
        grid=grid,
        in_specs=in_specs,
        out_specs=out_spec,
        scratch_shapes=[pltpu.VMEM((tm, d), BF16)] + list(scratch),
        compiler_params=_params(("arbitrary", "arbitrary"), block_bytes),
    )(x, gain.reshape(1, d), w3, *extra_inputs)


def _mm_res_kernel(a_ref, w_ref, r_ref, o_ref):
    o_ref[...] = r_ref[...] + jnp.dot(a_ref[...], w_ref[...].astype(BF16), preferred_element_type=F32)


def _mm_res_call(a, w3, layer, res):
    s_len, k = a.shape
    n = w3.shape[2]
    tm, tn = min(ROW_TILE, s_len), COL_TILE
    block_bytes = 2 * tm * k * 2 + 2 * k * tn * 4 + k * tn * 2 + 4 * tm * tn * 4 + tm * tn * 4
    return pl.pallas_call(
        _mm_res_kernel,
        name="out_proj_residual",
        out_shape=jax.ShapeDtypeStruct((s_len, n), F32),
        grid=(s_len // tm, n // tn),
        in_specs=[
            pl.BlockSpec((tm, k), lambda i, j: (i, 0)),
            pl.BlockSpec((None, k, tn), lambda i, j: (layer, 0, j)),
            pl.BlockSpec((tm, tn), lambda i, j: (i, j)),
        ],
        out_specs=pl.BlockSpec((tm, tn), lambda i, j: (i, j)),
        compiler_params=_params(("arbitrary", "arbitrary"), block_bytes),
    )(a, w3, res)


def _ffn_down_kernel(uv_ref, ug_ref, w_ref, r_ref, o_ref, g_scr):
    @pl.when(pl.program_id(1) == 0)
    def _():
        d_ff = g_scr.shape[1]
        for c0 in range(0, d_ff, COL_TILE):
            cols = slice(c0, min(c0 + COL_TILE, d_ff))
            g_scr[:, cols] = (_silu(ug_ref[:, cols].astype(F32)) * uv_ref[:, cols].astype(F32)).astype(BF16)

    o_ref[...] = r_ref[...] + jnp.dot(g_scr[...], w_ref[...], preferred_element_type=F32)


def _ffn_down_call(u, w_down, res):
    s_len = u.shape[0]
    d_ff, n = w_down.shape
    tm, tn = min(ROW_TILE // 2, s_len), COL_TILE
    block_bytes = 4 * tm * d_ff * 2 + tm * d_ff * 2 + 2 * d_ff * tn * 2 + 4 * tm * tn * 4
    return pl.pallas_call(
        _ffn_down_kernel,
        name="ffn_down_residual",
        out_shape=jax.ShapeDtypeStruct((s_len, n), F32),
        grid=(s_len // tm, n // tn),
        in_specs=[
            pl.BlockSpec((tm, d_ff), lambda i, j: (i, 0)),
            pl.BlockSpec((tm, d_ff), lambda i, j: (i, 1)),
            pl.BlockSpec((d_ff, tn), lambda i, j: (0, j)),
            pl.BlockSpec((tm, tn), lambda i, j: (i, j)),
        ],
        out_specs=pl.BlockSpec((tm, tn), lambda i, j: (i, j)),
        scratch_shapes=[pltpu.VMEM((tm, d_ff), BF16)],
        compiler_params=_params(("arbitrary", "arbitrary"), block_bytes),
    )(u, u, w_down, res)


def _attn_kernel(slopes_ref, qt_ref, k_ref, vt_ref, lq1_ref, lk1_ref, lq2_ref, lk2_ref, subln_ref, o_ref,
                 s_scr, kb_scr, m_scr, l_scr, acc_scr, *, tile, head_dim, lambda_init):
    h, qi = pl.program_id(0), pl.program_id(1)
    slope = slopes_ref[h]
    hd = head_dim

    def issue_scores(kj, slot):
        k_blk = k_ref[pl.ds(pl.multiple_of(kj * tile, tile), tile), :]
        for m in range(2):
            rows = slice(m * hd, (m + 1) * hd)
            s_scr[slot, m] = jnp.dot(k_blk[:, rows], qt_ref[rows, :], preferred_element_type=F32)

    issue_scores(qi, 1)
    issue_scores(0, 0)

    key = lax.broadcasted_iota(jnp.int32, (tile, tile), 0)
    qry = lax.broadcasted_iota(jnp.int32, (tile, tile), 1)
    kb_scr[...] = slope * key.astype(F32)
    bias = jnp.where((key // CHUNK) <= (qry // CHUNK), -slope * jnp.abs(key - qry).astype(F32), -jnp.inf)
    vt_diag = vt_ref[qi]
    for m in range(2):
        z = s_scr[1, m] + bias
        mx = jnp.max(z, axis=0, keepdims=True)
        p = jnp.exp(z - mx)
        m_scr[m] = mx
        l_scr[m] = jnp.sum(p, axis=0, keepdims=True)
        acc_scr[m] = jnp.dot(vt_diag, p.astype(BF16), preferred_element_type=F32)

    q_pos = (lax.broadcasted_iota(jnp.int32, (1, tile), 1) + qi * tile).astype(F32)

    def kv_step(kj, slot, issue_next):
        if issue_next:
            issue_scores(jnp.minimum(kj + 1, qi - 1), 1 - slot)
        c = -slope * (q_pos - (kj * tile).astype(F32))
        vt_blk = vt_ref[kj]
        for m in range(2):
            z = s_scr[slot, m] + kb_scr[...]
            m_old = m_scr[m]
            m_new = jnp.maximum(m_old, jnp.max(z, axis=0, keepdims=True) + c)
            p = jnp.exp(z - (m_new - c))
            alpha = jnp.exp(m_old - m_new)
            l_scr[m] = alpha * l_scr[m] + jnp.sum(p, axis=0, keepdims=True)
            acc_scr[m] = alpha * acc_scr[m] + jnp.dot(vt_blk, p.astype(BF16), preferred_element_type=F32)
            m_scr[m] = m_new

    def two_steps(t, carry):
        kv_step(2 * t, 0, True)
        kv_step(2 * t + 1, 1, True)
        return carry

    lax.fori_loop(0, qi // 2, two_steps, 0)

    @pl.when(lax.rem(qi, 2) == 1)
    def _():
        kv_step(qi - 1, 0, False)

    lam = (jnp.exp(jnp.sum(lq1_ref[...] * lk1_ref[...], axis=-1, keepdims=True))
           - jnp.exp(jnp.sum(lq2_ref[...] * lk2_ref[...], axis=-1, keepdims=True)) + lambda_init)
    o_t = acc_scr[0] / l_scr[0] - lam * (acc_scr[1] / l_scr[1])
    inv = lax.rsqrt(jnp.mean(o_t * o_t, axis=0, keepdims=True) + RMS_EPS)
    o_ref[...] = ((o_t * inv * subln_ref[...]) * (1.0 - lambda_init)).T.astype(o_ref.dtype)


def _attn_call(qt, k, vt, slopes, lq1, lk1, lq2, lk2, subln, *, n_heads, head_dim, lambda_init):
    n_tiles, _, tile = qt.shape
    s_len = k.shape[0]
    hw = 2 * head_dim
    vec = lambda a: a.reshape(1, -1)
    vec_spec = lambda n: pl.BlockSpec((1, n), lambda h, qi: (0, 0))
    block_bytes = 2 * (2 * s_len * hw * 2) + 4 * tile * hw * 2 + 2 * tile * hw * 4 + 5 * tile * tile * 4 \
        + 6 * tile * tile * 4
    return pl.pallas_call(
        functools.partial(_attn_kernel, tile=tile, head_dim=head_dim, lambda_init=lambda_init),
        name="diff_attention",
        out_shape=jax.ShapeDtypeStruct((s_len, n_heads * hw), BF16),
        grid=(n_heads, n_tiles),
        in_specs=[
            pl.BlockSpec(memory_space=pltpu.SMEM),
            pl.BlockSpec((None, hw, tile), lambda h, qi: (qi, h, 0)),
            pl.BlockSpec((s_len, hw), lambda h, qi: (0, h)),
            pl.BlockSpec((n_tiles, hw, tile), lambda h, qi: (0, h, 0)),
            vec_spec(head_dim), vec_spec(head_dim), vec_spec(head_dim), vec_spec(head_dim),
            pl.BlockSpec((hw, 1), lambda h, qi: (0, 0)),
        ],
        out_specs=pl.BlockSpec((tile, hw), lambda h, qi: (qi, h)),
        scratch_shapes=[
            pltpu.VMEM((2, 2, tile, tile), F32),
            pltpu.VMEM((tile, tile), F32),
            pltpu.VMEM((2, 1, tile), F32),
            pltpu.VMEM((2, 1, tile), F32),
            pltpu.VMEM((2, hw, tile), F32),
        ],
        compiler_params=_params(("arbitrary", "arbitrary"), block_bytes),
    )(slopes, qt, k, vt, vec(lq1), vec(lk1), vec(lq2), vec(lk2), subln.reshape(-1, 1))


def _gates_kernel(ba_ref, alog_ref, dtb_ref, beta_ref, gc_ref):
    rows = ba_ref.shape[0]
    beta_ref[...] = jax.nn.sigmoid(ba_ref[:, :V7X_LANES])
    g = -jnp.exp(alog_ref[...]) * jax.nn.softplus(ba_ref[:, V7X_LANES:] + dtb_ref[...])
    r = lax.broadcasted_iota(jnp.int32, (rows, rows), 0)
    c = lax.broadcasted_iota(jnp.int32, (rows, rows), 1)
    tri = jnp.where(r // CHUNK == c // CHUNK, jnp.where(c <= r, 1.0, 0.0), 0.0).astype(F32)
    gc_ref[...] = jnp.dot(tri, g, preferred_element_type=F32, precision=lax.Precision.HIGHEST)


def _gates_call(ba, a_log_row, dt_bias_row):
    s_len = ba.shape[0]
    tm = min(GATE_ROWS, s_len)
    out = jax.ShapeDtypeStruct((s_len, V7X_LANES), F32)
    return pl.pallas_call(
        _gates_kernel,
        name="gdn_gates",
        out_shape=(out, out),
        grid=(s_len // tm,),
        in_specs=[
            pl.BlockSpec((tm, 2 * V7X_LANES), lambda i: (i, 0)),
            pl.BlockSpec((1, V7X_LANES), lambda i: (0, 0)),
            pl.BlockSpec((1, V7X_LANES), lambda i: (0, 0)),
        ],
        out_specs=(pl.BlockSpec((tm, V7X_LANES), lambda i: (i, 0)), pl.BlockSpec((tm, V7X_LANES), lambda i: (i, 0))),
        compiler_params=_params(("arbitrary",), 8 * tm * 2 * V7X_LANES * 4 + 4 * tm * tm * 4),
    )(ba, a_log_row, dt_bias_row)


def _split_bf16(x):
    hi = x.astype(BF16)
    return hi, (x - hi.astype(F32)).astype(BF16)


def _delta_kernel(q_ref, k_ref, v_ref, z_ref, beta_ref, gc_ref, gct_ref, ng_ref, o_ref, state_scr, *,
                  n_pairs, n_chunks, head_dim):
    hd = head_dim
    pairs = range(n_pairs)

    @pl.when(pl.program_id(1) == 0)
    def _():
        state_scr[...] = jnp.zeros(state_scr.shape, F32)

    ri = lax.broadcasted_iota(jnp.int32, (CHUNK, 2 * CHUNK), 0)
    lane = lax.broadcasted_iota(jnp.int32, (CHUNK, 2 * CHUNK), 1)
    left = lane < CHUNK
    ci = jnp.where(left, lane, lane - CHUNK)
    tril = ci <= ri
    strict = ci < ri
    eye = jnp.where(ci == ri, 1.0, 0.0).astype(F32)
    state_left = lax.broadcasted_iota(jnp.int32, (hd, 2 * hd), 1) < hd

    def block_diag(y, keep_left):
        zero = jnp.zeros_like(y)
        return jnp.concatenate([jnp.where(keep_left, y, zero), jnp.where(keep_left, zero, y)], axis=0)

    def pair_matmul(xh, xl, yh, yl):
        bh, bl = block_diag(yh, left), block_diag(yl, left)
        rhs = jnp.concatenate([jnp.concatenate([bh, bl], axis=1),
                               jnp.concatenate([bh, jnp.zeros_like(bl)], axis=1)], axis=0)
        out = jnp.dot(jnp.concatenate([xh, xl], axis=1), rhs, preferred_element_type=F32)
        return out[:, :2 * CHUNK] + out[:, 2 * CHUNK:]

    def chunk_step(c, carry):
        rows = pl.ds(pl.multiple_of(c * CHUNK, CHUNK), CHUNK)
        beta = beta_ref[rows, :]
        g_col = gc_ref[rows, :]
        g_rows = gct_ref[0, c]
        b_rows = gct_ref[1, c]
        lanes = lambda a, h: jnp.broadcast_to(a[:, h:h + 1], (CHUNK, hd))

        k = [k_ref[rows, p * hd:(p + 1) * hd] for p in pairs]
        q = [q_ref[rows, p * hd:(p + 1) * hd] for p in pairs]
        kq = [lax.dot_general(jnp.concatenate([k[p], q[p]], axis=0), jnp.concatenate([k[p], k[p]], axis=0), _NT,
                              preferred_element_type=F32) for p in pairs]
        g_row = [g_rows[p:p + 1, :] for p in pairs]
        lmat, qk_decay, q_dec, k_dec = [], [], [], []
        for p in pairs:
            g_a, g_b = lanes(g_col, 2 * p), lanes(g_col, 2 * p + 1)
            b_pk = jnp.where(left, lanes(beta, 2 * p), lanes(beta, 2 * p + 1))
            decay = jnp.exp(jnp.where(tril, jnp.where(left, g_a, g_b) - g_row[p], -jnp.inf))
            lmat.append(jnp.where(strict, kq[p][:CHUNK] * b_pk * decay, 0.0))
            qk_decay.append((kq[p][CHUNK:] * decay).astype(BF16))
            qf, kf = q[p].astype(F32), k[p].astype(F32)
            q_dec.append(jnp.concatenate([qf * jnp.exp(g_a), qf * jnp.exp(g_b)], axis=1).astype(BF16))
            ge_a, ge_b = g_row[p][:, CHUNK - 1:CHUNK], g_row[p][:, 2 * CHUNK - 1:2 * CHUNK]
            k_dec.append(((kf * jnp.exp(ge_a - g_a)).astype(BF16), (kf * jnp.exp(ge_b - g_b)).astype(BF16)))
        lsplit = [_split_bf16(lmat[p]) for p in pairs]
        x = [pair_matmul(*lsplit[p], *lsplit[p]) for p in pairs]
        acc = [eye - lmat[p] for p in pairs]
        power = 2
        while power * 2 < CHUNK:
            for p in pairs:
                sh, sl = _split_bf16(jnp.concatenate([acc[p], x[p]], axis=0))
                both = pair_matmul(sh, sl, sh[CHUNK:], sl[CHUNK:])
                acc[p] = acc[p] + both[:CHUNK]
                x[p] = both[CHUNK:]
            power *= 2
        tmat = [acc[p] + pair_matmul(*_split_bf16(acc[p]), *_split_bf16(x[p])) for p in pairs]
        zb = jnp.zeros((CHUNK, hd), BF16)
        uw = []
        for p in pairs:
            b_row = b_rows[p:p + 1, :]
            t_scaled = jnp.concatenate([tmat[p] * b_row, tmat[p] * (b_row * jnp.exp(g_row[p]))], axis=1)
            v_pk = v_ref[rows, 2 * p * hd:2 * (p + 1) * hd]
            rhs = jnp.concatenate([jnp.concatenate([v_pk[:, :hd], zb, zb, zb], axis=1),
                                   jnp.concatenate([zb, v_pk[:, hd:], zb, zb], axis=1),
                                   jnp.concatenate([zb, zb, k[p], zb], axis=1),
                                   jnp.concatenate([zb, zb, zb, k[p]], axis=1)], axis=0)
            uw.append(jnp.dot(t_scaled.astype(BF16), rhs, preferred_element_type=F32))
        s_old = [state_scr[p] for p in pairs]
        wq = []
        for p in pairs:
            wq.append(jnp.dot(jnp.concatenate([uw[p][:, 2 * hd:].astype(BF16), q_dec[p]], axis=0),
                              block_diag(s_old[p].astype(BF16), state_left), preferred_element_type=F32))
        o_pk = []
        for p in pairs:
            vn = (uw[p][:, :2 * hd] - wq[p][:CHUNK]).astype(BF16)
            vn_diag = jnp.concatenate([jnp.concatenate([vn[:, :hd], zb], axis=1),
                                       jnp.concatenate([zb, vn[:, hd:]], axis=1)], axis=0)
            o_pk.append(wq[p][CHUNK:] + jnp.dot(qk_decay[p], vn_diag, preferred_element_type=F32))
            ge_a, ge_b = g_row[p][:, CHUNK - 1:CHUNK], g_row[p][:, 2 * CHUNK - 1:2 * CHUNK]
            ds_a = lax.dot_general(k_dec[p][0], vn[:, :hd], _TN, preferred_element_type=F32)
            ds_b = lax.dot_general(k_dec[p][1], vn[:, hd:], _TN, preferred_element_type=F32)
            state_scr[p] = jnp.concatenate([s_old[p][:, :hd] * jnp.exp(ge_a) + ds_a,
                                            s_old[p][:, hd:] * jnp.exp(ge_b) + ds_b], axis=1)
        for p in pairs:
            wide_cols = slice(2 * p * hd, 2 * (p + 1) * hd)
            z_pk = z_ref[rows, wide_cols].astype(F32)
            gated = [_rms_rows(o_pk[p][:, h * hd:(h + 1) * hd], ng_ref[...]) * _silu(z_pk[:, h * hd:(h + 1) * hd])
                     for h in range(2)]
            o_ref[rows, wide_cols] = jnp.concatenate(gated, axis=1).astype(o_ref.dtype)
        return carry

    lax.fori_loop(0, n_chunks, chunk_step, 0)


def _delta_call(qkvz, beta_g, gc_g, gct_g, norm_g, *, n_qk_heads, head_dim):
    s_len = qkvz.shape[0]
    n_groups = n_qk_heads // DELTA_PAIRS
    heads = 2 * DELTA_PAIRS
    rows = DELTA_CHUNKS * CHUNK
    qk_w, v_w = DELTA_PAIRS * head_dim, heads * head_dim
    key_dim, value_dim = n_qk_heads * head_dim, 2 * n_qk_heads * head_dim
    k_off, v_off, z_off = key_dim // qk_w, 2 * key_dim // v_w, (2 * key_dim + value_dim) // v_w
    block_bytes = 2 * rows * (2 * qk_w + 3 * v_w) * 2 + 6 * rows * V7X_LANES * 4 + heads * head_dim * head_dim * 4
    return pl.pallas_call(
        functools.partial(_delta_kernel, n_pairs=DELTA_PAIRS, n_chunks=DELTA_CHUNKS, head_dim=head_dim),
        name="gdn_delta_rule",
        out_shape=jax.ShapeDtypeStruct((s_len, value_dim), BF16),
        grid=(n_groups, s_len // rows),
        in_specs=[
            pl.BlockSpec((rows, qk_w), lambda g, b: (b, g)),
            pl.BlockSpec((rows, qk_w), lambda g, b: (b, k_off + g)),
            pl.BlockSpec((rows, v_w), lambda g, b: (b, v_off + g)),
            pl.BlockSpec((rows, v_w), lambda g, b: (b, z_off + g)),
            pl.BlockSpec((None, rows, heads), lambda g, b: (g, b, 0)),
            pl.BlockSpec((None, rows, heads), lambda g, b: (g, b, 0)),
            pl.BlockSpec((None, 2, DELTA_CHUNKS, DELTA_PAIRS, 2 * CHUNK), lambda g, b: (g, 0, b, 0, 0)),
            pl.BlockSpec((1, head_dim), lambda g, b: (0, 0)),
        ],
        out_specs=pl.BlockSpec((rows, v_w), lambda g, b: (b, g)),
        scratch_shapes=[pltpu.VMEM((DELTA_PAIRS, head_dim, 2 * head_dim), F32)],
        compiler_params=_params(("arbitrary", "arbitrary"), block_bytes),
    )(qkvz, qkvz, qkvz, qkvz, beta_g, gc_g, gct_g, norm_g.reshape(1, head_dim))


def _diff_attention_layer(x, gain, w_qkv, layer, q_norm, k_norm, lq1, lk1, lq2, lk2, subln, w_o, lambda_init):
    s_len = x.shape[0]
    head_dim = q_norm.shape[0]
    n_heads = w_o.shape[1] // subln.shape[0]
    qk_dim = 2 * n_heads * head_dim
    v_dim = w_qkv.shape[2] - 2 * qk_dim
    tile = min(ATTN_TILE, s_len)
    vec_spec = pl.BlockSpec((1, head_dim), lambda i, j: (0, 0))

    def proj(name, n_cols, first_col, gain_vec, scale, normalize, transpose):
        return _proj_call(
            name, functools.partial(_head_proj_kernel, head_dim=head_dim, scale=scale, normalize=normalize,
                                    transpose=transpose, tile=tile),
            x, gain, w_qkv, layer, n_cols, (gain_vec.reshape(1, -1),), (vec_spec,), BF16, (),
            first_col_tile=first_col // COL_TILE, transposed_tile=tile if transpose else None)

    qt = proj("attn_q_proj", qk_dim, 0, q_norm, head_dim ** -0.5, True, True)
    k = proj("attn_k_proj", qk_dim, qk_dim, k_norm, 1.0, True, False)
    vt = proj("attn_v_proj", v_dim, 2 * qk_dim, k_norm, 1.0, False, True)
    slopes = jnp.exp2(-8.0 * jnp.arange(1, n_heads + 1, dtype=F32) / n_heads)
    o = _attn_call(qt, k, vt, slopes, lq1, lk1, lq2, lk2, subln, n_heads=n_heads, head_dim=head_dim,
                   lambda_init=lambda_init)
    return _mm_res_call(o, w_o, layer, x)


def _gated_deltanet_layer(x, gain, w_in, layer, conv_w, a_log, dt_bias, norm_g, w_out):
    s_len, d = x.shape
    head_dim = norm_g.shape[0]
    n_v_heads = a_log.shape[0]
    n_qk_heads = n_v_heads // 2
    key_dim, value_dim = n_qk_heads * head_dim, n_v_heads * head_dim
    conv_dim = 2 * key_dim + value_dim
    main_cols = conv_dim + value_dim
    n_conv_tiles = conv_dim // COL_TILE
    cw_spec = pl.BlockSpec((None, conv_w.shape[1], COL_TILE), lambda i, j: (layer, 0, jnp.minimum(j, n_conv_tiles - 1)))
    qkvz = _proj_call(
        "gdn_in_proj", functools.partial(_gdn_in_kernel, n_q_tiles=key_dim // COL_TILE, n_qk_tiles=2 * key_dim // COL_TILE,
                          n_conv_tiles=n_conv_tiles, head_dim=head_dim, q_scale=head_dim ** -0.5),
        x, gain, w_in, layer, main_cols, (conv_w,), (cw_spec,), BF16,
        (pltpu.VMEM((n_conv_tiles, V7X_SUBLANES, COL_TILE), F32),))

    pad = V7X_LANES - n_v_heads
    w_b = jnp.pad(w_in[layer][:, main_cols:main_cols + n_v_heads], ((0, 0), (0, pad)))
    w_a = jnp.pad(w_in[layer][:, main_cols + n_v_heads:], ((0, 0), (0, pad)))
    w_ba = jnp.concatenate([w_b, w_a], axis=1)[None]
    ba = _proj_call("gdn_gate_proj", _plain_proj_kernel, x, gain, w_ba, 0, 2 * V7X_LANES, (), (), F32, (), col_tile=2 * V7X_LANES)
    beta, gc = _gates_call(ba, jnp.pad(a_log, (0, pad)).reshape(1, -1), jnp.pad(dt_bias, (0, pad)).reshape(1, -1))

    heads = 2 * DELTA_PAIRS
    n_groups = n_v_heads // heads
    beta_g = beta[:, :n_v_heads].reshape(s_len, n_groups, heads).transpose(1, 0, 2)
    gc_g = gc[:, :n_v_heads].reshape(s_len, n_groups, heads).transpose(1, 0, 2)
    rows_of = lambda a: a[:, :n_v_heads].reshape(s_len // CHUNK, CHUNK, n_groups, DELTA_PAIRS, 2).transpose(
        2, 0, 3, 4, 1).reshape(n_groups, s_len // CHUNK, DELTA_PAIRS, 2 * CHUNK)
    gct_g = jnp.stack([rows_of(gc), rows_of(beta)], axis=1)
    o = _delta_call(qkvz, beta_g, gc_g, gct_g, norm_g, n_qk_heads=n_qk_heads, head_dim=head_dim)
    return _mm_res_call(o, w_out, layer, x)


def _conv_ffn_layer(x, gain, w_up, conv_w, conv_b, w_down, layer):
    n_cols = w_up.shape[2]
    n_tiles = pl.cdiv(n_cols, COL_TILE)
    cw_spec = pl.BlockSpec((None, conv_w.shape[1], COL_TILE), lambda i, j: (layer, 0, j))
    cb_spec = pl.BlockSpec((None, 1, COL_TILE), lambda i, j: (layer, 0, j))
    u = _proj_call("ffn_up_proj", _ffn_up_kernel, x, gain, w_up, layer, n_cols, (conv_w, conv_b[:, None, :]), (cw_spec, cb_spec),
                   BF16, (pltpu.VMEM((n_tiles, V7X_SUBLANES, COL_TILE), F32),))
    return _ffn_down_call(u, w_down[layer].astype(BF16), x)


def kernel(x, mixer_norm, ffn_norm, diff_w_qkv, diff_q_norm, diff_k_norm, diff_lambda_q1, diff_lambda_k1,
           diff_lambda_q2, diff_lambda_k2, diff_subln, diff_w_o, gdn_w_in, gdn_conv_w, gdn_A_log, gdn_dt_bias,
           gdn_norm, gdn_w_out, ffn_w_up, ffn_conv_w, ffn_conv_b, ffn_w_down):
    b_sz, s_len, d = x.shape
    depth = mixer_norm.shape[0]
    outs = []
    for b in range(b_sz):
        xb = x.reshape(s_len, d) if b_sz == 1 else x[b]
        for i in range(depth):
            j = i // 2
            if i % 2 == 0:
                lambda_init = 0.8 - 0.6 * math.exp(-0.3 * i)
                xb = _diff_attention_layer(xb, mixer_norm[i], diff_w_qkv, j, diff_q_norm[j], diff_k_norm[j],
                                           diff_lambda_q1[j], diff_lambda_k1[j], diff_lambda_q2[j],
                                           diff_lambda_k2[j], diff_subln[j], diff_w_o, lambda_init)
            else:
                xb = _gated_deltanet_layer(xb, mixer_norm[i], gdn_w_in, j, gdn_conv_w, gdn_A_log[j], gdn_dt_bias[j],
                                           gdn_norm[j], gdn_w_out)
            xb = _conv_ffn_layer(xb, ffn_norm[i], ffn_w_up, ffn_conv_w, ffn_conv_b, ffn_w_down, i)
        outs.append(xb)
    return outs[0].reshape(x.shape) if b_sz == 1 else jnp.stack(outs, axis=0)
```

```python
import functools
import math

import jax
import jax.numpy as jnp
from jax import lax
from jax.experimental import pallas as pl
from jax.experimental.pallas import tpu as pltpu

F32 = jnp.float32
BF16 = jnp.bfloat16

RMS_EPS = 1e-6
LOG2_E = math.log2(math.e)
CHUNK = 64
V7X_VMEM_BYTES = 64 * 1024 * 1024
V7X_LANES = 128
V7X_SUBLANES = 8
COMPILER_TEMP_BYTES = 12 * 1024 * 1024

ROW_TILE = 1024
ROW_SUBTILE = 256
OUT_ROW_TILE = 512
DOWN_ROW_TILE = 256
COL_TILE = 512
ATTN_TILE = 512
DELTA_PAIRS = 16
DELTA_CHUNKS = 4
GATE_ROWS = 256

_NT = (((1,), (1,)), ((), ()))
_TN = (((0,), (0,)), ((), ()))


def _params(semantics, block_bytes):
    limit = min(int(block_bytes) + COMPILER_TEMP_BYTES, V7X_VMEM_BYTES - (4 << 20))
    return pltpu.CompilerParams(dimension_semantics=semantics, vmem_limit_bytes=limit)


def _rms_rows(x, gain):
    return x * lax.rsqrt(jnp.mean(x * x, axis=-1, keepdims=True) + RMS_EPS) * gain


def _silu(x):
    return x * jax.nn.sigmoid(x)


def _causal_conv(ext_scr, r, u, w_ref):
    kw = w_ref.shape[0]
    base = V7X_SUBLANES + r * ROW_SUBTILE
    out = u * w_ref[kw - 1:kw, :]
    for s in range(1, kw):
        out = out + ext_scr[base - s:base - s + ROW_SUBTILE, :] * w_ref[kw - 1 - s:kw - s, :]
    return out


def _normalize_once(x_ref, g_ref, h_scr):
    @pl.when(pl.program_id(1) == 0)
    def _():
        h_scr[...] = _rms_rows(x_ref[...], g_ref[...]).astype(BF16)


def _pipelined_rows(h_scr, w_bf16, epilogue, ext_scr=None):
    n_sub = h_scr.shape[0] // ROW_SUBTILE

    def piece(r):
        u = jnp.dot(h_scr[r * ROW_SUBTILE:(r + 1) * ROW_SUBTILE, :], w_bf16, preferred_element_type=F32)
        if ext_scr is not None:
            ext_scr[V7X_SUBLANES + r * ROW_SUBTILE:V7X_SUBLANES + (r + 1) * ROW_SUBTILE, :] = u
        return u

    u_prev = piece(0)
    for r in range(1, n_sub):
        u_next = piece(r)
        epilogue(r - 1, u_prev)
        u_prev = u_next
    epilogue(n_sub - 1, u_prev)


def _conv_pipelined_rows(h_scr, w_bf16, carry_ref, ext_scr, epilogue):
    ext_scr[0:V7X_SUBLANES, :] = carry_ref[...]
    _pipelined_rows(h_scr, w_bf16, epilogue, ext_scr)
    rows = h_scr.shape[0]
    carry_ref[...] = ext_scr[rows:rows + V7X_SUBLANES, :]


def _head_proj_kernel(x_ref, g_ref, w_ref, gain_ref, o_ref, h_scr, *, head_dim, scale, normalize, transpose, tile):
    _normalize_once(x_ref, g_ref, h_scr)

    def epilogue(r, acc):
        if normalize:
            acc = jnp.concatenate([_rms_rows(acc[:, c:c + head_dim], gain_ref[...]) * scale
                                   for c in range(0, acc.shape[1], head_dim)], axis=1)
        r0 = r * ROW_SUBTILE
        if transpose:
            o_ref[r0 // tile, :, r0 % tile:r0 % tile + ROW_SUBTILE] = acc.T.astype(o_ref.dtype)
        else:
            o_ref[r0:r0 + ROW_SUBTILE, :] = acc.astype(o_ref.dtype)

    _pipelined_rows(h_scr, w_ref[...].astype(BF16), epilogue)


def _plain_proj_kernel(x_ref, g_ref, w_ref, o_ref, h_scr):
    _normalize_once(x_ref, g_ref, h_scr)
    o_ref[...] = jnp.dot(h_scr[...], w_ref[...].astype(BF16), preferred_element_type=F32).astype(o_ref.dtype)


def _ffn_up_kernel(x_ref, g_ref, w_ref, cw_ref, cb_ref, o_ref, h_scr, carry_scr, ext_scr):
    i, j = pl.program_id(0), pl.program_id(1)
    _normalize_once(x_ref, g_ref, h_scr)

    @pl.when(i == 0)
    def _():
        carry_scr[j] = jnp.zeros(carry_scr.shape[1:], F32)

    def epilogue(r, u):
        y = _causal_conv(ext_scr, r, u, cw_ref) + cb_ref[...]
        o_ref[r * ROW_SUBTILE:(r + 1) * ROW_SUBTILE, :] = y.astype(o_ref.dtype)

    _conv_pipelined_rows(h_scr, w_ref[...].astype(BF16), carry_scr.at[j], ext_scr, epilogue)


def _gdn_in_kernel(x_ref, g_ref, w_ref, cw_ref, o_ref, h_scr, carry_scr, ext_scr, *, n_q_tiles, n_qk_tiles, n_conv_tiles,
                   head_dim, q_scale):
    i, j = pl.program_id(0), pl.program_id(1)
    _normalize_once(x_ref, g_ref, h_scr)

    def conv_branch(l2_scale):
        @pl.when(i == 0)
        def _():
            carry_scr[j] = jnp.zeros(carry_scr.shape[1:], F32)

        def epilogue(r, u):
            y = _silu(_causal_conv(ext_scr, r, u, cw_ref))
            if l2_scale is not None:
                y = jnp.concatenate(
                    [y[:, c:c + head_dim] * (lax.rsqrt(jnp.sum(y[:, c:c + head_dim] * y[:, c:c + head_dim], axis=-1,
                                                               keepdims=True) + RMS_EPS) * l2_scale)
                     for c in range(0, y.shape[1], head_dim)], axis=1)
            o_ref[r * ROW_SUBTILE:(r + 1) * ROW_SUBTILE, :] = y.astype(o_ref.dtype)

        _conv_pipelined_rows(h_scr, w_ref[...].astype(BF16), carry_scr.at[j], ext_scr, epilogue)

    @pl.when(j < n_q_tiles)
    def _():
        conv_branch(q_scale)

    @pl.when(jnp.logical_and(j >= n_q_tiles, j < n_qk_tiles))
    def _():
        conv_branch(1.0)

    @pl.when(jnp.logical_and(j >= n_qk_tiles, j < n_conv_tiles))
    def _():
        conv_branch(None)

    @pl.when(j >= n_conv_tiles)
    def _():
        o_ref[...] = jnp.dot(h_scr[...], w_ref[...].astype(BF16), preferred_element_type=F32).astype(o_ref.dtype)


def _proj_call(name, kernel, x, gain, w3, layer, n_cols, extra_inputs, extra_specs, out_dtype, scratch,
               col_tile=COL_TILE, first_col_tile=0, transposed_tile=None):
    s_len, d = x.shape
    tm, tn = min(ROW_TILE, s_len), col_tile
    grid = (s_len // tm, pl.cdiv(n_cols, tn))
    in_specs = [
        pl.BlockSpec((tm, d), lambda i, j: (i, 0)),
        pl.BlockSpec((1, d), lambda i, j: (0, 0)),
        pl.BlockSpec((None, d, tn), lambda i, j: (layer, 0, first_col_tile + j)),
    ] + list(extra_specs)
    if transposed_tile is None:
        out_shape = jax.ShapeDtypeStruct((s_len, n_cols), out_dtype)
        out_spec = pl.BlockSpec((tm, tn), lambda i, j: (i, j))
    else:
        out_shape = jax.ShapeDtypeStruct((s_len // transposed_tile, n_cols, transposed_tile), out_dtype)
        out_spec = pl.BlockSpec((tm // transposed_tile, tn, transposed_tile), lambda i, j: (i, j, 0))
    out_bytes = jnp.dtype(out_dtype).itemsize
    block_bytes = 2 * tm * d * 4 + tm * d * 2 + 2 * d * tn * 4 + d * tn * 2 + 2 * tm * tn * out_bytes + 3 * tm * tn * 4
    return pl.pallas_call(
        kernel,
        name=name,
        out_shape=out_shape,
        grid=grid,
        in_specs=in_specs,
        out_specs=out_spec,
        scratch_shapes=[pltpu.VMEM((tm, d), BF16)] + list(scratch),
        compiler_params=_params(("arbitrary", "arbitrary"), block_bytes),
    )(x, gain.reshape(1, d), w3, *extra_inputs)


def _mm_res_kernel(a_ref, w_ref, r_ref, o_ref, *w_scr):
    if w_scr:
        @pl.when(pl.program_id(0) == 0)
        def _():
            for r0 in range(0, w_ref.shape[0], COL_TILE):
                w_scr[0][r0:r0 + COL_TILE, :] = w_ref[r0:r0 + COL_TILE, :].astype(BF16)
        w = w_scr[0][...]
    else:
        w = w_ref[...]
    o_ref[...] = r_ref[...] + jnp.dot(a_ref[...], w, preferred_element_type=F32)


def _mm_res_call(a, w, res, layer=None):
    s_len, k = a.shape
    n = w.shape[-1]
    tm = min(OUT_ROW_TILE, s_len)
    once = pl.Buffered(1)
    if w.ndim == 3:
        w_spec = pl.BlockSpec((None, k, n), lambda i: (layer, 0, 0), pipeline_mode=once)
        scratch = [pltpu.VMEM((k, n), BF16)]
    else:
        w_spec = pl.BlockSpec((k, n), lambda i: (0, 0), pipeline_mode=once)
        scratch = []
    block_bytes = 2 * tm * k * 2 + k * n * w.dtype.itemsize + len(scratch) * k * n * 2 + 5 * tm * n * 4
    return pl.pallas_call(
        _mm_res_kernel,
        name="out_proj_residual",
        out_shape=jax.ShapeDtypeStruct((s_len, n), F32),
        grid=(s_len // tm,),
        in_specs=[pl.BlockSpec((tm, k), lambda i: (i, 0)), w_spec, pl.BlockSpec((tm, n), lambda i: (i, 0))],
        out_specs=pl.BlockSpec((tm, n), lambda i: (i, 0)),
        scratch_shapes=scratch,
        compiler_params=_params(("arbitrary",), block_bytes),
    )(a, w, res)


def _ffn_down_kernel(uv_ref, ug_ref, w_ref, r_ref, o_ref, g_scr):
    d_ff = w_ref.shape[0]
    mid = pl.cdiv(d_ff // 2, 2 * V7X_LANES) * 2 * V7X_LANES
    acc = r_ref[...]
    for k0, k1 in ((0, mid), (mid, d_ff)):
        for c0 in range(k0, k1, COL_TILE):
            cols = slice(c0, min(c0 + COL_TILE, k1))
            g_scr[:, cols] = (_silu(ug_ref[:, cols].astype(F32)) * uv_ref[:, cols].astype(F32)).astype(BF16)
        acc = acc + jnp.dot(g_scr[:, k0:k1], w_ref[k0:k1, :], preferred_element_type=F32)
    o_ref[...] = acc


def _ffn_down_call(u, w_down, layer, res):
    s_len = u.shape[0]
    _, d_ff, n = w_down.shape
    tm = min(DOWN_ROW_TILE, s_len)
    block_bytes = 4 * tm * d_ff * 2 + tm * d_ff * 2 + d_ff * n * 2 + 6 * tm * n * 4
    return pl.pallas_call(
        _ffn_down_kernel,
        name="ffn_down_residual",
        out_shape=jax.ShapeDtypeStruct((s_len, n), F32),
        grid=(s_len // tm,),
        in_specs=[
            pl.BlockSpec((tm, d_ff), lambda i: (i, 0)),
            pl.BlockSpec((tm, d_ff), lambda i: (i, 1)),
            pl.BlockSpec((None, d_ff, n), lambda i: (layer, 0, 0), pipeline_mode=pl.Buffered(1)),
            pl.BlockSpec((tm, n), lambda i: (i, 0)),
        ],
        out_specs=pl.BlockSpec((tm, n), lambda i: (i, 0)),
        scratch_shapes=[pltpu.VMEM((tm, d_ff), BF16)],
        compiler_params=_params(("arbitrary",), block_bytes),
    )(u, u, w_down, res)


def _attn_kernel(slopes_ref, qt_ref, k_ref, vt_ref, lq1_ref, lk1_ref, lq2_ref, lk2_ref, subln_ref, o_ref,
                 s_scr, kb_scr, m_scr, l_scr, acc_scr, *, tile, head_dim, lambda_init):
    h, qi = pl.program_id(0), pl.program_id(1)
    slope = slopes_ref[h]
    hd = head_dim

    def issue_scores(kj, slot):
        k_blk = k_ref[pl.ds(pl.multiple_of(kj * tile, tile), tile), :]
        for m in range(2):
            rows = slice(m * hd, (m + 1) * hd)
            s_scr[slot, m] = jnp.dot(k_blk[:, rows], qt_ref[rows, :], preferred_element_type=F32)

    issue_scores(qi, 1)
    issue_scores(0, 0)

    key = lax.broadcasted_iota(jnp.int32, (tile, tile), 0)
    qry = lax.broadcasted_iota(jnp.int32, (tile, tile), 1)
    kb_scr[...] = slope * key.astype(F32)
    bias = jnp.where((key // CHUNK) <= (qry // CHUNK), -slope * jnp.abs(key - qry).astype(F32), -jnp.inf)
    vt_diag = vt_ref[qi]
    for m in range(2):
        z = s_scr[1, m] + bias
        mx = jnp.max(z, axis=0, keepdims=True)
        p = jnp.exp2(z - mx)
        m_scr[m] = mx
        l_scr[m] = jnp.sum(p, axis=0, keepdims=True)
        acc_scr[m] = jnp.dot(vt_diag, p.astype(BF16), preferred_element_type=F32)

    q_pos = (lax.broadcasted_iota(jnp.int32, (1, tile), 1) + qi * tile).astype(F32)

    def kv_step(kj, slot, issue_next):
        if issue_next:
            issue_scores(jnp.minimum(kj + 1, qi - 1), 1 - slot)
        c = -slope * (q_pos - (kj * tile).astype(F32))
        vt_blk = vt_ref[kj]
        for m in range(2):
            z = s_scr[slot, m] + kb_scr[...]
            m_old = m_scr[m]
            m_new = jnp.maximum(m_old, jnp.max(z, axis=0, keepdims=True) + c)
            p = jnp.exp2(z - (m_new - c))
            alpha = jnp.exp2(m_old - m_new)
            l_scr[m] = alpha * l_scr[m] + jnp.sum(p, axis=0, keepdims=True)
            acc_scr[m] = alpha * acc_scr[m] + jnp.dot(vt_blk, p.astype(BF16), preferred_element_type=F32)
            m_scr[m] = m_new

    def two_steps(t, carry):
        kv_step(2 * t, 0, True)
        kv_step(2 * t + 1, 1, True)
        return carry

    lax.fori_loop(0, qi // 2, two_steps, 0)

    @pl.when(lax.rem(qi, 2) == 1)
    def _():
        kv_step(qi - 1, 0, False)

    lam = (jnp.exp(jnp.sum(lq1_ref[...] * lk1_ref[...], axis=-1, keepdims=True))
           - jnp.exp(jnp.sum(lq2_ref[...] * lk2_ref[...], axis=-1, keepdims=True)) + lambda_init)
    o_t = acc_scr[0] / l_scr[0] - lam * (acc_scr[1] / l_scr[1])
    inv = lax.rsqrt(jnp.mean(o_t * o_t, axis=0, keepdims=True) + RMS_EPS)
    o_ref[...] = ((o_t * inv * subln_ref[...]) * (1.0 - lambda_init)).T.astype(o_ref.dtype)


def _attn_call(qt, k, vt, slopes, lq1, lk1, lq2, lk2, subln, *, n_heads, head_dim, lambda_init):
    n_tiles, _, tile = qt.shape
    s_len = k.shape[0]
    hw = 2 * head_dim
    vec = lambda a: a.reshape(1, -1)
    vec_spec = lambda n: pl.BlockSpec((1, n), lambda h, qi: (0, 0))
    block_bytes = 2 * (2 * s_len * hw * 2) + 4 * tile * hw * 2 + 2 * tile * hw * 4 + 5 * tile * tile * 4 \
        + 6 * tile * tile * 4
    return pl.pallas_call(
        functools.partial(_attn_kernel, tile=tile, head_dim=head_dim, lambda_init=lambda_init),
        name="diff_attention",
        out_shape=jax.ShapeDtypeStruct((s_len, n_heads * hw), BF16),
        grid=(n_heads, n_tiles),
        in_specs=[
            pl.BlockSpec(memory_space=pltpu.SMEM),
            pl.BlockSpec((None, hw, tile), lambda h, qi: (qi, h, 0)),
            pl.BlockSpec((s_len, hw), lambda h, qi: (0, h)),
            pl.BlockSpec((n_tiles, hw, tile), lambda h, qi: (0, h, 0)),
            vec_spec(head_dim), vec_spec(head_dim), vec_spec(head_dim), vec_spec(head_dim),
            pl.BlockSpec((hw, 1), lambda h, qi: (0, 0)),
        ],
        out_specs=pl.BlockSpec((tile, hw), lambda h, qi: (qi, h)),
        scratch_shapes=[
            pltpu.VMEM((2, 2, tile, tile), F32),
            pltpu.VMEM((tile, tile), F32),
            pltpu.VMEM((2, 1, tile), F32),
            pltpu.VMEM((2, 1, tile), F32),
            pltpu.VMEM((2, hw, tile), F32),
        ],
        compiler_params=_params(("arbitrary", "arbitrary"), block_bytes),
    )(slopes, qt, k, vt, vec(lq1), vec(lk1), vec(lq2), vec(lk2), subln.reshape(-1, 1))


def _gates_kernel(ba_ref, alog_ref, dtb_ref, beta_ref, gc_ref):
    rows = ba_ref.shape[0]
    beta_ref[...] = jax.nn.sigmoid(ba_ref[...])
    g = -jnp.exp(alog_ref[...]) * jax.nn.softplus(ba_ref[...] + dtb_ref[...])
    r = lax.broadcasted_iota(jnp.int32, (rows, rows), 0)
    c = lax.broadcasted_iota(jnp.int32, (rows, rows), 1)
    tri = jnp.where(r // CHUNK == c // CHUNK, jnp.where(c <= r, 1.0, 0.0), 0.0).astype(F32)
    gc_ref[...] = jnp.dot(tri, g, preferred_element_type=F32, precision=lax.Precision.HIGHEST)


def _gates_call(ba, a_log_row, dt_bias_row):
    s_len = ba.shape[0]
    tm = min(GATE_ROWS, s_len)
    out = jax.ShapeDtypeStruct((s_len, V7X_LANES), F32)
    return pl.pallas_call(
        _gates_kernel,
        name="gdn_gates",
        out_shape=(out, out),
        grid=(s_len // tm,),
        in_specs=[
            pl.BlockSpec((tm, V7X_LANES), lambda i: (i, 0)),
            pl.BlockSpec((1, V7X_LANES), lambda i: (0, 0)),
            pl.BlockSpec((1, V7X_LANES), lambda i: (0, 0)),
        ],
        out_specs=(pl.BlockSpec((tm, V7X_LANES), lambda i: (i, 0)), pl.BlockSpec((tm, V7X_LANES), lambda i: (i, 0))),
        compiler_params=_params(("arbitrary",), 8 * tm * 2 * V7X_LANES * 4 + 4 * tm * tm * 4),
    )(ba, a_log_row, dt_bias_row)


def _split_bf16(x):
    hi = x.astype(BF16)
    return hi, (x - hi.astype(F32)).astype(BF16)


def _delta_kernel(q_ref, k_ref, v_ref, z_ref, beta_ref, gc_ref, gct_ref, ng_ref, o_ref, state_scr, *,
                  n_pairs, n_chunks, head_dim):
    hd = head_dim
    pairs = range(n_pairs)

    @pl.when(pl.program_id(1) == 0)
    def _():
        state_scr[...] = jnp.zeros(state_scr.shape, F32)

    ri = lax.broadcasted_iota(jnp.int32, (CHUNK, 2 * CHUNK), 0)
    lane = lax.broadcasted_iota(jnp.int32, (CHUNK, 2 * CHUNK), 1)
    left = lane < CHUNK
    ci = jnp.where(left, lane, lane - CHUNK)
    tril = ci <= ri
    strict = ci < ri
    eye = jnp.where(ci == ri, 1.0, 0.0).astype(F32)
    state_left = lax.broadcasted_iota(jnp.int32, (hd, 2 * hd), 1) < hd

    def block_diag(y, keep_left):
        zero = jnp.zeros_like(y)
        return jnp.concatenate([jnp.where(keep_left, y, zero), jnp.where(keep_left, zero, y)], axis=0)

    def pair_matmul(xh, xl, yh, yl):
        bh, bl = block_diag(yh, left), block_diag(yl, left)
        rhs = jnp.concatenate([jnp.concatenate([bh, bl], axis=1),
                               jnp.concatenate([bh, jnp.zeros_like(bl)], axis=1)], axis=0)
        out = jnp.dot(jnp.concatenate([xh, xl], axis=1), rhs, preferred_element_type=F32)
        return out[:, :2 * CHUNK] + out[:, 2 * CHUNK:]

    def chunk_step(c, carry):
        rows = pl.ds(pl.multiple_of(c * CHUNK, CHUNK), CHUNK)
        beta = beta_ref[rows, :]
        g_col = gc_ref[rows, :]
        g_rows = gct_ref[0, c]
        b_rows = gct_ref[1, c]
        lanes = lambda a, h: jnp.broadcast_to(a[:, h:h + 1], (CHUNK, hd))

        k = [k_ref[rows, p * hd:(p + 1) * hd] for p in pairs]
        q = [q_ref[rows, p * hd:(p + 1) * hd] for p in pairs]
        kq = [lax.dot_general(jnp.concatenate([k[p], q[p]], axis=0), jnp.concatenate([k[p], k[p]], axis=0), _NT,
                              preferred_element_type=F32) for p in pairs]
        g_row = [g_rows[p:p + 1, :] for p in pairs]
        lmat, qk_decay, q_dec, k_dec = [], [], [], []
        for p in pairs:
            g_a, g_b = lanes(g_col, 2 * p), lanes(g_col, 2 * p + 1)
            b_pk = jnp.where(left, lanes(beta, 2 * p), lanes(beta, 2 * p + 1))
            decay = jnp.exp(jnp.where(tril, jnp.where(left, g_a, g_b) - g_row[p], -jnp.inf))
            lmat.append(jnp.where(strict, kq[p][:CHUNK] * b_pk * decay, 0.0))
            qk_decay.append((kq[p][CHUNK:] * decay).astype(BF16))
            qf, kf = q[p].astype(F32), k[p].astype(F32)
            q_dec.append(jnp.concatenate([qf * jnp.exp(g_a), qf * jnp.exp(g_b)], axis=1).astype(BF16))
            ge_a, ge_b = g_row[p][:, CHUNK - 1:CHUNK], g_row[p][:, 2 * CHUNK - 1:2 * CHUNK]
            k_dec.append(((kf * jnp.exp(ge_a - g_a)).astype(BF16), (kf * jnp.exp(ge_b - g_b)).astype(BF16)))
        lsplit = [_split_bf16(lmat[p]) for p in pairs]
        x = [pair_matmul(*lsplit[p], *lsplit[p]) for p in pairs]
        acc = [eye - lmat[p] for p in pairs]
        power = 2
        while power * 2 < CHUNK:
            for p in pairs:
                sh, sl = _split_bf16(jnp.concatenate([acc[p], x[p]], axis=0))
                both = pair_matmul(sh, sl, sh[CHUNK:], sl[CHUNK:])
                acc[p] = acc[p] + both[:CHUNK]
                x[p] = both[CHUNK:]
            power *= 2
        tmat = [acc[p] + pair_matmul(*_split_bf16(acc[p]), *_split_bf16(x[p])) for p in pairs]
        zb = jnp.zeros((CHUNK, hd), BF16)
        uw = []
        for p in pairs:
            b_row = b_rows[p:p + 1, :]
            t_scaled = jnp.concatenate([tmat[p] * b_row, tmat[p] * (b_row * jnp.exp(g_row[p]))], axis=1)
            v_pk = v_ref[rows, 2 * p * hd:2 * (p + 1) * hd]
            rhs = jnp.concatenate([jnp.concatenate([v_pk[:, :hd], zb, zb, zb], axis=1),
                                   jnp.concatenate([zb, v_pk[:, hd:], zb, zb], axis=1),
                                   jnp.concatenate([zb, zb, k[p], zb], axis=1),
                                   jnp.concatenate([zb, zb, zb, k[p]], axis=1)], axis=0)
            uw.append(jnp.dot(t_scaled.astype(BF16), rhs, preferred_element_type=F32))
        s_old = [state_scr[p] for p in pairs]
        wq = []
        for p in pairs:
            wq.append(jnp.dot(jnp.concatenate([uw[p][:, 2 * hd:].astype(BF16), q_dec[p]], axis=0),
                              block_diag(s_old[p].astype(BF16), state_left), preferred_element_type=F32))
        o_pk = []
        for p in pairs:
            vn = (uw[p][:, :2 * hd] - wq[p][:CHUNK]).astype(BF16)
            vn_diag = jnp.concatenate([jnp.concatenate([vn[:, :hd], zb], axis=1),
                                       jnp.concatenate([zb, vn[:, hd:]], axis=1)], axis=0)
            o_pk.append(wq[p][CHUNK:] + jnp.dot(qk_decay[p], vn_diag, preferred_element_type=F32))
            ge_a, ge_b = g_row[p][:, CHUNK - 1:CHUNK], g_row[p][:, 2 * CHUNK - 1:2 * CHUNK]
            ds_a = lax.dot_general(k_dec[p][0], vn[:, :hd], _TN, preferred_element_type=F32)
            ds_b = lax.dot_general(k_dec[p][1], vn[:, hd:], _TN, preferred_element_type=F32)
            state_scr[p] = jnp.concatenate([s_old[p][:, :hd] * jnp.exp(ge_a) + ds_a,
                                            s_old[p][:, hd:] * jnp.exp(ge_b) + ds_b], axis=1)
        for p in pairs:
            wide_cols = slice(2 * p * hd, 2 * (p + 1) * hd)
            z_pk = z_ref[rows, wide_cols].astype(F32)
            gated = [_rms_rows(o_pk[p][:, h * hd:(h + 1) * hd], ng_ref[...]) * _silu(z_pk[:, h * hd:(h + 1) * hd])
                     for h in range(2)]
            o_ref[rows, wide_cols] = jnp.concatenate(gated, axis=1).astype(o_ref.dtype)
        return carry

    lax.fori_loop(0, n_chunks, chunk_step, 0)


def _delta_call(qkvz, beta_g, gc_g, gct_g, norm_g, *, n_qk_heads, head_dim):
    s_len = qkvz.shape[0]
    n_groups = n_qk_heads // DELTA_PAIRS
    heads = 2 * DELTA_PAIRS
    rows = DELTA_CHUNKS * CHUNK
    qk_w, v_w = DELTA_PAIRS * head_dim, heads * head_dim
    key_dim, value_dim = n_qk_heads * head_dim, 2 * n_qk_heads * head_dim
    k_off, v_off, z_off = key_dim // qk_w, 2 * key_dim // v_w, (2 * key_dim + value_dim) // v_w
    block_bytes = 2 * rows * (2 * qk_w + 3 * v_w) * 2 + 6 * rows * V7X_LANES * 4 + heads * head_dim * head_dim * 4
    return pl.pallas_call(
        functools.partial(_delta_kernel, n_pairs=DELTA_PAIRS, n_chunks=DELTA_CHUNKS, head_dim=head_dim),
        name="gdn_delta_rule",
        out_shape=jax.ShapeDtypeStruct((s_len, value_dim), BF16),
        grid=(n_groups, s_len // rows),
        in_specs=[
            pl.BlockSpec((rows, qk_w), lambda g, b: (b, g)),
            pl.BlockSpec((rows, qk_w), lambda g, b: (b, k_off + g)),
            pl.BlockSpec((rows, v_w), lambda g, b: (b, v_off + g)),
            pl.BlockSpec((rows, v_w), lambda g, b: (b, z_off + g)),
            pl.BlockSpec((None, rows, heads), lambda g, b: (g, b, 0)),
            pl.BlockSpec((None, rows, heads), lambda g, b: (g, b, 0)),
            pl.BlockSpec((None, 2, DELTA_CHUNKS, DELTA_PAIRS, 2 * CHUNK), lambda g, b: (g, 0, b, 0, 0)),
            pl.BlockSpec((1, head_dim), lambda g, b: (0, 0)),
        ],
        out_specs=pl.BlockSpec((rows, v_w), lambda g, b: (b, g)),
        scratch_shapes=[pltpu.VMEM((DELTA_PAIRS, head_dim, 2 * head_dim), F32)],
        compiler_params=_params(("arbitrary", "arbitrary"), block_bytes),
    )(qkvz, qkvz, qkvz, qkvz, beta_g, gc_g, gct_g, norm_g.reshape(1, head_dim))


def _diff_attention_layer(x, gain, w_qkv, layer, q_norm, k_norm, lq1, lk1, lq2, lk2, subln, w_o, lambda_init):
    s_len = x.shape[0]
    head_dim = q_norm.shape[0]
    n_heads = w_o.shape[1] // subln.shape[0]
    qk_dim = 2 * n_heads * head_dim
    v_dim = w_qkv.shape[2] - 2 * qk_dim
    tile = min(ATTN_TILE, s_len)
    vec_spec = pl.BlockSpec((1, head_dim), lambda i, j: (0, 0))

    def proj(name, n_cols, first_col, gain_vec, scale, normalize, transpose):
        return _proj_call(
            name, functools.partial(_head_proj_kernel, head_dim=head_dim, scale=scale, normalize=normalize,
                                    transpose=transpose, tile=tile),
            x, gain, w_qkv, layer, n_cols, (gain_vec.reshape(1, -1),), (vec_spec,), BF16, (),
            first_col_tile=first_col // COL_TILE, transposed_tile=tile if transpose else None)

    qt = proj("attn_q_proj", qk_dim, 0, q_norm, head_dim ** -0.5 * LOG2_E, True, True)
    k = proj("attn_k_proj", qk_dim, qk_dim, k_norm, 1.0, True, False)
    vt = proj("attn_v_proj", v_dim, 2 * qk_dim, k_norm, 1.0, False, True)
    slopes = jnp.exp2(-8.0 * jnp.arange(1, n_heads + 1, dtype=F32) / n_heads) * LOG2_E
    o = _attn_call(qt, k, vt, slopes, lq1, lk1, lq2, lk2, subln, n_heads=n_heads, head_dim=head_dim,
                   lambda_init=lambda_init)
    return _mm_res_call(o, w_o, x, layer)


def _gated_deltanet_layer(x, gain, w_in, layer, conv_w, a_log, dt_bias, norm_g, w_out):
    s_len, d = x.shape
    head_dim = norm_g.shape[0]
    n_v_heads = a_log.shape[0]
    n_qk_heads = n_v_heads // 2
    key_dim, value_dim = n_qk_heads * head_dim, n_v_heads * head_dim
    conv_dim = 2 * key_dim + value_dim
    main_cols = conv_dim + value_dim
    n_conv_tiles = conv_dim // COL_TILE
    cw_spec = pl.BlockSpec((None, conv_w.shape[1], COL_TILE), lambda i, j: (layer, 0, jnp.minimum(j, n_conv_tiles - 1)))
    qkvz = _proj_call(
        "gdn_in_proj", functools.partial(_gdn_in_kernel, n_q_tiles=key_dim // COL_TILE, n_qk_tiles=2 * key_dim // COL_TILE,
                          n_conv_tiles=n_conv_tiles, head_dim=head_dim, q_scale=head_dim ** -0.5),
        x, gain, w_in, layer, main_cols, (conv_w,), (cw_spec,), BF16,
        (pltpu.VMEM((n_conv_tiles, V7X_SUBLANES, COL_TILE), F32),
         pltpu.VMEM((V7X_SUBLANES + min(ROW_TILE, s_len), COL_TILE), F32)))

    assert main_cols % V7X_LANES == 0 and 2 * n_v_heads <= V7X_LANES
    ba = _proj_call("gdn_gate_proj", _plain_proj_kernel, x, gain, w_in, layer, V7X_LANES, (), (), F32, (),
                    col_tile=V7X_LANES, first_col_tile=main_cols // V7X_LANES)
    on_a_lanes = lambda v: jnp.pad(v, (n_v_heads, V7X_LANES - 2 * n_v_heads)).reshape(1, -1)
    beta, gc = _gates_call(ba, on_a_lanes(a_log), on_a_lanes(dt_bias))
    beta, gc = beta[:, :n_v_heads], gc[:, n_v_heads:2 * n_v_heads]

    heads = 2 * DELTA_PAIRS
    n_groups = n_v_heads // heads
    beta_g = beta.reshape(s_len, n_groups, heads).transpose(1, 0, 2)
    gc_g = gc.reshape(s_len, n_groups, heads).transpose(1, 0, 2)
    rows_of = lambda a: a.reshape(s_len // CHUNK, CHUNK, n_groups, DELTA_PAIRS, 2).transpose(
        2, 0, 3, 4, 1).reshape(n_groups, s_len // CHUNK, DELTA_PAIRS, 2 * CHUNK)
    gct_g = jnp.stack([rows_of(gc), rows_of(beta)], axis=1)
    o = _delta_call(qkvz, beta_g, gc_g, gct_g, norm_g, n_qk_heads=n_qk_heads, head_dim=head_dim)
    return _mm_res_call(o, w_out[layer].astype(BF16), x)


def _conv_ffn_layer(x, gain, w_up, conv_w, conv_b, w_down, layer):
    n_cols = w_up.shape[2]
    n_tiles = pl.cdiv(n_cols, COL_TILE)
    cw_spec = pl.BlockSpec((None, conv_w.shape[1], COL_TILE), lambda i, j: (layer, 0, j))
    cb_spec = pl.BlockSpec((None, 1, COL_TILE), lambda i, j: (layer, 0, j))
    u = _proj_call("ffn_up_proj", _ffn_up_kernel, x, gain, w_up, layer, n_cols, (conv_w, conv_b[:, None, :]), (cw_spec, cb_spec),
                   BF16, (pltpu.VMEM((n_tiles, V7X_SUBLANES, COL_TILE), F32),
                          pltpu.VMEM((V7X_SUBLANES + min(ROW_TILE, x.shape[0]), COL_TILE), F32)))
    return _ffn_down_call(u, w_down.astype(BF16), layer, x)


def kernel(x, mixer_norm, ffn_norm, diff_w_qkv, diff_q_norm, diff_k_norm, diff_lambda_q1, diff_lambda_k1,
           diff_lambda_q2, diff_lambda_k2, diff_subln, diff_w_o, gdn_w_in, gdn_conv_w, gdn_A_log, gdn_dt_bias,
           gdn_norm, gdn_w_out, ffn_w_up, ffn_conv_w, ffn_conv_b, ffn_w_down):
    b_sz, s_len, d = x.shape
    depth = mixer_norm.shape[0]
    outs = []
    for b in range(b_sz):
        xb = x.reshape(s_len, d) if b_sz == 1 else x[b]
        for i in range(depth):
            j = i // 2
            if i % 2 == 0:
                lambda_init = 0.8 - 0.6 * math.exp(-0.3 * i)
                xb = _diff_attention_layer(xb, mixer_norm[i], diff_w_qkv, j, diff_q_norm[j], diff_k_norm[j],
                                           diff_lambda_q1[j], diff_lambda_k1[j], diff_lambda_q2[j],
                                           diff_lambda_k2[j], diff_subln[j], diff_w_o, lambda_init)
            else:
                xb = _gated_deltanet_layer(xb, mixer_norm[i], gdn_w_in, j, gdn_conv_w, gdn_A_log[j], gdn_dt_bias[j],
                                           gdn_norm[j], gdn_w_out)
            xb = _conv_ffn_layer(xb, ffn_norm[i], ffn_w_up, ffn_conv_w, ffn_conv_b, ffn_w_down, i)
        outs.append(xb)
    return outs[0].reshape(x.shape) if b_sz == 1 else jnp.stack(outs, axis=0)
```

```python
import functools
import math

import jax
import jax.numpy as jnp
from jax import lax
from jax.experimental import pallas as pl
from jax.experimental.pallas import tpu as pltpu

F32 = jnp.float32
BF16 = jnp.bfloat16

RMS_EPS = 1e-6
LOG2_E = math.log2(math.e)
CHUNK = 64
V7X_VMEM_BYTES = 64 * 1024 * 1024
V7X_LANES = 128
V7X_SUBLANES = 8
COMPILER_TEMP_BYTES = 12 * 1024 * 1024

ROW_TILE = 1024
ROW_SUBTILE = 256
OUT_ROW_TILE = 512
DOWN_ROW_TILE = 256
COL_TILE = 512
ATTN_TILE = 512
DELTA_PAIRS = 16
DELTA_CHUNKS = 4
GATE_ROWS = 256

_NT = (((1,), (1,)), ((), ()))
_TN = (((0,), (0,)), ((), ()))


def _params(semantics, block_bytes):
    limit = min(int(block_bytes) + COMPILER_TEMP_BYTES, V7X_VMEM_BYTES - (4 << 20))
    return pltpu.CompilerParams(dimension_semantics=semantics, vmem_limit_bytes=limit)


def _rms_rows(x, gain):
    return x * lax.rsqrt(jnp.mean(x * x, axis=-1, keepdims=True) + RMS_EPS) * gain


def _silu(x):
    return x * jax.nn.sigmoid(x)


def _matmul(h, w_bf16, w_transposed):
    if w_transposed:
        return lax.dot_general(h, w_bf16, _NT, preferred_element_type=F32)
    return jnp.dot(h, w_bf16, preferred_element_type=F32)


def _normalize_once(x_ref, g_ref, h_scr):
    @pl.when(pl.program_id(1) == 0)
    def _():
        h_scr[...] = _rms_rows(x_ref[...], g_ref[...]).astype(BF16)


def _pipelined_rows(h_scr, w_bf16, w_transposed, epilogue):
    n_sub = h_scr.shape[0] // ROW_SUBTILE
    piece = lambda r: _matmul(h_scr[r * ROW_SUBTILE:(r + 1) * ROW_SUBTILE, :], w_bf16, w_transposed)
    u_prev = piece(0)
    for r in range(1, n_sub):
        u_next = piece(r)
        epilogue(r - 1, u_prev)
        u_prev = u_next
    epilogue(n_sub - 1, u_prev)


def _head_proj_kernel(x_ref, g_ref, w_ref, gain_ref, o_ref, h_scr, *, head_dim, scale, normalize, transpose, tile,
                      w_transposed):
    _normalize_once(x_ref, g_ref, h_scr)

    def epilogue(r, acc):
        if normalize:
            acc = jnp.concatenate([_rms_rows(acc[:, c:c + head_dim], gain_ref[...]) * scale
                                   for c in range(0, acc.shape[1], head_dim)], axis=1)
        r0 = r * ROW_SUBTILE
        if transpose:
            o_ref[r0 // tile, :, r0 % tile:r0 % tile + ROW_SUBTILE] = acc.T.astype(o_ref.dtype)
        else:
            o_ref[r0:r0 + ROW_SUBTILE, :] = acc.astype(o_ref.dtype)

    _pipelined_rows(h_scr, w_ref[...].astype(BF16), w_transposed, epilogue)


def _plain_proj_kernel(x_ref, g_ref, w_ref, o_ref, h_scr, *, w_transposed):
    _normalize_once(x_ref, g_ref, h_scr)
    o_ref[...] = _matmul(h_scr[...], w_ref[...].astype(BF16), w_transposed).astype(o_ref.dtype)


def _conv_proj_steps(h_scr, w_ref, cw_ref, o_ref, carry_scr, u_scr, *, piece_rows, w_transposed, finish):
    i, j = pl.program_id(0), pl.program_id(1)
    rows, kw = h_scr.shape[0], cw_ref.shape[0]
    head = V7X_SUBLANES

    @pl.when(jnp.logical_and(i == 0, j == 0))
    def _():
        carry_scr[...] = jnp.zeros(carry_scr.shape, F32)

    u_scr[0:head, :] = carry_scr[j]
    w_bf16 = w_ref[...].astype(BF16)

    def matmul_piece(r):
        piece = slice(r * piece_rows, (r + 1) * piece_rows)
        u_scr[head + r * piece_rows:head + (r + 1) * piece_rows, :] = _matmul(h_scr[piece, :], w_bf16, w_transposed)

    def epilogue_piece(r):
        for r0 in range(r * piece_rows, (r + 1) * piece_rows, ROW_SUBTILE):
            y = u_scr[head + r0:head + r0 + ROW_SUBTILE, :] * cw_ref[kw - 1:kw, :]
            for s in range(1, kw):
                y = y + u_scr[head + r0 - s:head + r0 - s + ROW_SUBTILE, :] * cw_ref[kw - 1 - s:kw - s, :]
            o_ref[r0:r0 + ROW_SUBTILE, :] = finish(y, j).astype(o_ref.dtype)

    matmul_piece(0)
    for r in range(1, rows // piece_rows):
        matmul_piece(r)
        epilogue_piece(r - 1)
    epilogue_piece(rows // piece_rows - 1)
    carry_scr[j] = u_scr[rows:rows + head, :]


def _ffn_up_kernel(x_ref, g_ref, w_ref, cw_ref, cb_ref, o_ref, h_scr, carry_scr, u_scr):
    _normalize_once(x_ref, g_ref, h_scr)
    _conv_proj_steps(h_scr, w_ref, cw_ref, o_ref, carry_scr, u_scr, piece_rows=h_scr.shape[0], w_transposed=False,
                     finish=lambda y, jj: y + cb_ref[...])


def _gdn_conv_kernel(x_ref, g_ref, w_ref, cw_ref, o_ref, h_scr, carry_scr, u_scr, *, n_q_tiles, n_qk_tiles, head_dim,
                     q_scale):
    _normalize_once(x_ref, g_ref, h_scr)

    def finish(y, jj):
        y = _silu(y)
        l2_scale = jnp.where(jj < n_q_tiles, q_scale, 1.0)
        heads = []
        for c in range(0, y.shape[1], head_dim):
            blk = y[:, c:c + head_dim]
            inv = lax.rsqrt(jnp.sum(blk * blk, axis=-1, keepdims=True) + RMS_EPS) * l2_scale
            heads.append(blk * jnp.where(jj < n_qk_tiles, inv, 1.0))
        return jnp.concatenate(heads, axis=1)

    _conv_proj_steps(h_scr, w_ref, cw_ref, o_ref, carry_scr, u_scr, piece_rows=ROW_SUBTILE, w_transposed=True,
                     finish=finish)


def _proj_call(name, kernel, x, gain, w3, layer, n_cols, extra_inputs, extra_specs, out_dtype, scratch,
               col_tile=COL_TILE, first_col_tile=0, transposed_tile=None, w_transposed=False):
    s_len, d = x.shape
    tm, tn = min(ROW_TILE, s_len), col_tile
    grid = (s_len // tm, pl.cdiv(n_cols, tn))
    if w_transposed:
        w_spec = pl.BlockSpec((None, tn, d), lambda i, j: (layer, first_col_tile + j, 0))
    else:
        w_spec = pl.BlockSpec((None, d, tn), lambda i, j: (layer, 0, first_col_tile + j))
    in_specs = [pl.BlockSpec((tm, d), lambda i, j: (i, 0)), pl.BlockSpec((1, d), lambda i, j: (0, 0)), w_spec]
    in_specs += list(extra_specs)
    if transposed_tile is None:
        out_shape = jax.ShapeDtypeStruct((s_len, n_cols), out_dtype)
        out_spec = pl.BlockSpec((tm, tn), lambda i, j: (i, j))
    else:
        out_shape = jax.ShapeDtypeStruct((s_len // transposed_tile, n_cols, transposed_tile), out_dtype)
        out_spec = pl.BlockSpec((tm // transposed_tile, tn, transposed_tile), lambda i, j: (i, j, 0))
    out_bytes = jnp.dtype(out_dtype).itemsize
    block_bytes = 2 * tm * d * 4 + tm * d * 2 + 2 * d * tn * 4 + d * tn * 2 + 2 * tm * tn * out_bytes + 3 * tm * tn * 4
    return pl.pallas_call(
        kernel,
        name=name,
        out_shape=out_shape,
        grid=grid,
        in_specs=in_specs,
        out_specs=out_spec,
        scratch_shapes=[pltpu.VMEM((tm, d), BF16)] + list(scratch),
        compiler_params=_params(("arbitrary", "arbitrary"), block_bytes),
    )(x, gain.reshape(1, d), w3, *extra_inputs)


def _conv_scratch(s_len, n_tiles):
    tm = min(ROW_TILE, s_len)
    return (pltpu.VMEM((n_tiles, V7X_SUBLANES, COL_TILE), F32), pltpu.VMEM((V7X_SUBLANES + tm, COL_TILE), F32))


def _mm_res_kernel(a_ref, w_ref, r_ref, o_ref, *w_scr):
    if w_scr:
        @pl.when(pl.program_id(0) == 0)
        def _():
            for r0 in range(0, w_ref.shape[0], COL_TILE):
                w_scr[0][r0:r0 + COL_TILE, :] = w_ref[r0:r0 + COL_TILE, :].astype(BF16)
        w = w_scr[0][...]
    else:
        w = w_ref[...]
    o_ref[...] = r_ref[...] + jnp.dot(a_ref[...], w, preferred_element_type=F32)


def _mm_res_call(a, w, res, layer=None):
    s_len, k = a.shape
    n = w.shape[-1]
    tm = min(OUT_ROW_TILE, s_len)
    once = pl.Buffered(1)
    if w.ndim == 3:
        w_spec = pl.BlockSpec((None, k, n), lambda i: (layer, 0, 0), pipeline_mode=once)
        scratch = [pltpu.VMEM((k, n), BF16)]
    else:
        w_spec = pl.BlockSpec((k, n), lambda i: (0, 0), pipeline_mode=once)
        scratch = []
    block_bytes = 2 * tm * k * 2 + k * n * w.dtype.itemsize + len(scratch) * k * n * 2 + 5 * tm * n * 4
    return pl.pallas_call(
        _mm_res_kernel,
        name="out_proj_residual",
        out_shape=jax.ShapeDtypeStruct((s_len, n), F32),
        grid=(s_len // tm,),
        in_specs=[pl.BlockSpec((tm, k), lambda i: (i, 0)), w_spec, pl.BlockSpec((tm, n), lambda i: (i, 0))],
        out_specs=pl.BlockSpec((tm, n), lambda i: (i, 0)),
        scratch_shapes=scratch,
        compiler_params=_params(("arbitrary",), block_bytes),
    )(a, w, res)


def _ffn_down_kernel(uv_ref, ug_ref, w_ref, r_ref, o_ref, g_scr):
    d_ff = w_ref.shape[0]
    mid = pl.cdiv(d_ff // 2, 2 * V7X_LANES) * 2 * V7X_LANES
    acc = r_ref[...]
    for k0, k1 in ((0, mid), (mid, d_ff)):
        for c0 in range(k0, k1, COL_TILE):
            cols = slice(c0, min(c0 + COL_TILE, k1))
            g_scr[:, cols] = (_silu(ug_ref[:, cols].astype(F32)) * uv_ref[:, cols].astype(F32)).astype(BF16)
        acc = acc + jnp.dot(g_scr[:, k0:k1], w_ref[k0:k1, :], preferred_element_type=F32)
    o_ref[...] = acc


def _ffn_down_call(u, w_down, layer, res):
    s_len = u.shape[0]
    _, d_ff, n = w_down.shape
    tm = min(DOWN_ROW_TILE, s_len)
    block_bytes = 4 * tm * d_ff * 2 + tm * d_ff * 2 + d_ff * n * 2 + 6 * tm * n * 4
    return pl.pallas_call(
        _ffn_down_kernel,
        name="ffn_down_residual",
        out_shape=jax.ShapeDtypeStruct((s_len, n), F32),
        grid=(s_len // tm,),
        in_specs=[
            pl.BlockSpec((tm, d_ff), lambda i: (i, 0)),
            pl.BlockSpec((tm, d_ff), lambda i: (i, 1)),
            pl.BlockSpec((None, d_ff, n), lambda i: (layer, 0, 0), pipeline_mode=pl.Buffered(1)),
            pl.BlockSpec((tm, n), lambda i: (i, 0)),
        ],
        out_specs=pl.BlockSpec((tm, n), lambda i: (i, 0)),
        scratch_shapes=[pltpu.VMEM((tm, d_ff), BF16)],
        compiler_params=_params(("arbitrary",), block_bytes),
    )(u, u, w_down, res)


def _attn_kernel(slopes_ref, qt_ref, k_ref, vt_ref, lq1_ref, lk1_ref, lq2_ref, lk2_ref, subln_ref, o_ref,
                 s_scr, kb_scr, m_scr, l_scr, acc_scr, *, tile, head_dim, lambda_init):
    h, qi = pl.program_id(0), pl.program_id(1)
    slope = slopes_ref[h]
    hd = head_dim

    def issue_scores(kj, slot):
        k_blk = k_ref[pl.ds(pl.multiple_of(kj * tile, tile), tile), :]
        for m in range(2):
            rows = slice(m * hd, (m + 1) * hd)
            s_scr[slot, m] = jnp.dot(k_blk[:, rows], qt_ref[rows, :], preferred_element_type=F32)

    issue_scores(qi, 1)
    issue_scores(0, 0)

    key = lax.broadcasted_iota(jnp.int32, (tile, tile), 0)
    qry = lax.broadcasted_iota(jnp.int32, (tile, tile), 1)
    kb_scr[...] = slope * key.astype(F32)
    bias = jnp.where((key // CHUNK) <= (qry // CHUNK), -slope * jnp.abs(key - qry).astype(F32), -jnp.inf)
    vt_diag = vt_ref[qi]
    for m in range(2):
        z = s_scr[1, m] + bias
        mx = jnp.max(z, axis=0, keepdims=True)
        p = jnp.exp2(z - mx)
        m_scr[m] = mx
        l_scr[m] = jnp.sum(p, axis=0, keepdims=True)
        acc_scr[m] = jnp.dot(vt_diag, p.astype(BF16), preferred_element_type=F32)

    q_pos = (lax.broadcasted_iota(jnp.int32, (1, tile), 1) + qi * tile).astype(F32)

    def kv_step(kj, slot, issue_next):
        if issue_next:
            issue_scores(jnp.minimum(kj + 1, qi - 1), 1 - slot)
        c = -slope * (q_pos - (kj * tile).astype(F32))
        vt_blk = vt_ref[kj]
        for m in range(2):
            z = s_scr[slot, m] + kb_scr[...]
            m_old = m_scr[m]
            m_new = jnp.maximum(m_old, jnp.max(z, axis=0, keepdims=True) + c)
            p = jnp.exp2(z - (m_new - c))
            alpha = jnp.exp2(m_old - m_new)
            l_scr[m] = alpha * l_scr[m] + jnp.sum(p, axis=0, keepdims=True)
            acc_scr[m] = alpha * acc_scr[m] + jnp.dot(vt_blk, p.astype(BF16), preferred_element_type=F32)
            m_scr[m] = m_new

    def two_steps(t, carry):
        kv_step(2 * t, 0, True)
        kv_step(2 * t + 1, 1, True)
        return carry

    lax.fori_loop(0, qi // 2, two_steps, 0)

    @pl.when(lax.rem(qi, 2) == 1)
    def _():
        kv_step(qi - 1, 0, False)

    lam = (jnp.exp(jnp.sum(lq1_ref[...] * lk1_ref[...], axis=-1, keepdims=True))
           - jnp.exp(jnp.sum(lq2_ref[...] * lk2_ref[...], axis=-1, keepdims=True)) + lambda_init)
    o_t = acc_scr[0] / l_scr[0] - lam * (acc_scr[1] / l_scr[1])
    inv = lax.rsqrt(jnp.mean(o_t * o_t, axis=0, keepdims=True) + RMS_EPS)
    o_ref[...] = ((o_t * inv * subln_ref[...]) * (1.0 - lambda_init)).T.astype(o_ref.dtype)


def _attn_call(qt, k, vt, slopes, lq1, lk1, lq2, lk2, subln, *, n_heads, head_dim, lambda_init):
    n_tiles, _, tile = qt.shape
    s_len = k.shape[0]
    hw = 2 * head_dim
    vec = lambda a: a.reshape(1, -1)
    vec_spec = lambda n: pl.BlockSpec((1, n), lambda h, qi: (0, 0))
    block_bytes = 2 * (2 * s_len * hw * 2) + 4 * tile * hw * 2 + 2 * tile * hw * 4 + 5 * tile * tile * 4 \
        + 6 * tile * tile * 4
    return pl.pallas_call(
        functools.partial(_attn_kernel, tile=tile, head_dim=head_dim, lambda_init=lambda_init),
        name="diff_attention",
        out_shape=jax.ShapeDtypeStruct((s_len, n_heads * hw), BF16),
        grid=(n_heads, n_tiles),
        in_specs=[
            pl.BlockSpec(memory_space=pltpu.SMEM),
            pl.BlockSpec((None, hw, tile), lambda h, qi: (qi, h, 0)),
            pl.BlockSpec((s_len, hw), lambda h, qi: (0, h)),
            pl.BlockSpec((n_tiles, hw, tile), lambda h, qi: (0, h, 0)),
            vec_spec(head_dim), vec_spec(head_dim), vec_spec(head_dim), vec_spec(head_dim),
            pl.BlockSpec((hw, 1), lambda h, qi: (0, 0)),
        ],
        out_specs=pl.BlockSpec((tile, hw), lambda h, qi: (qi, h)),
        scratch_shapes=[
            pltpu.VMEM((2, 2, tile, tile), F32),
            pltpu.VMEM((tile, tile), F32),
            pltpu.VMEM((2, 1, tile), F32),
            pltpu.VMEM((2, 1, tile), F32),
            pltpu.VMEM((2, hw, tile), F32),
        ],
        compiler_params=_params(("arbitrary", "arbitrary"), block_bytes),
    )(slopes, qt, k, vt, vec(lq1), vec(lk1), vec(lq2), vec(lk2), subln.reshape(-1, 1))


def _gates_kernel(ba_ref, alog_ref, dtb_ref, beta_ref, gc_ref):
    rows = ba_ref.shape[0]
    beta_ref[...] = jax.nn.sigmoid(ba_ref[...])
    g = -jnp.exp(alog_ref[...]) * jax.nn.softplus(ba_ref[...] + dtb_ref[...])
    r = lax.broadcasted_iota(jnp.int32, (rows, rows), 0)
    c = lax.broadcasted_iota(jnp.int32, (rows, rows), 1)
    tri = jnp.where(r // CHUNK == c // CHUNK, jnp.where(c <= r, 1.0, 0.0), 0.0).astype(F32)
    gc_ref[...] = jnp.dot(tri, g, preferred_element_type=F32, precision=lax.Precision.HIGHEST)


def _gates_call(ba, a_log_row, dt_bias_row):
    s_len = ba.shape[0]
    tm = min(GATE_ROWS, s_len)
    out = jax.ShapeDtypeStruct((s_len, V7X_LANES), F32)
    return pl.pallas_call(
        _gates_kernel,
        name="gdn_gates",
        out_shape=(out, out),
        grid=(s_len // tm,),
        in_specs=[
            pl.BlockSpec((tm, V7X_LANES), lambda i: (i, 0)),
            pl.BlockSpec((1, V7X_LANES), lambda i: (0, 0)),
            pl.BlockSpec((1, V7X_LANES), lambda i: (0, 0)),
        ],
        out_specs=(pl.BlockSpec((tm, V7X_LANES), lambda i: (i, 0)), pl.BlockSpec((tm, V7X_LANES), lambda i: (i, 0))),
        compiler_params=_params(("arbitrary",), 8 * tm * 2 * V7X_LANES * 4 + 4 * tm * tm * 4),
    )(ba, a_log_row, dt_bias_row)


def _split_bf16(x):
    hi = x.astype(BF16)
    return hi, (x - hi.astype(F32)).astype(BF16)


def _delta_kernel(q_ref, k_ref, v_ref, z_ref, beta_ref, gc_ref, gct_ref, ng_ref, o_ref, state_scr, *,
                  n_pairs, n_chunks, head_dim):
    hd = head_dim
    pairs = range(n_pairs)

    @pl.when(pl.program_id(1) == 0)
    def _():
        state_scr[...] = jnp.zeros(state_scr.shape, F32)

    ri = lax.broadcasted_iota(jnp.int32, (CHUNK, 2 * CHUNK), 0)
    lane = lax.broadcasted_iota(jnp.int32, (CHUNK, 2 * CHUNK), 1)
    left = lane < CHUNK
    ci = jnp.where(left, lane, lane - CHUNK)
    tril = ci <= ri
    strict = ci < ri
    eye = jnp.where(ci == ri, 1.0, 0.0).astype(F32)
    state_left = lax.broadcasted_iota(jnp.int32, (hd, 2 * hd), 1) < hd

    def block_diag(y, keep_left):
        zero = jnp.zeros_like(y)
        return jnp.concatenate([jnp.where(keep_left, y, zero), jnp.where(keep_left, zero, y)], axis=0)

    def pair_matmul(xh, xl, yh, yl):
        bh, bl = block_diag(yh, left), block_diag(yl, left)
        rhs = jnp.concatenate([jnp.concatenate([bh, bl], axis=1),
                               jnp.concatenate([bh, jnp.zeros_like(bl)], axis=1)], axis=0)
        out = jnp.dot(jnp.concatenate([xh, xl], axis=1), rhs, preferred_element_type=F32)
        return out[:, :2 * CHUNK] + out[:, 2 * CHUNK:]

    def chunk_step(c, carry):
        rows = pl.ds(pl.multiple_of(c * CHUNK, CHUNK), CHUNK)
        beta = beta_ref[rows, :]
        g_col = gc_ref[rows, :]
        g_rows = gct_ref[0, c]
        b_rows = gct_ref[1, c]
        lanes = lambda a, h: jnp.broadcast_to(a[:, h:h + 1], (CHUNK, hd))

        k = [k_ref[rows, p * hd:(p + 1) * hd] for p in pairs]
        q = [q_ref[rows, p * hd:(p + 1) * hd] for p in pairs]
        kq = [lax.dot_general(jnp.concatenate([k[p], q[p]], axis=0), jnp.concatenate([k[p], k[p]], axis=0), _NT,
                              preferred_element_type=F32) for p in pairs]
        g_row = [g_rows[p:p + 1, :] for p in pairs]
        lmat, qk_decay, q_dec, k_dec = [], [], [], []
        for p in pairs:
            g_a, g_b = lanes(g_col, 2 * p), lanes(g_col, 2 * p + 1)
            b_pk = jnp.where(left, lanes(beta, 2 * p), lanes(beta, 2 * p + 1))
            decay = jnp.exp(jnp.where(tril, jnp.where(left, g_a, g_b) - g_row[p], -jnp.inf))
            lmat.append(jnp.where(strict, kq[p][:CHUNK] * b_pk * decay, 0.0))
            qk_decay.append((kq[p][CHUNK:] * decay).astype(BF16))
            qf, kf = q[p].astype(F32), k[p].astype(F32)
            q_dec.append(jnp.concatenate([qf * jnp.exp(g_a), qf * jnp.exp(g_b)], axis=1).astype(BF16))
            ge_a, ge_b = g_row[p][:, CHUNK - 1:CHUNK], g_row[p][:, 2 * CHUNK - 1:2 * CHUNK]
            k_dec.append(((kf * jnp.exp(ge_a - g_a)).astype(BF16), (kf * jnp.exp(ge_b - g_b)).astype(BF16)))
        lsplit = [_split_bf16(lmat[p]) for p in pairs]
        x = [pair_matmul(*lsplit[p], *lsplit[p]) for p in pairs]
        acc = [eye - lmat[p] for p in pairs]
        power = 2
        while power * 2 < CHUNK:
            for p in pairs:
                sh, sl = _split_bf16(jnp.concatenate([acc[p], x[p]], axis=0))
                both = pair_matmul(sh, sl, sh[CHUNK:], sl[CHUNK:])
                acc[p] = acc[p] + both[:CHUNK]
                x[p] = both[CHUNK:]
            power *= 2
        tmat = [acc[p] + pair_matmul(*_split_bf16(acc[p]), *_split_bf16(x[p])) for p in pairs]
        zb = jnp.zeros((CHUNK, hd), BF16)
        uw = []
        for p in pairs:
            b_row = b_rows[p:p + 1, :]
            t_scaled = jnp.concatenate([tmat[p] * b_row, tmat[p] * (b_row * jnp.exp(g_row[p]))], axis=1)
            v_pk = v_ref[rows, 2 * p * hd:2 * (p + 1) * hd]
            rhs = jnp.concatenate([jnp.concatenate([v_pk[:, :hd], zb, zb, zb], axis=1),
                                   jnp.concatenate([zb, v_pk[:, hd:], zb, zb], axis=1),
                                   jnp.concatenate([zb, zb, k[p], zb], axis=1),
                                   jnp.concatenate([zb, zb, zb, k[p]], axis=1)], axis=0)
            uw.append(jnp.dot(t_scaled.astype(BF16), rhs, preferred_element_type=F32))
        s_old = [state_scr[p] for p in pairs]
        wq = []
        for p in pairs:
            wq.append(jnp.dot(jnp.concatenate([uw[p][:, 2 * hd:].astype(BF16), q_dec[p]], axis=0),
                              block_diag(s_old[p].astype(BF16), state_left), preferred_element_type=F32))
        o_pk = []
        for p in pairs:
            vn = (uw[p][:, :2 * hd] - wq[p][:CHUNK]).astype(BF16)
            vn_diag = jnp.concatenate([jnp.concatenate([vn[:, :hd], zb], axis=1),
                                       jnp.concatenate([zb, vn[:, hd:]], axis=1)], axis=0)
            o_pk.append(wq[p][CHUNK:] + jnp.dot(qk_decay[p], vn_diag, preferred_element_type=F32))
            ge_a, ge_b = g_row[p][:, CHUNK - 1:CHUNK], g_row[p][:, 2 * CHUNK - 1:2 * CHUNK]
            ds_a = lax.dot_general(k_dec[p][0], vn[:, :hd], _TN, preferred_element_type=F32)
            ds_b = lax.dot_general(k_dec[p][1], vn[:, hd:], _TN, preferred_element_type=F32)
            state_scr[p] = jnp.concatenate([s_old[p][:, :hd] * jnp.exp(ge_a) + ds_a,
                                            s_old[p][:, hd:] * jnp.exp(ge_b) + ds_b], axis=1)
        for p in pairs:
            wide_cols = slice(2 * p * hd, 2 * (p + 1) * hd)
            z_pk = z_ref[rows, wide_cols].astype(F32)
            gated = [_rms_rows(o_pk[p][:, h * hd:(h + 1) * hd], ng_ref[...]) * _silu(z_pk[:, h * hd:(h + 1) * hd])
                     for h in range(2)]
            o_ref[rows, wide_cols] = jnp.concatenate(gated, axis=1).astype(o_ref.dtype)
        return carry

    lax.fori_loop(0, n_chunks, chunk_step, 0)


def _delta_call(qkv, z, beta_g, gc_g, gct_g, norm_g, *, n_qk_heads, head_dim):
    s_len = qkv.shape[0]
    n_groups = n_qk_heads // DELTA_PAIRS
    heads = 2 * DELTA_PAIRS
    rows = DELTA_CHUNKS * CHUNK
    qk_w, v_w = DELTA_PAIRS * head_dim, heads * head_dim
    key_dim, value_dim = n_qk_heads * head_dim, 2 * n_qk_heads * head_dim
    k_off, v_off = key_dim // qk_w, 2 * key_dim // v_w
    block_bytes = 2 * rows * (2 * qk_w + 3 * v_w) * 2 + 6 * rows * V7X_LANES * 4 + heads * head_dim * head_dim * 4
    return pl.pallas_call(
        functools.partial(_delta_kernel, n_pairs=DELTA_PAIRS, n_chunks=DELTA_CHUNKS, head_dim=head_dim),
        name="gdn_delta_rule",
        out_shape=jax.ShapeDtypeStruct((s_len, value_dim), BF16),
        grid=(n_groups, s_len // rows),
        in_specs=[
            pl.BlockSpec((rows, qk_w), lambda g, b: (b, g)),
            pl.BlockSpec((rows, qk_w), lambda g, b: (b, k_off + g)),
            pl.BlockSpec((rows, v_w), lambda g, b: (b, v_off + g)),
            pl.BlockSpec((rows, v_w), lambda g, b: (b, g)),
            pl.BlockSpec((None, rows, heads), lambda g, b: (g, b, 0)),
            pl.BlockSpec((None, rows, heads), lambda g, b: (g, b, 0)),
            pl.BlockSpec((None, 2, DELTA_CHUNKS, DELTA_PAIRS, 2 * CHUNK), lambda g, b: (g, 0, b, 0, 0)),
            pl.BlockSpec((1, head_dim), lambda g, b: (0, 0)),
        ],
        out_specs=pl.BlockSpec((rows, v_w), lambda g, b: (b, g)),
        scratch_shapes=[pltpu.VMEM((DELTA_PAIRS, head_dim, 2 * head_dim), F32)],
        compiler_params=_params(("arbitrary", "arbitrary"), block_bytes),
    )(qkv, qkv, qkv, z, beta_g, gc_g, gct_g, norm_g.reshape(1, head_dim))


def _diff_attention_layer(x, gain, w_qkv, layer, q_norm, k_norm, lq1, lk1, lq2, lk2, subln, w_o, lambda_init):
    s_len = x.shape[0]
    head_dim = q_norm.shape[0]
    n_heads = w_o.shape[1] // subln.shape[0]
    qk_dim = 2 * n_heads * head_dim
    v_dim = w_qkv.shape[2] - 2 * qk_dim
    tile = min(ATTN_TILE, s_len)
    vec_spec = pl.BlockSpec((1, head_dim), lambda i, j: (0, 0))

    def proj(name, n_cols, first_col, gain_vec, scale, normalize, transpose):
        return _proj_call(
            name, functools.partial(_head_proj_kernel, head_dim=head_dim, scale=scale, normalize=normalize,
                                    transpose=transpose, tile=tile, w_transposed=False),
            x, gain, w_qkv, layer, n_cols, (gain_vec.reshape(1, -1),), (vec_spec,), BF16, (),
            first_col_tile=first_col // COL_TILE, transposed_tile=tile if transpose else None)

    qt = proj("attn_q_proj", qk_dim, 0, q_norm, head_dim ** -0.5 * LOG2_E, True, True)
    k = proj("attn_k_proj", qk_dim, qk_dim, k_norm, 1.0, True, False)
    vt = proj("attn_v_proj", v_dim, 2 * qk_dim, k_norm, 1.0, False, True)
    slopes = jnp.exp2(-8.0 * jnp.arange(1, n_heads + 1, dtype=F32) / n_heads) * LOG2_E
    o = _attn_call(qt, k, vt, slopes, lq1, lk1, lq2, lk2, subln, n_heads=n_heads, head_dim=head_dim,
                   lambda_init=lambda_init)
    return _mm_res_call(o, w_o, x, layer)


def _gated_deltanet_layer(x, gain, w_in, layer, conv_w, a_log, dt_bias, norm_g, w_out):
    s_len, d = x.shape
    head_dim = norm_g.shape[0]
    n_v_heads = a_log.shape[0]
    n_qk_heads = n_v_heads // 2
    key_dim, value_dim = n_qk_heads * head_dim, n_v_heads * head_dim
    conv_dim = 2 * key_dim + value_dim
    main_cols = conv_dim + value_dim
    n_conv_tiles = conv_dim // COL_TILE
    w_in_t = jnp.swapaxes(w_in, 1, 2)
    cw_spec = pl.BlockSpec((None, conv_w.shape[1], COL_TILE), lambda i, j: (layer, 0, j))
    qkv = _proj_call(
        "gdn_in_proj", functools.partial(_gdn_conv_kernel, n_q_tiles=key_dim // COL_TILE,
                                         n_qk_tiles=2 * key_dim // COL_TILE, head_dim=head_dim,
                                         q_scale=head_dim ** -0.5),
        x, gain, w_in_t, layer, conv_dim, (conv_w,), (cw_spec,), BF16, _conv_scratch(s_len, n_conv_tiles),
        w_transposed=True)
    z = _proj_call(
        "gdn_z_proj", functools.partial(_head_proj_kernel, head_dim=head_dim, scale=1.0, normalize=False,
                                        transpose=False, tile=None, w_transposed=True),
        x, gain, w_in_t, layer, value_dim, (norm_g.reshape(1, -1),),
        (pl.BlockSpec((1, head_dim), lambda i, j: (0, 0)),), BF16, (), first_col_tile=n_conv_tiles,
        w_transposed=True)

    assert main_cols % V7X_LANES == 0 and 2 * n_v_heads <= V7X_LANES
    ba = _proj_call("gdn_gate_proj", functools.partial(_plain_proj_kernel, w_transposed=True), x, gain, w_in_t, layer,
                    V7X_LANES, (), (), F32, (), col_tile=V7X_LANES, first_col_tile=main_cols // V7X_LANES,
                    w_transposed=True)
    on_a_lanes = lambda v: jnp.pad(v, (n_v_heads, V7X_LANES - 2 * n_v_heads)).reshape(1, -1)
    beta, gc = _gates_call(ba, on_a_lanes(a_log), on_a_lanes(dt_bias))
    beta, gc = beta[:, :n_v_heads], gc[:, n_v_heads:2 * n_v_heads]

    heads = 2 * DELTA_PAIRS
    n_groups = n_v_heads // heads
    beta_g = beta.reshape(s_len, n_groups, heads).transpose(1, 0, 2)
    gc_g = gc.reshape(s_len, n_groups, heads).transpose(1, 0, 2)
    rows_of = lambda a: a.reshape(s_len // CHUNK, CHUNK, n_groups, DELTA_PAIRS, 2).transpose(
        2, 0, 3, 4, 1).reshape(n_groups, s_len // CHUNK, DELTA_PAIRS, 2 * CHUNK)
    gct_g = jnp.stack([rows_of(gc), rows_of(beta)], axis=1)
    o = _delta_call(qkv, z, beta_g, gc_g, gct_g, norm_g, n_qk_heads=n_qk_heads, head_dim=head_dim)
    return _mm_res_call(o, w_out[layer].astype(BF16), x)


def _conv_ffn_layer(x, gain, w_up, conv_w, conv_b, w_down, layer):
    n_cols = w_up.shape[2]
    n_tiles = pl.cdiv(n_cols, COL_TILE)
    tile_spec = lambda rows: pl.BlockSpec((None, rows, COL_TILE), lambda i, j: (layer, 0, j))
    u = _proj_call("ffn_up_proj", _ffn_up_kernel, x, gain, w_up, layer, n_cols, (conv_w, conv_b[:, None, :]),
                   (tile_spec(conv_w.shape[1]), tile_spec(1)), BF16, _conv_scratch(x.shape[0], n_tiles))
    return _ffn_down_call(u, w_down.astype(BF16), layer, x)


def kernel(x, mixer_norm, ffn_norm, diff_w_qkv, diff_q_norm, diff_k_norm, diff_lambda_q1, diff_lambda_k1,
           diff_lambda_q2, diff_lambda_k2, diff_subln, diff_w_o, gdn_w_in, gdn_conv_w, gdn_A_log, gdn_dt_bias,
           gdn_norm, gdn_w_out, ffn_w_up, ffn_conv_w, ffn_conv_b, ffn_w_down):
    b_sz, s_len, d = x.shape
    depth = mixer_norm.shape[0]
    outs = []
    for b in range(b_sz):
        xb = x.reshape(s_len, d) if b_sz == 1 else x[b]
        for i in range(depth):
            j = i // 2
            if i % 2 == 0:
                lambda_init = 0.8 - 0.6 * math.exp(-0.3 * i)
                xb = _diff_attention_layer(xb, mixer_norm[i], diff_w_qkv, j, diff_q_norm[j], diff_k_norm[j],
                                           diff_lambda_q1[j], diff_lambda_k1[j], diff_lambda_q2[j],
                                           diff_lambda_k2[j], diff_subln[j], diff_w_o, lambda_init)
            else:
                xb = _gated_deltanet_layer(xb, mixer_norm[i], gdn_w_in, j, gdn_conv_w, gdn_A_log[j], gdn_dt_bias[j],
                                           gdn_norm[j], gdn_w_out)
            xb = _conv_ffn_layer(xb, ffn_norm[i], ffn_w_up, ffn_conv_w, ffn_conv_b, ffn_w_down, i)
        outs.append(xb)
    return outs[0].reshape(x.shape) if b_sz == 1 else jnp.stack(outs, axis=0)
```

```python
import functools
import math

import jax
import jax.numpy as jnp
from jax import lax
from jax.experimental import pallas as pl
from jax.experimental.pallas import tpu as pltpu

F32 = jnp.float32
BF16 = jnp.bfloat16

RMS_EPS = 1e-6
LOG2_E = math.log2(math.e)
BIAS_PARTS = 3
CHUNK = 64
V7X_VMEM_BYTES = 64 * 1024 * 1024
V7X_LANES = 128
V7X_SUBLANES = 8
COMPILER_TEMP_BYTES = 12 * 1024 * 1024

ROW_TILE = 1024
ROW_SUBTILE = 256
OUT_ROW_TILE = 512
DOWN_ROW_TILE = 256
COL_TILE = 512
ATTN_TILE = 512
DELTA_PAIRS = 16
DELTA_CHUNKS = 4
GATE_ROWS = 256

_NT = (((1,), (1,)), ((), ()))
_TN = (((0,), (0,)), ((), ()))


def _params(semantics, block_bytes):
    limit = min(int(block_bytes) + COMPILER_TEMP_BYTES, V7X_VMEM_BYTES - (4 << 20))
    return pltpu.CompilerParams(dimension_semantics=semantics, vmem_limit_bytes=limit)


def _rms_rows(x, gain):
    return x * lax.rsqrt(jnp.mean(x * x, axis=-1, keepdims=True) + RMS_EPS) * gain


def _silu(x):
    return x * jax.nn.sigmoid(x)


def _matmul(h, w_bf16, w_transposed):
    if w_transposed:
        return lax.dot_general(h, w_bf16, _NT, preferred_element_type=F32)
    return jnp.dot(h, w_bf16, preferred_element_type=F32)


def _normalize_once(x_ref, g_ref, h_scr):
    @pl.when(pl.program_id(1) == 0)
    def _():
        h_scr[...] = _rms_rows(x_ref[...], g_ref[...]).astype(BF16)


def _pipelined_rows(h_scr, w_bf16, w_transposed, piece_rows, epilogue):
    n_sub = h_scr.shape[0] // piece_rows
    piece = lambda r: _matmul(h_scr[r * piece_rows:(r + 1) * piece_rows, :], w_bf16, w_transposed)
    u_prev = piece(0)
    for r in range(1, n_sub):
        u_next = piece(r)
        epilogue(r - 1, u_prev)
        u_prev = u_next
    epilogue(n_sub - 1, u_prev)


def _head_proj_kernel(x_ref, g_ref, w_ref, gain_ref, o_ref, h_scr, *, head_dim, scale, normalize, transpose, tile,
                      w_transposed, piece_rows):
    _normalize_once(x_ref, g_ref, h_scr)

    def epilogue(r, acc):
        if normalize:
            acc = jnp.concatenate([_rms_rows(acc[:, c:c + head_dim], gain_ref[...]) * scale
                                   for c in range(0, acc.shape[1], head_dim)], axis=1)
        r0 = r * piece_rows
        if transpose:
            o_ref[r0 // tile, :, r0 % tile:r0 % tile + piece_rows] = acc.T.astype(o_ref.dtype)
        else:
            o_ref[r0:r0 + piece_rows, :] = acc.astype(o_ref.dtype)

    _pipelined_rows(h_scr, w_ref[...].astype(BF16), w_transposed, piece_rows, epilogue)


def _plain_proj_kernel(x_ref, g_ref, w_ref, o_ref, h_scr, *, w_transposed):
    _normalize_once(x_ref, g_ref, h_scr)
    o_ref[...] = _matmul(h_scr[...], w_ref[...].astype(BF16), w_transposed).astype(o_ref.dtype)


def _conv_proj_steps(h_scr, w_ref, cw_ref, o_ref, carry_scr, u_scr, *, piece_rows, piece_cols, w_transposed, finish):
    i, j = pl.program_id(0), pl.program_id(1)
    rows, kw = h_scr.shape[0], cw_ref.shape[0]
    head = V7X_SUBLANES

    @pl.when(jnp.logical_and(i == 0, j == 0))
    def _():
        carry_scr[...] = jnp.zeros(carry_scr.shape, F32)

    u_scr[0:head, :] = carry_scr[j]
    w_bf16 = w_ref[...].astype(BF16)
    pieces = [(r0, c0) for r0 in range(0, rows, piece_rows) for c0 in range(0, o_ref.shape[1], piece_cols)]

    def matmul_piece(r0, c0):
        cols = slice(c0, c0 + piece_cols)
        w_piece = w_bf16[cols, :] if w_transposed else w_bf16[:, cols]
        u_scr[head + r0:head + r0 + piece_rows, cols] = _matmul(h_scr[r0:r0 + piece_rows, :], w_piece, w_transposed)

    def epilogue_piece(p0, c0):
        cols = slice(c0, c0 + piece_cols)
        for r0 in range(p0, p0 + piece_rows, ROW_SUBTILE):
            y = u_scr[head + r0:head + r0 + ROW_SUBTILE, cols] * cw_ref[kw - 1:kw, cols]
            for s in range(1, kw):
                y = y + u_scr[head + r0 - s:head + r0 - s + ROW_SUBTILE, cols] * cw_ref[kw - 1 - s:kw - s, cols]
            o_ref[r0:r0 + ROW_SUBTILE, cols] = finish(y, j, cols).astype(o_ref.dtype)

    matmul_piece(*pieces[0])
    for prev, cur in zip(pieces[:-1], pieces[1:]):
        matmul_piece(*cur)
        epilogue_piece(*prev)
    epilogue_piece(*pieces[-1])
    carry_scr[j] = u_scr[rows:rows + head, :]


def _ffn_up_kernel(x_ref, g_ref, w_ref, cw_ref, cb_ref, o_ref, h_scr, carry_scr, u_scr):
    _normalize_once(x_ref, g_ref, h_scr)
    _conv_proj_steps(h_scr, w_ref, cw_ref, o_ref, carry_scr, u_scr, piece_rows=h_scr.shape[0],
                     piece_cols=o_ref.shape[1], w_transposed=False,
                     finish=lambda y, jj, cols: y + cb_ref[:, cols])


def _gdn_conv_kernel(x_ref, g_ref, w_ref, cw_ref, o_ref, h_scr, carry_scr, u_scr, *, n_q_tiles, n_qk_tiles, head_dim,
                     q_scale):
    _normalize_once(x_ref, g_ref, h_scr)

    def finish(y, jj, cols):
        y = _silu(y)
        l2_scale = jnp.where(jj < n_q_tiles, q_scale, 1.0)
        heads = []
        for c in range(0, y.shape[1], head_dim):
            blk = y[:, c:c + head_dim]
            inv = lax.rsqrt(jnp.sum(blk * blk, axis=-1, keepdims=True) + RMS_EPS) * l2_scale
            heads.append(blk * jnp.where(jj < n_qk_tiles, inv, 1.0))
        return jnp.concatenate(heads, axis=1)

    _conv_proj_steps(h_scr, w_ref, cw_ref, o_ref, carry_scr, u_scr, piece_rows=ROW_SUBTILE,
                     piece_cols=o_ref.shape[1], w_transposed=True, finish=finish)


def _proj_call(name, kernel, x, gain, w3, layer, n_cols, extra_inputs, extra_specs, out_dtype, scratch,
               col_tile=COL_TILE, first_col_tile=0, transposed_tile=None, w_transposed=False):
    s_len, d = x.shape
    tm, tn = min(ROW_TILE, s_len), col_tile
    grid = (s_len // tm, pl.cdiv(n_cols, tn))
    if w_transposed:
        w_spec = pl.BlockSpec((None, tn, d), lambda i, j: (layer, first_col_tile + j, 0))
    else:
        w_spec = pl.BlockSpec((None, d, tn), lambda i, j: (layer, 0, first_col_tile + j))
    in_specs = [pl.BlockSpec((tm, d), lambda i, j: (i, 0)), pl.BlockSpec((1, d), lambda i, j: (0, 0)), w_spec]
    in_specs += list(extra_specs)
    if transposed_tile is None:
        out_shape = jax.ShapeDtypeStruct((s_len, n_cols), out_dtype)
        out_spec = pl.BlockSpec((tm, tn), lambda i, j: (i, j))
    else:
        out_shape = jax.ShapeDtypeStruct((s_len // transposed_tile, n_cols, transposed_tile), out_dtype)
        out_spec = pl.BlockSpec((tm // transposed_tile, tn, transposed_tile), lambda i, j: (i, j, 0))
    out_bytes = jnp.dtype(out_dtype).itemsize
    block_bytes = 2 * tm * d * 4 + tm * d * 2 + 2 * d * tn * 4 + d * tn * 2 + 2 * tm * tn * out_bytes + 3 * tm * tn * 4
    return pl.pallas_call(
        kernel,
        name=name,
        out_shape=out_shape,
        grid=grid,
        in_specs=in_specs,
        out_specs=out_spec,
        scratch_shapes=[pltpu.VMEM((tm, d), BF16)] + list(scratch),
        compiler_params=_params(("arbitrary", "arbitrary"), block_bytes),
    )(x, gain.reshape(1, d), w3, *extra_inputs)


def _conv_scratch(s_len, n_tiles):
    tm = min(ROW_TILE, s_len)
    return (pltpu.VMEM((n_tiles, V7X_SUBLANES, COL_TILE), F32), pltpu.VMEM((V7X_SUBLANES + tm, COL_TILE), F32))


def _mm_res_kernel(a_ref, w_ref, r_ref, o_ref, *w_scr):
    if w_scr:
        @pl.when(pl.program_id(0) == 0)
        def _():
            for r0 in range(0, w_ref.shape[0], COL_TILE):
                w_scr[0][r0:r0 + COL_TILE, :] = w_ref[r0:r0 + COL_TILE, :].astype(BF16)
        w = w_scr[0][...]
    else:
        w = w_ref[...]
    o_ref[...] = r_ref[...] + jnp.dot(a_ref[...], w, preferred_element_type=F32)


def _mm_res_call(a, w, res, layer=None):
    s_len, k = a.shape
    n = w.shape[-1]
    tm = min(OUT_ROW_TILE, s_len)
    once = pl.Buffered(1)
    if w.ndim == 3:
        w_spec = pl.BlockSpec((None, k, n), lambda i: (layer, 0, 0), pipeline_mode=once)
        scratch = [pltpu.VMEM((k, n), BF16)]
    else:
        w_spec = pl.BlockSpec((k, n), lambda i: (0, 0), pipeline_mode=once)
        scratch = []
    block_bytes = 2 * tm * k * 2 + k * n * w.dtype.itemsize + len(scratch) * k * n * 2 + 5 * tm * n * 4
    return pl.pallas_call(
        _mm_res_kernel,
        name="out_proj_residual",
        out_shape=jax.ShapeDtypeStruct((s_len, n), F32),
        grid=(s_len // tm,),
        in_specs=[pl.BlockSpec((tm, k), lambda i: (i, 0)), w_spec, pl.BlockSpec((tm, n), lambda i: (i, 0))],
        out_specs=pl.BlockSpec((tm, n), lambda i: (i, 0)),
        scratch_shapes=scratch,
        compiler_params=_params(("arbitrary",), block_bytes),
    )(a, w, res)


def _ffn_down_kernel(uv_ref, ug_ref, w_ref, r_ref, o_ref, g_scr):
    d_ff = w_ref.shape[0]
    mid = pl.cdiv(d_ff // 2, 2 * V7X_LANES) * 2 * V7X_LANES
    acc = r_ref[...]
    for k0, k1 in ((0, mid), (mid, d_ff)):
        for c0 in range(k0, k1, COL_TILE):
            cols = slice(c0, min(c0 + COL_TILE, k1))
            g_scr[:, cols] = (_silu(ug_ref[:, cols].astype(F32)) * uv_ref[:, cols].astype(F32)).astype(BF16)
        acc = acc + jnp.dot(g_scr[:, k0:k1], w_ref[k0:k1, :], preferred_element_type=F32)
    o_ref[...] = acc


def _ffn_down_call(u, w_down, layer, res):
    s_len = u.shape[0]
    _, d_ff, n = w_down.shape
    tm = min(DOWN_ROW_TILE, s_len)
    block_bytes = 4 * tm * d_ff * 2 + tm * d_ff * 2 + d_ff * n * 2 + 6 * tm * n * 4
    return pl.pallas_call(
        _ffn_down_kernel,
        name="ffn_down_residual",
        out_shape=jax.ShapeDtypeStruct((s_len, n), F32),
        grid=(s_len // tm,),
        in_specs=[
            pl.BlockSpec((tm, d_ff), lambda i: (i, 0)),
            pl.BlockSpec((tm, d_ff), lambda i: (i, 1)),
            pl.BlockSpec((None, d_ff, n), lambda i: (layer, 0, 0), pipeline_mode=pl.Buffered(1)),
            pl.BlockSpec((tm, n), lambda i: (i, 0)),
        ],
        out_specs=pl.BlockSpec((tm, n), lambda i: (i, 0)),
        scratch_shapes=[pltpu.VMEM((tm, d_ff), BF16)],
        compiler_params=_params(("arbitrary",), block_bytes),
    )(u, u, w_down, res)


def _attn_kernel(slopes_ref, qt_ref, k_ref, vt_ref, lq1_ref, lk1_ref, lq2_ref, lk2_ref, subln_ref, o_ref,
                 s_scr, kb_scr, m_scr, l_scr, acc_scr, *, tile, head_dim, lambda_init):
    h, qi = pl.program_id(0), pl.program_id(1)
    slope = slopes_ref[h]
    hd = head_dim

    key_bias = slope * lax.broadcasted_iota(jnp.int32, (tile, hd), 0).astype(F32)
    lane = lax.broadcasted_iota(jnp.int32, (tile, hd), 1)
    part, parts = key_bias, []
    for _ in range(BIAS_PARTS):
        parts.append(part.astype(BF16).astype(F32))
        part = part - parts[-1]
    kb_cols = jnp.zeros((tile, hd), F32)
    for c in reversed(range(BIAS_PARTS)):
        kb_cols = jnp.where(lane == c, parts[c], kb_cols)
    kb_scr[...] = kb_cols.astype(BF16)
    ones_rows = jnp.where(lax.broadcasted_iota(jnp.int32, (hd, tile), 0) < BIAS_PARTS, 1.0, 0.0).astype(BF16)

    def issue_scores(kj, slot):
        k_blk = k_ref[pl.ds(pl.multiple_of(kj * tile, tile), tile), :]
        for m in range(2):
            rows = slice(m * hd, (m + 1) * hd)
            s_scr[slot, m] = jnp.dot(jnp.concatenate([k_blk[:, rows], kb_scr[...]], axis=1),
                                     jnp.concatenate([qt_ref[rows, :], ones_rows], axis=0),
                                     preferred_element_type=F32)

    issue_scores(qi, 1)
    issue_scores(0, 0)

    key = lax.broadcasted_iota(jnp.int32, (tile, tile), 0)
    qry = lax.broadcasted_iota(jnp.int32, (tile, tile), 1)
    bias = jnp.where((key // CHUNK) <= (qry // CHUNK), -slope * (jnp.abs(key - qry) + key).astype(F32), -jnp.inf)
    vt_diag = vt_ref[qi]
    for m in range(2):
        z = s_scr[1, m] + bias
        mx = jnp.max(z, axis=0, keepdims=True)
        p = jnp.exp2(z - mx)
        m_scr[m] = mx
        l_scr[m] = jnp.sum(p, axis=0, keepdims=True)
        acc_scr[m] = jnp.dot(vt_diag, p.astype(BF16), preferred_element_type=F32)

    q_pos = (lax.broadcasted_iota(jnp.int32, (1, tile), 1) + qi * tile).astype(F32)

    def kv_step(kj, slot, issue_next):
        if issue_next:
            issue_scores(jnp.minimum(kj + 1, qi - 1), 1 - slot)
        c = -slope * (q_pos - (kj * tile).astype(F32))
        vt_blk = vt_ref[kj]
        for m in range(2):
            z = s_scr[slot, m]
            m_old = m_scr[m]
            m_new = jnp.maximum(m_old, jnp.max(z, axis=0, keepdims=True) + c)
            p = jnp.exp2(z - (m_new - c))
            alpha = jnp.exp2(m_old - m_new)
            l_scr[m] = alpha * l_scr[m] + jnp.sum(p, axis=0, keepdims=True)
            acc_scr[m] = alpha * acc_scr[m] + jnp.dot(vt_blk, p.astype(BF16), preferred_element_type=F32)
            m_scr[m] = m_new

    def two_steps(t, carry):
        kv_step(2 * t, 0, True)
        kv_step(2 * t + 1, 1, True)
        return carry

    lax.fori_loop(0, qi // 2, two_steps, 0)

    @pl.when(lax.rem(qi, 2) == 1)
    def _():
        kv_step(qi - 1, 0, False)

    lam = (jnp.exp(jnp.sum(lq1_ref[...] * lk1_ref[...], axis=-1, keepdims=True))
           - jnp.exp(jnp.sum(lq2_ref[...] * lk2_ref[...], axis=-1, keepdims=True)) + lambda_init)
    o_t = acc_scr[0] / l_scr[0] - lam * (acc_scr[1] / l_scr[1])
    inv = lax.rsqrt(jnp.mean(o_t * o_t, axis=0, keepdims=True) + RMS_EPS)
    o_ref[...] = ((o_t * inv * subln_ref[...]) * (1.0 - lambda_init)).T.astype(o_ref.dtype)


def _attn_call(qt, k, vt, slopes, lq1, lk1, lq2, lk2, subln, *, n_heads, head_dim, lambda_init):
    n_tiles, _, tile = qt.shape
    s_len = k.shape[0]
    hw = 2 * head_dim
    vec = lambda a: a.reshape(1, -1)
    vec_spec = lambda n: pl.BlockSpec((1, n), lambda h, qi: (0, 0))
    block_bytes = 2 * (2 * s_len * hw * 2) + 4 * tile * hw * 2 + 2 * tile * hw * 4 + 5 * tile * tile * 4 \
        + 6 * tile * tile * 4
    return pl.pallas_call(
        functools.partial(_attn_kernel, tile=tile, head_dim=head_dim, lambda_init=lambda_init),
        name="diff_attention",
        out_shape=jax.ShapeDtypeStruct((s_len, n_heads * hw), BF16),
        grid=(n_heads, n_tiles),
        in_specs=[
            pl.BlockSpec(memory_space=pltpu.SMEM),
            pl.BlockSpec((None, hw, tile), lambda h, qi: (qi, h, 0)),
            pl.BlockSpec((s_len, hw), lambda h, qi: (0, h)),
            pl.BlockSpec((n_tiles, hw, tile), lambda h, qi: (0, h, 0)),
            vec_spec(head_dim), vec_spec(head_dim), vec_spec(head_dim), vec_spec(head_dim),
            pl.BlockSpec((hw, 1), lambda h, qi: (0, 0)),
        ],
        out_specs=pl.BlockSpec((tile, hw), lambda h, qi: (qi, h)),
        scratch_shapes=[
            pltpu.VMEM((2, 2, tile, tile), F32),
            pltpu.VMEM((tile, head_dim), BF16),
            pltpu.VMEM((2, 1, tile), F32),
            pltpu.VMEM((2, 1, tile), F32),
            pltpu.VMEM((2, hw, tile), F32),
        ],
        compiler_params=_params(("arbitrary", "arbitrary"), block_bytes),
    )(slopes, qt, k, vt, vec(lq1), vec(lk1), vec(lq2), vec(lk2), subln.reshape(-1, 1))


def _gates_kernel(ba_ref, alog_ref, dtb_ref, beta_ref, gc_ref):
    rows = ba_ref.shape[0]
    beta_ref[...] = jax.nn.sigmoid(ba_ref[...])
    g = -jnp.exp(alog_ref[...]) * jax.nn.softplus(ba_ref[...] + dtb_ref[...])
    r = lax.broadcasted_iota(jnp.int32, (rows, rows), 0)
    c = lax.broadcasted_iota(jnp.int32, (rows, rows), 1)
    tri = jnp.where(r // CHUNK == c // CHUNK, jnp.where(c <= r, 1.0, 0.0), 0.0).astype(F32)
    gc_ref[...] = jnp.dot(tri, g, preferred_element_type=F32, precision=lax.Precision.HIGHEST)


def _gates_call(ba, a_log_row, dt_bias_row):
    s_len = ba.shape[0]
    tm = min(GATE_ROWS, s_len)
    out = jax.ShapeDtypeStruct((s_len, V7X_LANES), F32)
    return pl.pallas_call(
        _gates_kernel,
        name="gdn_gates",
        out_shape=(out, out),
        grid=(s_len // tm,),
        in_specs=[
            pl.BlockSpec((tm, V7X_LANES), lambda i: (i, 0)),
            pl.BlockSpec((1, V7X_LANES), lambda i: (0, 0)),
            pl.BlockSpec((1, V7X_LANES), lambda i: (0, 0)),
        ],
        out_specs=(pl.BlockSpec((tm, V7X_LANES), lambda i: (i, 0)), pl.BlockSpec((tm, V7X_LANES), lambda i: (i, 0))),
        compiler_params=_params(("arbitrary",), 8 * tm * 2 * V7X_LANES * 4 + 4 * tm * tm * 4),
    )(ba, a_log_row, dt_bias_row)


def _split_bf16(x):
    hi = x.astype(BF16)
    return hi, (x - hi.astype(F32)).astype(BF16)


def _delta_kernel(q_ref, k_ref, v_ref, z_ref, beta_ref, gc_ref, gct_ref, ng_ref, o_ref, state_scr, *,
                  n_pairs, n_chunks, head_dim):
    hd = head_dim
    pairs = range(n_pairs)

    @pl.when(pl.program_id(1) == 0)
    def _():
        state_scr[...] = jnp.zeros(state_scr.shape, F32)

    ri = lax.broadcasted_iota(jnp.int32, (CHUNK, 2 * CHUNK), 0)
    lane = lax.broadcasted_iota(jnp.int32, (CHUNK, 2 * CHUNK), 1)
    left = lane < CHUNK
    ci = jnp.where(left, lane, lane - CHUNK)
    tril = ci <= ri
    strict = ci < ri
    eye = jnp.where(ci == ri, 1.0, 0.0).astype(F32)
    state_left = lax.broadcasted_iota(jnp.int32, (hd, 2 * hd), 1) < hd

    def block_diag(y, keep_left):
        zero = jnp.zeros_like(y)
        return jnp.concatenate([jnp.where(keep_left, y, zero), jnp.where(keep_left, zero, y)], axis=0)

    def pair_matmul(xh, xl, yh, yl):
        bh, bl = block_diag(yh, left), block_diag(yl, left)
        rhs = jnp.concatenate([jnp.concatenate([bh, bl], axis=1),
                               jnp.concatenate([bh, jnp.zeros_like(bl)], axis=1)], axis=0)
        out = jnp.dot(jnp.concatenate([xh, xl], axis=1), rhs, preferred_element_type=F32)
        return out[:, :2 * CHUNK] + out[:, 2 * CHUNK:]

    def chunk_step(c, carry):
        rows = pl.ds(pl.multiple_of(c * CHUNK, CHUNK), CHUNK)
        beta = beta_ref[rows, :]
        g_col = gc_ref[rows, :]
        g_rows = gct_ref[0, c]
        b_rows = gct_ref[1, c]
        lanes = lambda a, h: jnp.broadcast_to(a[:, h:h + 1], (CHUNK, hd))

        k = [k_ref[rows, p * hd:(p + 1) * hd] for p in pairs]
        q = [q_ref[rows, p * hd:(p + 1) * hd] for p in pairs]
        kq = [lax.dot_general(jnp.concatenate([k[p], q[p]], axis=0), jnp.concatenate([k[p], k[p]], axis=0), _NT,
                              preferred_element_type=F32) for p in pairs]
        g_row = [g_rows[p:p + 1, :] for p in pairs]
        lmat, qk_decay, q_dec, k_dec = [], [], [], []
        for p in pairs:
            g_a, g_b = lanes(g_col, 2 * p), lanes(g_col, 2 * p + 1)
            b_pk = jnp.where(left, lanes(beta, 2 * p), lanes(beta, 2 * p + 1))
            decay = jnp.exp(jnp.where(tril, jnp.where(left, g_a, g_b) - g_row[p], -jnp.inf))
            lmat.append(jnp.where(strict, kq[p][:CHUNK] * b_pk * decay, 0.0))
            qk_decay.append((kq[p][CHUNK:] * decay).astype(BF16))
            qf, kf = q[p].astype(F32), k[p].astype(F32)
            q_dec.append(jnp.concatenate([qf * jnp.exp(g_a), qf * jnp.exp(g_b)], axis=1).astype(BF16))
            ge_a, ge_b = g_row[p][:, CHUNK - 1:CHUNK], g_row[p][:, 2 * CHUNK - 1:2 * CHUNK]
            k_dec.append(((kf * jnp.exp(ge_a - g_a)).astype(BF16), (kf * jnp.exp(ge_b - g_b)).astype(BF16)))
        lsplit = [_split_bf16(lmat[p]) for p in pairs]
        x = [pair_matmul(*lsplit[p], *lsplit[p]) for p in pairs]
        acc = [eye - lmat[p] for p in pairs]
        power = 2
        while power * 2 < CHUNK:
            for p in pairs:
                sh, sl = _split_bf16(jnp.concatenate([acc[p], x[p]], axis=0))
                both = pair_matmul(sh, sl, sh[CHUNK:], sl[CHUNK:])
                acc[p] = acc[p] + both[:CHUNK]
                x[p] = both[CHUNK:]
            power *= 2
        tmat = [acc[p] + pair_matmul(*_split_bf16(acc[p]), *_split_bf16(x[p])) for p in pairs]
        zb = jnp.zeros((CHUNK, hd), BF16)
        uw = []
        for p in pairs:
            b_row = b_rows[p:p + 1, :]
            t_scaled = jnp.concatenate([tmat[p] * b_row, tmat[p] * (b_row * jnp.exp(g_row[p]))], axis=1)
            v_pk = v_ref[rows, 2 * p * hd:2 * (p + 1) * hd]
            rhs = jnp.concatenate([jnp.concatenate([v_pk[:, :hd], zb, zb, zb], axis=1),
                                   jnp.concatenate([zb, v_pk[:, hd:], zb, zb], axis=1),
                                   jnp.concatenate([zb, zb, k[p], zb], axis=1),
                                   jnp.concatenate([zb, zb, zb, k[p]], axis=1)], axis=0)
            uw.append(jnp.dot(t_scaled.astype(BF16), rhs, preferred_element_type=F32))
        s_old = [state_scr[p] for p in pairs]
        wq = []
        for p in pairs:
            wq.append(jnp.dot(jnp.concatenate([uw[p][:, 2 * hd:].astype(BF16), q_dec[p]], axis=0),
                              block_diag(s_old[p].astype(BF16), state_left), preferred_element_type=F32))
        o_pk = []
        for p in pairs:
            vn = (uw[p][:, :2 * hd] - wq[p][:CHUNK]).astype(BF16)
            vn_diag = jnp.concatenate([jnp.concatenate([vn[:, :hd], zb], axis=1),
                                       jnp.concatenate([zb, vn[:, hd:]], axis=1)], axis=0)
            o_pk.append(wq[p][CHUNK:] + jnp.dot(qk_decay[p], vn_diag, preferred_element_type=F32))
            ge_a, ge_b = g_row[p][:, CHUNK - 1:CHUNK], g_row[p][:, 2 * CHUNK - 1:2 * CHUNK]
            ds_a = lax.dot_general(k_dec[p][0], vn[:, :hd], _TN, preferred_element_type=F32)
            ds_b = lax.dot_general(k_dec[p][1], vn[:, hd:], _TN, preferred_element_type=F32)
            state_scr[p] = jnp.concatenate([s_old[p][:, :hd] * jnp.exp(ge_a) + ds_a,
                                            s_old[p][:, hd:] * jnp.exp(ge_b) + ds_b], axis=1)
        for p in pairs:
            wide_cols = slice(2 * p * hd, 2 * (p + 1) * hd)
            z_pk = z_ref[rows, wide_cols].astype(F32)
            gated = [_rms_rows(o_pk[p][:, h * hd:(h + 1) * hd], ng_ref[...]) * _silu(z_pk[:, h * hd:(h + 1) * hd])
                     for h in range(2)]
            o_ref[rows, wide_cols] = jnp.concatenate(gated, axis=1).astype(o_ref.dtype)
        return carry

    lax.fori_loop(0, n_chunks, chunk_step, 0)


def _delta_call(qkv, z, beta_g, gc_g, gct_g, norm_g, *, n_qk_heads, head_dim):
    s_len = qkv.shape[0]
    n_groups = n_qk_heads // DELTA_PAIRS
    heads = 2 * DELTA_PAIRS
    rows = DELTA_CHUNKS * CHUNK
    qk_w, v_w = DELTA_PAIRS * head_dim, heads * head_dim
    key_dim, value_dim = n_qk_heads * head_dim, 2 * n_qk_heads * head_dim
    k_off, v_off = key_dim // qk_w, 2 * key_dim // v_w
    block_bytes = 2 * rows * (2 * qk_w + 3 * v_w) * 2 + 6 * rows * V7X_LANES * 4 + heads * head_dim * head_dim * 4
    return pl.pallas_call(
        functools.partial(_delta_kernel, n_pairs=DELTA_PAIRS, n_chunks=DELTA_CHUNKS, head_dim=head_dim),
        name="gdn_delta_rule",
        out_shape=jax.ShapeDtypeStruct((s_len, value_dim), BF16),
        grid=(n_groups, s_len // rows),
        in_specs=[
            pl.BlockSpec((rows, qk_w), lambda g, b: (b, g)),
            pl.BlockSpec((rows, qk_w), lambda g, b: (b, k_off + g)),
            pl.BlockSpec((rows, v_w), lambda g, b: (b, v_off + g)),
            pl.BlockSpec((rows, v_w), lambda g, b: (b, g)),
            pl.BlockSpec((None, rows, heads), lambda g, b: (g, b, 0)),
            pl.BlockSpec((None, rows, heads), lambda g, b: (g, b, 0)),
            pl.BlockSpec((None, 2, DELTA_CHUNKS, DELTA_PAIRS, 2 * CHUNK), lambda g, b: (g, 0, b, 0, 0)),
            pl.BlockSpec((1, head_dim), lambda g, b: (0, 0)),
        ],
        out_specs=pl.BlockSpec((rows, v_w), lambda g, b: (b, g)),
        scratch_shapes=[pltpu.VMEM((DELTA_PAIRS, head_dim, 2 * head_dim), F32)],
        compiler_params=_params(("arbitrary", "arbitrary"), block_bytes),
    )(qkv, qkv, qkv, z, beta_g, gc_g, gct_g, norm_g.reshape(1, head_dim))


def _diff_attention_layer(x, gain, w_qkv, layer, q_norm, k_norm, lq1, lk1, lq2, lk2, subln, w_o, lambda_init):
    s_len = x.shape[0]
    head_dim = q_norm.shape[0]
    n_heads = w_o.shape[1] // subln.shape[0]
    qk_dim = 2 * n_heads * head_dim
    v_dim = w_qkv.shape[2] - 2 * qk_dim
    tile = min(ATTN_TILE, s_len)
    vec_spec = pl.BlockSpec((1, head_dim), lambda i, j: (0, 0))

    def proj(name, n_cols, first_col, gain_vec, scale, normalize, transpose):
        return _proj_call(
            name, functools.partial(_head_proj_kernel, head_dim=head_dim, scale=scale, normalize=normalize,
                                    transpose=transpose, tile=tile, w_transposed=False,
                                    piece_rows=ROW_SUBTILE),
            x, gain, w_qkv, layer, n_cols, (gain_vec.reshape(1, -1),), (vec_spec,), BF16, (),
            first_col_tile=first_col // COL_TILE, transposed_tile=tile if transpose else None)

    qt = proj("attn_q_proj", qk_dim, 0, q_norm, head_dim ** -0.5 * LOG2_E, True, True)
    k = proj("attn_k_proj", qk_dim, qk_dim, k_norm, 1.0, True, False)
    vt = proj("attn_v_proj", v_dim, 2 * qk_dim, k_norm, 1.0, False, True)
    slopes = jnp.exp2(-8.0 * jnp.arange(1, n_heads + 1, dtype=F32) / n_heads) * LOG2_E
    o = _attn_call(qt, k, vt, slopes, lq1, lk1, lq2, lk2, subln, n_heads=n_heads, head_dim=head_dim,
                   lambda_init=lambda_init)
    return _mm_res_call(o, w_o, x, layer)


def _gated_deltanet_layer(x, gain, w_in, layer, conv_w, a_log, dt_bias, norm_g, w_out):
    s_len, d = x.shape
    head_dim = norm_g.shape[0]
    n_v_heads = a_log.shape[0]
    n_qk_heads = n_v_heads // 2
    key_dim, value_dim = n_qk_heads * head_dim, n_v_heads * head_dim
    conv_dim = 2 * key_dim + value_dim
    main_cols = conv_dim + value_dim
    n_conv_tiles = conv_dim // COL_TILE
    w_in_t = jnp.swapaxes(w_in, 1, 2)
    cw_spec = pl.BlockSpec((None, conv_w.shape[1], COL_TILE), lambda i, j: (layer, 0, j))
    qkv = _proj_call(
        "gdn_in_proj", functools.partial(_gdn_conv_kernel, n_q_tiles=key_dim // COL_TILE,
                                         n_qk_tiles=2 * key_dim // COL_TILE, head_dim=head_dim,
                                         q_scale=head_dim ** -0.5),
        x, gain, w_in_t, layer, conv_dim, (conv_w,), (cw_spec,), BF16, _conv_scratch(s_len, n_conv_tiles),
        w_transposed=True)
    z = _proj_call(
        "gdn_z_proj", functools.partial(_head_proj_kernel, head_dim=head_dim, scale=1.0, normalize=False,
                                        transpose=False, tile=None, w_transposed=True,
                                        piece_rows=min(ROW_TILE, s_len)),
        x, gain, w_in_t, layer, value_dim, (norm_g.reshape(1, -1),),
        (pl.BlockSpec((1, head_dim), lambda i, j: (0, 0)),), BF16, (), first_col_tile=n_conv_tiles,
        w_transposed=True)

    assert main_cols % V7X_LANES == 0 and 2 * n_v_heads <= V7X_LANES
    ba = _proj_call("gdn_gate_proj", functools.partial(_plain_proj_kernel, w_transposed=True), x, gain, w_in_t, layer,
                    V7X_LANES, (), (), F32, (), col_tile=V7X_LANES, first_col_tile=main_cols // V7X_LANES,
                    w_transposed=True)
    on_a_lanes = lambda v: jnp.pad(v, (n_v_heads, V7X_LANES - 2 * n_v_heads)).reshape(1, -1)
    beta, gc = _gates_call(ba, on_a_lanes(a_log), on_a_lanes(dt_bias))
    beta, gc = beta[:, :n_v_heads], gc[:, n_v_heads:2 * n_v_heads]

    heads = 2 * DELTA_PAIRS
    n_groups = n_v_heads // heads
    beta_g = beta.reshape(s_len, n_groups, heads).transpose(1, 0, 2)
    gc_g = gc.reshape(s_len, n_groups, heads).transpose(1, 0, 2)
    rows_of = lambda a: a.reshape(s_len // CHUNK, CHUNK, n_groups, DELTA_PAIRS, 2).transpose(
        2, 0, 3, 4, 1).reshape(n_groups, s_len // CHUNK, DELTA_PAIRS, 2 * CHUNK)
    gct_g = jnp.stack([rows_of(gc), rows_of(beta)], axis=1)
    o = _delta_call(qkv, z, beta_g, gc_g, gct_g, norm_g, n_qk_heads=n_qk_heads, head_dim=head_dim)
    return _mm_res_call(o, w_out[layer].astype(BF16), x)


def _conv_ffn_layer(x, gain, w_up, conv_w, conv_b, w_down, layer):
    n_cols = w_up.shape[2]
    n_tiles = pl.cdiv(n_cols, COL_TILE)
    tile_spec = lambda rows: pl.BlockSpec((None, rows, COL_TILE), lambda i, j: (layer, 0, j))
    u = _proj_call("ffn_up_proj", _ffn_up_kernel, x, gain, w_up, layer, n_cols, (conv_w, conv_b[:, None, :]),
                   (tile_spec(conv_w.shape[1]), tile_spec(1)), BF16, _conv_scratch(x.shape[0], n_tiles))
    return _ffn_down_call(u, w_down.astype(BF16), layer, x)


def kernel(x, mixer_norm, ffn_norm, diff_w_qkv, diff_q_norm, diff_k_norm, diff_lambda_q1, diff_lambda_k1,
           diff_lambda_q2, diff_lambda_k2, diff_subln, diff_w_o, gdn_w_in, gdn_conv_w, gdn_A_log, gdn_dt_bias,
           gdn_norm, gdn_w_out, ffn_w_up, ffn_conv_w, ffn_conv_b, ffn_w_down):
    b_sz, s_len, d = x.shape
    depth = mixer_norm.shape[0]
    outs = []
    for b in range(b_sz):
        xb = x.reshape(s_len, d) if b_sz == 1 else x[b]
        for i in range(depth):
            j = i // 2
            if i % 2 == 0:
                lambda_init = 0.8 - 0.6 * math.exp(-0.3 * i)
                xb = _diff_attention_layer(xb, mixer_norm[i], diff_w_qkv, j, diff_q_norm[j], diff_k_norm[j],
                                           diff_lambda_q1[j], diff_lambda_k1[j], diff_lambda_q2[j],
                                           diff_lambda_k2[j], diff_subln[j], diff_w_o, lambda_init)
            else:
                xb = _gated_deltanet_layer(xb, mixer_norm[i], gdn_w_in, j, gdn_conv_w, gdn_A_log[j], gdn_dt_bias[j],
                                           gdn_norm[j], gdn_w_out)
            xb = _conv_ffn_layer(xb, ffn_norm[i], ffn_w_up, ffn_conv_w, ffn_conv_b, ffn_w_down, i)
        outs.append(xb)
    return outs[0].reshape(x.shape) if b_sz == 1 else jnp.stack(outs, axis=0)
```

```python
import functools
import math

import jax
import jax.numpy as jnp
from jax import lax
from jax.experimental import pallas as pl
from jax.experimental.pallas import tpu as pltpu

F32 = jnp.float32
BF16 = jnp.bfloat16

RMS_EPS = 1e-6
LOG2_E = math.log2(math.e)
BIAS_PARTS = 3
CHUNK = 64
V7X_VMEM_BYTES = 64 * 1024 * 1024
V7X_LANES = 128
V7X_SUBLANES = 8
COMPILER_TEMP_BYTES = 12 * 1024 * 1024

ROW_TILE = 1024
ROW_SUBTILE = 256
OUT_ROW_TILE = 512
DOWN_ROW_TILE = 256
COL_TILE = 512
ATTN_TILE = 512
DELTA_PAIRS = 16
DELTA_CHUNKS = 4
GATE_ROWS = 256

_NT = (((1,), (1,)), ((), ()))
_TN = (((0,), (0,)), ((), ()))


def _params(semantics, block_bytes):
    limit = min(int(block_bytes) + COMPILER_TEMP_BYTES, V7X_VMEM_BYTES - (4 << 20))
    return pltpu.CompilerParams(dimension_semantics=semantics, vmem_limit_bytes=limit)


def _rms_rows(x, gain):
    return x * lax.rsqrt(jnp.mean(x * x, axis=-1, keepdims=True) + RMS_EPS) * gain


def _silu(x):
    return x * jax.nn.sigmoid(x)


def _matmul(h, w_bf16, w_transposed):
    if w_transposed:
        return lax.dot_general(h, w_bf16, _NT, preferred_element_type=F32)
    return jnp.dot(h, w_bf16, preferred_element_type=F32)


def _normalize_once(x_ref, g_ref, h_scr):
    @pl.when(pl.program_id(1) == 0)
    def _():
        h_scr[...] = _rms_rows(x_ref[...], g_ref[...]).astype(BF16)


def _pipelined_rows(h_scr, w_bf16, w_transposed, piece_rows, epilogue):
    n_sub = h_scr.shape[0] // piece_rows
    piece = lambda r: _matmul(h_scr[r * piece_rows:(r + 1) * piece_rows, :], w_bf16, w_transposed)
    u_prev = piece(0)
    for r in range(1, n_sub):
        u_next = piece(r)
        epilogue(r - 1, u_prev)
        u_prev = u_next
    epilogue(n_sub - 1, u_prev)


def _head_proj_kernel(x_ref, g_ref, w_ref, *refs, head_dim, scale, transpose, tile, w_transposed, piece_rows):
    normalize = len(refs) == 3
    gain_ref = refs[0] if normalize else None
    o_ref, h_scr = refs[-2:]
    _normalize_once(x_ref, g_ref, h_scr)

    def epilogue(r, acc):
        if normalize:
            acc = jnp.concatenate([_rms_rows(acc[:, c:c + head_dim], gain_ref[...]) * scale
                                   for c in range(0, acc.shape[1], head_dim)], axis=1)
        r0 = r * piece_rows
        if transpose:
            o_ref[r0 // tile, :, r0 % tile:r0 % tile + piece_rows] = acc.T.astype(o_ref.dtype)
        else:
            o_ref[r0:r0 + piece_rows, :] = acc.astype(o_ref.dtype)

    _pipelined_rows(h_scr, w_ref[...].astype(BF16), w_transposed, piece_rows, epilogue)


def _plain_proj_kernel(x_ref, g_ref, w_ref, o_ref, h_scr, *, w_transposed):
    _normalize_once(x_ref, g_ref, h_scr)
    o_ref[...] = _matmul(h_scr[...], w_ref[...].astype(BF16), w_transposed).astype(o_ref.dtype)


def _conv_proj_steps(h_scr, w_ref, cw_ref, o_ref, carry_scr, u_scr, *, piece_rows, piece_cols, w_transposed, finish):
    i, j = pl.program_id(0), pl.program_id(1)
    rows, kw = h_scr.shape[0], cw_ref.shape[0]
    head = V7X_SUBLANES

    @pl.when(jnp.logical_and(i == 0, j == 0))
    def _():
        carry_scr[...] = jnp.zeros(carry_scr.shape, F32)

    u_scr[0:head, :] = carry_scr[j]
    w_bf16 = w_ref[...].astype(BF16)
    pieces = [(r0, c0) for r0 in range(0, rows, piece_rows) for c0 in range(0, o_ref.shape[1], piece_cols)]

    def matmul_piece(r0, c0):
        cols = slice(c0, c0 + piece_cols)
        w_piece = w_bf16[cols, :] if w_transposed else w_bf16[:, cols]
        u_scr[head + r0:head + r0 + piece_rows, cols] = _matmul(h_scr[r0:r0 + piece_rows, :], w_piece, w_transposed)

    def epilogue_piece(p0, c0):
        cols = slice(c0, c0 + piece_cols)
        for r0 in range(p0, p0 + piece_rows, ROW_SUBTILE):
            y = u_scr[head + r0:head + r0 + ROW_SUBTILE, cols] * cw_ref[kw - 1:kw, cols]
            for s in range(1, kw):
                y = y + u_scr[head + r0 - s:head + r0 - s + ROW_SUBTILE, cols] * cw_ref[kw - 1 - s:kw - s, cols]
            o_ref[r0:r0 + ROW_SUBTILE, cols] = finish(y, j, cols).astype(o_ref.dtype)

    matmul_piece(*pieces[0])
    for prev, cur in zip(pieces[:-1], pieces[1:]):
        matmul_piece(*cur)
        epilogue_piece(*prev)
    epilogue_piece(*pieces[-1])
    carry_scr[j] = u_scr[rows:rows + head, :]


def _ffn_up_kernel(x_ref, g_ref, w_ref, cw_ref, cb_ref, o_ref, h_scr, carry_scr, u_scr):
    _normalize_once(x_ref, g_ref, h_scr)
    _conv_proj_steps(h_scr, w_ref, cw_ref, o_ref, carry_scr, u_scr, piece_rows=h_scr.shape[0] // 2,
                     piece_cols=o_ref.shape[1], w_transposed=False,
                     finish=lambda y, jj, cols: y + cb_ref[:, cols])


def _gdn_conv_kernel(x_ref, g_ref, w_ref, cw_ref, o_ref, h_scr, carry_scr, u_scr, *, n_q_tiles, n_qk_tiles, head_dim,
                     q_scale):
    _normalize_once(x_ref, g_ref, h_scr)

    def finish(y, jj, cols):
        y = _silu(y)
        l2_scale = jnp.where(jj < n_q_tiles, q_scale, 1.0)
        heads = []
        for c in range(0, y.shape[1], head_dim):
            blk = y[:, c:c + head_dim]
            inv = lax.rsqrt(jnp.sum(blk * blk, axis=-1, keepdims=True) + RMS_EPS) * l2_scale
            heads.append(blk * jnp.where(jj < n_qk_tiles, inv, 1.0))
        return jnp.concatenate(heads, axis=1)

    _conv_proj_steps(h_scr, w_ref, cw_ref, o_ref, carry_scr, u_scr, piece_rows=ROW_SUBTILE,
                     piece_cols=o_ref.shape[1], w_transposed=True, finish=finish)


def _proj_call(name, kernel, x, gain, w3, layer, n_cols, extra_inputs, extra_specs, out_dtype, scratch,
               col_tile=COL_TILE, first_col_tile=0, transposed_tile=None, w_transposed=False):
    s_len, d = x.shape
    tm, tn = min(ROW_TILE, s_len), col_tile
    grid = (s_len // tm, pl.cdiv(n_cols, tn))
    if w_transposed:
        w_spec = pl.BlockSpec((None, tn, d), lambda i, j: (layer, first_col_tile + j, 0))
    else:
        w_spec = pl.BlockSpec((None, d, tn), lambda i, j: (layer, 0, first_col_tile + j))
    in_specs = [pl.BlockSpec((tm, d), lambda i, j: (i, 0)), pl.BlockSpec((1, d), lambda i, j: (0, 0)), w_spec]
    in_specs += list(extra_specs)
    if transposed_tile is None:
        out_shape = jax.ShapeDtypeStruct((s_len, n_cols), out_dtype)
        out_spec = pl.BlockSpec((tm, tn), lambda i, j: (i, j))
    else:
        out_shape = jax.ShapeDtypeStruct((s_len // transposed_tile, n_cols, transposed_tile), out_dtype)
        out_spec = pl.BlockSpec((tm // transposed_tile, tn, transposed_tile), lambda i, j: (i, j, 0))
    out_bytes = jnp.dtype(out_dtype).itemsize
    block_bytes = 2 * tm * d * 4 + tm * d * 2 + 2 * d * tn * 4 + d * tn * 2 + 2 * tm * tn * out_bytes + 3 * tm * tn * 4
    return pl.pallas_call(
        kernel,
        name=name,
        out_shape=out_shape,
        grid=grid,
        in_specs=in_specs,
        out_specs=out_spec,
        scratch_shapes=[pltpu.VMEM((tm, d), BF16)] + list(scratch),
        compiler_params=_params(("arbitrary", "arbitrary"), block_bytes),
    )(x, gain.reshape(1, d), w3, *extra_inputs)


def _conv_scratch(s_len, n_tiles):
    tm = min(ROW_TILE, s_len)
    return (pltpu.VMEM((n_tiles, V7X_SUBLANES, COL_TILE), F32), pltpu.VMEM((V7X_SUBLANES + tm, COL_TILE), F32))


def _mm_res_kernel(a_ref, w_ref, r_ref, o_ref, *w_scr):
    if w_scr:
        @pl.when(pl.program_id(0) == 0)
        def _():
            for r0 in range(0, w_ref.shape[0], COL_TILE):
                w_scr[0][r0:r0 + COL_TILE, :] = w_ref[r0:r0 + COL_TILE, :].astype(BF16)
        w = w_scr[0][...]
    else:
        w = w_ref[...]
    o_ref[...] = r_ref[...] + jnp.dot(a_ref[...], w, preferred_element_type=F32)


def _mm_res_call(a, w, res, layer=None):
    s_len, k = a.shape
    n = w.shape[-1]
    tm = min(OUT_ROW_TILE, s_len)
    once = pl.Buffered(1)
    if w.ndim == 3:
        w_spec = pl.BlockSpec((None, k, n), lambda i: (layer, 0, 0), pipeline_mode=once)
        scratch = [pltpu.VMEM((k, n), BF16)]
    else:
        w_spec = pl.BlockSpec((k, n), lambda i: (0, 0), pipeline_mode=once)
        scratch = []
    block_bytes = 2 * tm * k * 2 + k * n * w.dtype.itemsize + len(scratch) * k * n * 2 + 5 * tm * n * 4
    return pl.pallas_call(
        _mm_res_kernel,
        name="out_proj_residual",
        out_shape=jax.ShapeDtypeStruct((s_len, n), F32),
        grid=(s_len // tm,),
        in_specs=[pl.BlockSpec((tm, k), lambda i: (i, 0)), w_spec, pl.BlockSpec((tm, n), lambda i: (i, 0))],
        out_specs=pl.BlockSpec((tm, n), lambda i: (i, 0)),
        scratch_shapes=scratch,
        compiler_params=_params(("arbitrary",), block_bytes),
    )(a, w, res)


def _ffn_down_kernel(uv_ref, ug_ref, w_ref, r_ref, o_ref, g_scr):
    d_ff = w_ref.shape[0]
    mid = pl.cdiv(d_ff // 2, 2 * V7X_LANES) * 2 * V7X_LANES
    acc = r_ref[...]
    for k0, k1 in ((0, mid), (mid, d_ff)):
        for c0 in range(k0, k1, COL_TILE):
            cols = slice(c0, min(c0 + COL_TILE, k1))
            g_scr[:, cols] = (_silu(ug_ref[:, cols].astype(F32)) * uv_ref[:, cols].astype(F32)).astype(BF16)
        acc = acc + jnp.dot(g_scr[:, k0:k1], w_ref[k0:k1, :], preferred_element_type=F32)
    o_ref[...] = acc


def _ffn_down_call(u, w_down, layer, res):
    s_len = u.shape[0]
    _, d_ff, n = w_down.shape
    tm = min(DOWN_ROW_TILE, s_len)
    block_bytes = 4 * tm * d_ff * 2 + tm * d_ff * 2 + d_ff * n * 2 + 6 * tm * n * 4
    return pl.pallas_call(
        _ffn_down_kernel,
        name="ffn_down_residual",
        out_shape=jax.ShapeDtypeStruct((s_len, n), F32),
        grid=(s_len // tm,),
        in_specs=[
            pl.BlockSpec((tm, d_ff), lambda i: (i, 0)),
            pl.BlockSpec((tm, d_ff), lambda i: (i, 1)),
            pl.BlockSpec((None, d_ff, n), lambda i: (layer, 0, 0), pipeline_mode=pl.Buffered(1)),
            pl.BlockSpec((tm, n), lambda i: (i, 0)),
        ],
        out_specs=pl.BlockSpec((tm, n), lambda i: (i, 0)),
        scratch_shapes=[pltpu.VMEM((tm, d_ff), BF16)],
        compiler_params=_params(("arbitrary",), block_bytes),
    )(u, u, w_down, res)


def _attn_kernel(slopes_ref, qt_ref, k_ref, vt_ref, lq1_ref, lk1_ref, lq2_ref, lk2_ref, subln_ref, o_ref,
                 s_scr, kb_scr, bias_scr, m_scr, l_scr, acc_scr, *, tile, head_dim, lambda_init):
    h, qi = pl.program_id(0), pl.program_id(1)
    slope = slopes_ref[h]
    hd = head_dim

    @pl.when(qi == 0)
    def _():
        key_bias = slope * lax.broadcasted_iota(jnp.int32, (tile, hd), 0).astype(F32)
        lane = lax.broadcasted_iota(jnp.int32, (tile, hd), 1)
        part, parts = key_bias, []
        for _ in range(BIAS_PARTS):
            parts.append(part.astype(BF16).astype(F32))
            part = part - parts[-1]
        kb_cols = jnp.zeros((tile, hd), F32)
        for c in reversed(range(BIAS_PARTS)):
            kb_cols = jnp.where(lane == c, parts[c], kb_cols)
        kb_scr[...] = kb_cols.astype(BF16)
        key = lax.broadcasted_iota(jnp.int32, (tile, tile), 0)
        qry = lax.broadcasted_iota(jnp.int32, (tile, tile), 1)
        bias_scr[...] = jnp.where((key // CHUNK) <= (qry // CHUNK),
                                  -slope * (jnp.abs(key - qry) + key).astype(F32), -jnp.inf)

    ones_rows = jnp.where(lax.broadcasted_iota(jnp.int32, (hd, tile), 0) < BIAS_PARTS, 1.0, 0.0).astype(BF16)

    def issue_scores(kj, slot):
        k_blk = k_ref[pl.ds(pl.multiple_of(kj * tile, tile), tile), :]
        for m in range(2):
            rows = slice(m * hd, (m + 1) * hd)
            s_scr[slot, m] = jnp.dot(jnp.concatenate([k_blk[:, rows], kb_scr[...]], axis=1),
                                     jnp.concatenate([qt_ref[rows, :], ones_rows], axis=0),
                                     preferred_element_type=F32)

    issue_scores(qi, 1)
    issue_scores(0, 0)

    vt_diag = vt_ref[qi]
    for m in range(2):
        z = s_scr[1, m] + bias_scr[...]
        mx = jnp.max(z, axis=0, keepdims=True)
        p = jnp.exp2(z - mx)
        m_scr[m] = mx
        l_scr[m] = jnp.sum(p, axis=0, keepdims=True)
        acc_scr[m] = jnp.dot(vt_diag, p.astype(BF16), preferred_element_type=F32)

    q_pos = (lax.broadcasted_iota(jnp.int32, (1, tile), 1) + qi * tile).astype(F32)

    def kv_step(kj, slot, issue_next):
        if issue_next:
            issue_scores(jnp.minimum(kj + 1, qi - 1), 1 - slot)
        c = -slope * (q_pos - (kj * tile).astype(F32))
        vt_blk = vt_ref[kj]
        for m in range(2):
            z = s_scr[slot, m]
            m_old = m_scr[m]
            m_new = jnp.maximum(m_old, jnp.max(z, axis=0, keepdims=True) + c)
            p = jnp.exp2(z - (m_new - c))
            alpha = jnp.exp2(m_old - m_new)
            l_scr[m] = alpha * l_scr[m] + jnp.sum(p, axis=0, keepdims=True)
            acc_scr[m] = alpha * acc_scr[m] + jnp.dot(vt_blk, p.astype(BF16), preferred_element_type=F32)
            m_scr[m] = m_new

    def two_steps(t, carry):
        kv_step(2 * t, 0, True)
        kv_step(2 * t + 1, 1, True)
        return carry

    lax.fori_loop(0, qi // 2, two_steps, 0)

    @pl.when(lax.rem(qi, 2) == 1)
    def _():
        kv_step(qi - 1, 0, False)

    lam = (jnp.exp(jnp.sum(lq1_ref[...] * lk1_ref[...], axis=-1, keepdims=True))
           - jnp.exp(jnp.sum(lq2_ref[...] * lk2_ref[...], axis=-1, keepdims=True)) + lambda_init)
    o_t = acc_scr[0] / l_scr[0] - lam * (acc_scr[1] / l_scr[1])
    inv = lax.rsqrt(jnp.mean(o_t * o_t, axis=0, keepdims=True) + RMS_EPS)
    o_ref[...] = ((o_t * inv * subln_ref[...]) * (1.0 - lambda_init)).T.astype(o_ref.dtype)


def _attn_call(qt, k, vt, slopes, lq1, lk1, lq2, lk2, subln, *, n_heads, head_dim, lambda_init):
    n_tiles, _, tile = qt.shape
    s_len = k.shape[0]
    hw = 2 * head_dim
    vec = lambda a: a.reshape(1, -1)
    vec_spec = lambda n: pl.BlockSpec((1, n), lambda h, qi: (0, 0))
    block_bytes = 2 * (2 * s_len * hw * 2) + 4 * tile * hw * 2 + 2 * tile * hw * 4 + 5 * tile * tile * 4 \
        + 6 * tile * tile * 4
    return pl.pallas_call(
        functools.partial(_attn_kernel, tile=tile, head_dim=head_dim, lambda_init=lambda_init),
        name="diff_attention",
        out_shape=jax.ShapeDtypeStruct((s_len, n_heads * hw), BF16),
        grid=(n_heads, n_tiles),
        in_specs=[
            pl.BlockSpec(memory_space=pltpu.SMEM),
            pl.BlockSpec((None, hw, tile), lambda h, qi: (qi, h, 0)),
            pl.BlockSpec((s_len, hw), lambda h, qi: (0, h)),
            pl.BlockSpec((n_tiles, hw, tile), lambda h, qi: (0, h, 0)),
            vec_spec(head_dim), vec_spec(head_dim), vec_spec(head_dim), vec_spec(head_dim),
            pl.BlockSpec((hw, 1), lambda h, qi: (0, 0)),
        ],
        out_specs=pl.BlockSpec((tile, hw), lambda h, qi: (qi, h)),
        scratch_shapes=[
            pltpu.VMEM((2, 2, tile, tile), F32),
            pltpu.VMEM((tile, head_dim), BF16),
            pltpu.VMEM((tile, tile), F32),
            pltpu.VMEM((2, 1, tile), F32),
            pltpu.VMEM((2, 1, tile), F32),
            pltpu.VMEM((2, hw, tile), F32),
        ],
        compiler_params=_params(("arbitrary", "arbitrary"), block_bytes),
    )(slopes, qt, k, vt, vec(lq1), vec(lk1), vec(lq2), vec(lk2), subln.reshape(-1, 1))


def _gates_kernel(ba_ref, alog_ref, dtb_ref, beta_ref, gc_ref):
    rows = ba_ref.shape[0]
    beta_ref[...] = jax.nn.sigmoid(ba_ref[...])
    g = -jnp.exp(alog_ref[...]) * jax.nn.softplus(ba_ref[...] + dtb_ref[...])
    r = lax.broadcasted_iota(jnp.int32, (rows, rows), 0)
    c = lax.broadcasted_iota(jnp.int32, (rows, rows), 1)
    tri = jnp.where(r // CHUNK == c // CHUNK, jnp.where(c <= r, 1.0, 0.0), 0.0).astype(F32)
    gc_ref[...] = jnp.dot(tri, g, preferred_element_type=F32, precision=lax.Precision.HIGHEST)


def _gates_call(ba, a_log_row, dt_bias_row):
    s_len, width = ba.shape
    tm = min(GATE_ROWS, s_len)
    out = jax.ShapeDtypeStruct((s_len, width), F32)
    row_spec = pl.BlockSpec((tm, width), lambda i: (i, 0))
    vec_spec = pl.BlockSpec((1, width), lambda i: (0, 0))
    return pl.pallas_call(
        _gates_kernel,
        name="gdn_gates",
        out_shape=(out, out),
        grid=(s_len // tm,),
        in_specs=[row_spec, vec_spec, vec_spec],
        out_specs=(row_spec, row_spec),
        compiler_params=_params(("arbitrary",), 8 * tm * 2 * V7X_LANES * 4 + 4 * tm * tm * 4),
    )(ba, a_log_row, dt_bias_row)


def _split_bf16(x):
    hi = x.astype(BF16)
    return hi, (x - hi.astype(F32)).astype(BF16)


def _delta_kernel(q_ref, k_ref, v_ref, z_ref, beta_ref, gc_ref, gct_ref, ng_ref, o_ref, state_scr, *,
                  n_pairs, n_chunks, head_dim):
    hd = head_dim
    pairs = range(n_pairs)

    @pl.when(pl.program_id(1) == 0)
    def _():
        state_scr[...] = jnp.zeros(state_scr.shape, F32)

    ri = lax.broadcasted_iota(jnp.int32, (CHUNK, 2 * CHUNK), 0)
    lane = lax.broadcasted_iota(jnp.int32, (CHUNK, 2 * CHUNK), 1)
    left = lane < CHUNK
    ci = jnp.where(left, lane, lane - CHUNK)
    tril = ci <= ri
    strict = ci < ri
    eye = jnp.where(ci == ri, 1.0, 0.0).astype(F32)
    state_left = lax.broadcasted_iota(jnp.int32, (hd, 2 * hd), 1) < hd

    def block_diag(y, keep_left):
        zero = jnp.zeros_like(y)
        return jnp.concatenate([jnp.where(keep_left, y, zero), jnp.where(keep_left, zero, y)], axis=0)

    def pair_matmul(xh, xl, yh, yl):
        bh, bl = block_diag(yh, left), block_diag(yl, left)
        rhs = jnp.concatenate([jnp.concatenate([bh, bl], axis=1),
                               jnp.concatenate([bh, jnp.zeros_like(bl)], axis=1)], axis=0)
        out = jnp.dot(jnp.concatenate([xh, xl], axis=1), rhs, preferred_element_type=F32)
        return out[:, :2 * CHUNK] + out[:, 2 * CHUNK:]

    def chunk_step(c, carry):
        rows = pl.ds(pl.multiple_of(c * CHUNK, CHUNK), CHUNK)
        beta = beta_ref[rows, :]
        g_col = gc_ref[rows, :]
        g_rows = gct_ref[0, c]
        b_rows = gct_ref[1, c]
        lanes = lambda a, h: jnp.broadcast_to(a[:, h:h + 1], (CHUNK, hd))

        k = [k_ref[rows, p * hd:(p + 1) * hd] for p in pairs]
        q = [q_ref[rows, p * hd:(p + 1) * hd] for p in pairs]
        kq = [lax.dot_general(jnp.concatenate([k[p], q[p]], axis=0), jnp.concatenate([k[p], k[p]], axis=0), _NT,
                              preferred_element_type=F32) for p in pairs]
        g_row = [g_rows[p:p + 1, :] for p in pairs]
        lmat, qk_decay, q_dec, k_dec = [], [], [], []
        for p in pairs:
            g_a, g_b = lanes(g_col, 2 * p), lanes(g_col, 2 * p + 1)
            b_pk = jnp.where(left, lanes(beta, 2 * p), lanes(beta, 2 * p + 1))
            decay = jnp.exp(jnp.where(tril, jnp.where(left, g_a, g_b) - g_row[p], -jnp.inf))
            lmat.append(jnp.where(strict, kq[p][:CHUNK] * b_pk * decay, 0.0))
            qk_decay.append((kq[p][CHUNK:] * decay).astype(BF16))
            qf, kf = q[p].astype(F32), k[p].astype(F32)
            q_dec.append(jnp.concatenate([qf * jnp.exp(g_a), qf * jnp.exp(g_b)], axis=1).astype(BF16))
            ge_a, ge_b = g_row[p][:, CHUNK - 1:CHUNK], g_row[p][:, 2 * CHUNK - 1:2 * CHUNK]
            k_dec.append(((kf * jnp.exp(ge_a - g_a)).astype(BF16), (kf * jnp.exp(ge_b - g_b)).astype(BF16)))
        lsplit = [_split_bf16(lmat[p]) for p in pairs]
        x = [pair_matmul(*lsplit[p], *lsplit[p]) for p in pairs]
        acc = [eye - lmat[p] for p in pairs]
        power = 2
        while power * 2 < CHUNK:
            for p in pairs:
                sh, sl = _split_bf16(jnp.concatenate([acc[p], x[p]], axis=0))
                both = pair_matmul(sh, sl, sh[CHUNK:], sl[CHUNK:])
                acc[p] = acc[p] + both[:CHUNK]
                x[p] = both[CHUNK:]
            power *= 2
        tmat = [acc[p] + pair_matmul(*_split_bf16(acc[p]), *_split_bf16(x[p])) for p in pairs]
        zb = jnp.zeros((CHUNK, hd), BF16)
        uw = []
        for p in pairs:
            b_row = b_rows[p:p + 1, :]
            t_scaled = jnp.concatenate([tmat[p] * b_row, tmat[p] * (b_row * jnp.exp(g_row[p]))], axis=1)
            v_pk = v_ref[rows, 2 * p * hd:2 * (p + 1) * hd]
            rhs = jnp.concatenate([jnp.concatenate([v_pk[:, :hd], zb, zb, zb], axis=1),
                                   jnp.concatenate([zb, v_pk[:, hd:], zb, zb], axis=1),
                                   jnp.concatenate([zb, zb, k[p], zb], axis=1),
                                   jnp.concatenate([zb, zb, zb, k[p]], axis=1)], axis=0)
            uw.append(jnp.dot(t_scaled.astype(BF16), rhs, preferred_element_type=F32))
        s_old = [state_scr[p] for p in pairs]
        wq = []
        for p in pairs:
            wq.append(jnp.dot(jnp.concatenate([uw[p][:, 2 * hd:].astype(BF16), q_dec[p]], axis=0),
                              block_diag(s_old[p].astype(BF16), state_left), preferred_element_type=F32))
        o_pk = []
        for p in pairs:
            vn = (uw[p][:, :2 * hd] - wq[p][:CHUNK]).astype(BF16)
            vn_diag = jnp.concatenate([jnp.concatenate([vn[:, :hd], zb], axis=1),
                                       jnp.concatenate([zb, vn[:, hd:]], axis=1)], axis=0)
            o_pk.append(wq[p][CHUNK:] + jnp.dot(qk_decay[p], vn_diag, preferred_element_type=F32))
            ge_a, ge_b = g_row[p][:, CHUNK - 1:CHUNK], g_row[p][:, 2 * CHUNK - 1:2 * CHUNK]
            ds_a = lax.dot_general(k_dec[p][0], vn[:, :hd], _TN, preferred_element_type=F32)
            ds_b = lax.dot_general(k_dec[p][1], vn[:, hd:], _TN, preferred_element_type=F32)
            state_scr[p] = jnp.concatenate([s_old[p][:, :hd] * jnp.exp(ge_a) + ds_a,
                                            s_old[p][:, hd:] * jnp.exp(ge_b) + ds_b], axis=1)
        for p in pairs:
            wide_cols = slice(2 * p * hd, 2 * (p + 1) * hd)
            z_pk = z_ref[rows, wide_cols].astype(F32)
            gated = [_rms_rows(o_pk[p][:, h * hd:(h + 1) * hd], ng_ref[...]) * _silu(z_pk[:, h * hd:(h + 1) * hd])
                     for h in range(2)]
            o_ref[rows, wide_cols] = jnp.concatenate(gated, axis=1).astype(o_ref.dtype)
        return carry

    lax.fori_loop(0, n_chunks, chunk_step, 0)


def _delta_call(qkv, z, beta_g, gc_g, gct_g, norm_g, *, n_qk_heads, head_dim):
    s_len = qkv.shape[0]
    n_groups = n_qk_heads // DELTA_PAIRS
    heads = 2 * DELTA_PAIRS
    rows = DELTA_CHUNKS * CHUNK
    qk_w, v_w = DELTA_PAIRS * head_dim, heads * head_dim
    key_dim, value_dim = n_qk_heads * head_dim, 2 * n_qk_heads * head_dim
    k_off, v_off = key_dim // qk_w, 2 * key_dim // v_w
    block_bytes = 2 * rows * (2 * qk_w + 3 * v_w) * 2 + 6 * rows * V7X_LANES * 4 + heads * head_dim * head_dim * 4
    return pl.pallas_call(
        functools.partial(_delta_kernel, n_pairs=DELTA_PAIRS, n_chunks=DELTA_CHUNKS, head_dim=head_dim),
        name="gdn_delta_rule",
        out_shape=jax.ShapeDtypeStruct((s_len, value_dim), BF16),
        grid=(n_groups, s_len // rows),
        in_specs=[
            pl.BlockSpec((rows, qk_w), lambda g, b: (b, g)),
            pl.BlockSpec((rows, qk_w), lambda g, b: (b, k_off + g)),
            pl.BlockSpec((rows, v_w), lambda g, b: (b, v_off + g)),
            pl.BlockSpec((rows, v_w), lambda g, b: (b, g)),
            pl.BlockSpec((None, rows, heads), lambda g, b: (g, b, 0)),
            pl.BlockSpec((None, rows, heads), lambda g, b: (g, b, 0)),
            pl.BlockSpec((None, 2, DELTA_CHUNKS, DELTA_PAIRS, 2 * CHUNK), lambda g, b: (g, 0, b, 0, 0)),
            pl.BlockSpec((1, head_dim), lambda g, b: (0, 0)),
        ],
        out_specs=pl.BlockSpec((rows, v_w), lambda g, b: (b, g)),
        scratch_shapes=[pltpu.VMEM((DELTA_PAIRS, head_dim, 2 * head_dim), F32)],
        compiler_params=_params(("arbitrary", "arbitrary"), block_bytes),
    )(qkv, qkv, qkv, z, beta_g, gc_g, gct_g, norm_g.reshape(1, head_dim))


def _diff_attention_layer(x, gain, w_qkv, layer, q_norm, k_norm, lq1, lk1, lq2, lk2, subln, w_o, lambda_init):
    s_len = x.shape[0]
    head_dim = q_norm.shape[0]
    n_heads = w_o.shape[1] // subln.shape[0]
    qk_dim = 2 * n_heads * head_dim
    v_dim = w_qkv.shape[2] - 2 * qk_dim
    tile = min(ATTN_TILE, s_len)
    vec_spec = pl.BlockSpec((1, head_dim), lambda i, j: (0, 0))

    def proj(name, n_cols, first_col, head_gain, scale, transpose):
        head_norm = ((head_gain.reshape(1, -1),), (vec_spec,)) if head_gain is not None else ((), ())
        return _proj_call(
            name, functools.partial(_head_proj_kernel, head_dim=head_dim, scale=scale, transpose=transpose, tile=tile,
                                    w_transposed=False, piece_rows=ROW_SUBTILE),
            x, gain, w_qkv, layer, n_cols, *head_norm, BF16, (),
            first_col_tile=first_col // COL_TILE, transposed_tile=tile if transpose else None)

    qt = proj("attn_q_proj", qk_dim, 0, q_norm, head_dim ** -0.5 * LOG2_E, True)
    k = proj("attn_k_proj", qk_dim, qk_dim, k_norm, 1.0, False)
    vt = proj("attn_v_proj", v_dim, 2 * qk_dim, None, 1.0, True)
    slopes = jnp.exp2(-8.0 * jnp.arange(1, n_heads + 1, dtype=F32) / n_heads) * LOG2_E
    o = _attn_call(qt, k, vt, slopes, lq1, lk1, lq2, lk2, subln, n_heads=n_heads, head_dim=head_dim,
                   lambda_init=lambda_init)
    return _mm_res_call(o, w_o, x, layer)


def _gated_deltanet_layer(x, gain, w_in, layer, conv_w, a_log, dt_bias, norm_g, w_out):
    s_len, d = x.shape
    head_dim = norm_g.shape[0]
    n_v_heads = a_log.shape[0]
    n_qk_heads = n_v_heads // 2
    key_dim, value_dim = n_qk_heads * head_dim, n_v_heads * head_dim
    conv_dim = 2 * key_dim + value_dim
    main_cols = conv_dim + value_dim
    n_conv_tiles = conv_dim // COL_TILE
    w_in_t = jnp.swapaxes(w_in, 1, 2)
    cw_spec = pl.BlockSpec((None, conv_w.shape[1], COL_TILE), lambda i, j: (layer, 0, j))
    qkv = _proj_call(
        "gdn_in_proj", functools.partial(_gdn_conv_kernel, n_q_tiles=key_dim // COL_TILE,
                                         n_qk_tiles=2 * key_dim // COL_TILE, head_dim=head_dim,
                                         q_scale=head_dim ** -0.5),
        x, gain, w_in_t, layer, conv_dim, (conv_w,), (cw_spec,), BF16, _conv_scratch(s_len, n_conv_tiles),
        w_transposed=True)
    z = _proj_call(
        "gdn_z_proj", functools.partial(_head_proj_kernel, head_dim=head_dim, scale=1.0, transpose=False, tile=None,
                                        w_transposed=True, piece_rows=min(ROW_TILE, s_len)),
        x, gain, w_in_t, layer, value_dim, (), (), BF16, (), first_col_tile=n_conv_tiles, w_transposed=True)

    gate_cols = 2 * n_v_heads
    assert main_cols % gate_cols == 0
    ba = _proj_call("gdn_gate_proj", functools.partial(_plain_proj_kernel, w_transposed=True), x, gain, w_in_t, layer,
                    gate_cols, (), (), F32, (), col_tile=gate_cols, first_col_tile=main_cols // gate_cols,
                    w_transposed=True)
    on_a_lanes = lambda v: jnp.pad(v, (n_v_heads, 0)).reshape(1, -1)
    beta, gc = _gates_call(ba, on_a_lanes(a_log), on_a_lanes(dt_bias))
    beta, gc = beta[:, :n_v_heads], gc[:, n_v_heads:2 * n_v_heads]

    heads = 2 * DELTA_PAIRS
    n_groups = n_v_heads // heads
    beta_g = beta.reshape(s_len, n_groups, heads).transpose(1, 0, 2)
    gc_g = gc.reshape(s_len, n_groups, heads).transpose(1, 0, 2)
    rows_of = lambda a: a.reshape(s_len // CHUNK, CHUNK, n_groups, DELTA_PAIRS, 2).transpose(
        2, 0, 3, 4, 1).reshape(n_groups, s_len // CHUNK, DELTA_PAIRS, 2 * CHUNK)
    gct_g = jnp.stack([rows_of(gc), rows_of(beta)], axis=1)
    o = _delta_call(qkv, z, beta_g, gc_g, gct_g, norm_g, n_qk_heads=n_qk_heads, head_dim=head_dim)
    return _mm_res_call(o, w_out[layer].astype(BF16), x)


def _conv_ffn_layer(x, gain, w_up, conv_w, conv_b, w_down, layer):
    n_cols = w_up.shape[2]
    n_tiles = pl.cdiv(n_cols, COL_TILE)
    tile_spec = lambda rows: pl.BlockSpec((None, rows, COL_TILE), lambda i, j: (layer, 0, j))
    u = _proj_call("ffn_up_proj", _ffn_up_kernel, x, gain, w_up, layer, n_cols, (conv_w, conv_b[:, None, :]),
                   (tile_spec(conv_w.shape[1]), tile_spec(1)), BF16, _conv_scratch(x.shape[0], n_tiles))
    return _ffn_down_call(u, w_down.astype(BF16), layer, x)


def kernel(x, mixer_norm, ffn_norm, diff_w_qkv, diff_q_norm, diff_k_norm, diff_lambda_q1, diff_lambda_k1,
           diff_lambda_q2, diff_lambda_k2, diff_subln, diff_w_o, gdn_w_in, gdn_conv_w, gdn_A_log, gdn_dt_bias,
           gdn_norm, gdn_w_out, ffn_w_up, ffn_conv_w, ffn_conv_b, ffn_w_down):
    b_sz, s_len, d = x.shape
    depth = mixer_norm.shape[0]
    outs = []
    for b in range(b_sz):
        xb = x.reshape(s_len, d) if b_sz == 1 else x[b]
        for i in range(depth):
            j = i // 2
            if i % 2 == 0:
                lambda_init = 0.8 - 0.6 * math.exp(-0.3 * i)
                xb = _diff_attention_layer(xb, mixer_norm[i], diff_w_qkv, j, diff_q_norm[j], diff_k_norm[j],
                                           diff_lambda_q1[j], diff_lambda_k1[j], diff_lambda_q2[j],
                                           diff_lambda_k2[j], diff_subln[j], diff_w_o, lambda_init)
            else:
                xb = _gated_deltanet_layer(xb, mixer_norm[i], gdn_w_in, j, gdn_conv_w, gdn_A_log[j], gdn_dt_bias[j],
                                           gdn_norm[j], gdn_w_out)
            xb = _conv_ffn_layer(xb, ffn_norm[i], ffn_w_up, ffn_conv_w, ffn_conv_b, ffn_w_down, i)
        outs.append(xb)
    return outs[0].reshape(x.shape) if b_sz == 1 else jnp.stack(outs, axis=0)
```

```python
import functools
import math

import jax
import jax.numpy as jnp
from jax import lax
from jax.experimental import pallas as pl
from jax.experimental.pallas import tpu as pltpu

F32 = jnp.float32
BF16 = jnp.bfloat16

RMS_EPS = 1e-6
LOG2_E = math.log2(math.e)
BIAS_PARTS = 3
CHUNK = 64
V7X_VMEM_BYTES = 64 * 1024 * 1024
V7X_LANES = 128
V7X_SUBLANES = 8
COMPILER_TEMP_BYTES = 12 * 1024 * 1024

ROW_TILE = 1024
ROW_SUBTILE = 256
OUT_ROW_TILE = 512
DOWN_ROW_TILE = 256
COL_TILE = 512
ATTN_TILE = 512
DELTA_PAIRS = 16
DELTA_CHUNKS = 4
GATE_ROWS = 256

_NT = (((1,), (1,)), ((), ()))
_TN = (((0,), (0,)), ((), ()))


def _params(semantics, block_bytes):
    limit = min(int(block_bytes) + COMPILER_TEMP_BYTES, V7X_VMEM_BYTES - (4 << 20))
    return pltpu.CompilerParams(dimension_semantics=semantics, vmem_limit_bytes=limit)


def _rms_rows(x, gain):
    return x * lax.rsqrt(jnp.mean(x * x, axis=-1, keepdims=True) + RMS_EPS) * gain


def _silu(x):
    return x * jax.nn.sigmoid(x)


def _matmul(h, w_bf16, w_transposed):
    dims = _NT if w_transposed else (((1,), (0,)), ((), ()))
    return lax.dot_general(h, w_bf16, dims, preferred_element_type=F32)


def _normalize_once(x_ref, g_ref, h_scr):
    @pl.when(pl.program_id(1) == 0)
    def _():
        h_scr[...] = _rms_rows(x_ref[...], g_ref[...]).astype(BF16)


def _pipelined_rows(h_scr, w_bf16, w_transposed, piece_rows, epilogue):
    n_sub = h_scr.shape[0] // piece_rows
    piece = lambda r: _matmul(h_scr[r * piece_rows:(r + 1) * piece_rows, :], w_bf16, w_transposed)
    u_prev = piece(0)
    for r in range(1, n_sub):
        u_next = piece(r)
        epilogue(r - 1, u_prev)
        u_prev = u_next
    epilogue(n_sub - 1, u_prev)


def _head_proj_kernel(x_ref, g_ref, w_ref, *refs, head_dim, scale, transpose, tile, w_transposed, piece_rows):
    normalize = len(refs) == 3
    gain_ref = refs[0] if normalize else None
    o_ref, h_scr = refs[-2:]
    _normalize_once(x_ref, g_ref, h_scr)

    def epilogue(r, acc):
        if normalize:
            acc = jnp.concatenate([_rms_rows(acc[:, c:c + head_dim], gain_ref[...]) * scale
                                   for c in range(0, acc.shape[1], head_dim)], axis=1)
        r0 = r * piece_rows
        if transpose:
            o_ref[r0 // tile, :, r0 % tile:r0 % tile + piece_rows] = acc.T.astype(o_ref.dtype)
        else:
            o_ref[r0:r0 + piece_rows, :] = acc.astype(o_ref.dtype)

    _pipelined_rows(h_scr, w_ref[...].astype(BF16), w_transposed, piece_rows, epilogue)


def _plain_proj_kernel(x_ref, g_ref, w_ref, o_ref, h_scr, *, w_transposed):
    _normalize_once(x_ref, g_ref, h_scr)
    o_ref[...] = _matmul(h_scr[...], w_ref[...].astype(BF16), w_transposed).astype(o_ref.dtype)


def _conv_proj_steps(h_scr, w_ref, cw_ref, o_ref, carry_scr, u_scr, *, piece_rows, piece_cols, w_transposed, finish):
    i, j = pl.program_id(0), pl.program_id(1)
    rows, kw = h_scr.shape[0], cw_ref.shape[0]
    head = V7X_SUBLANES

    @pl.when(jnp.logical_and(i == 0, j == 0))
    def _():
        carry_scr[...] = jnp.zeros(carry_scr.shape, F32)

    u_scr[0:head, :] = carry_scr[j]
    w_bf16 = w_ref[...].astype(BF16)
    pieces =[(r0, c0) for r0 in range(0, rows, piece_rows) for c0 in range(0, o_ref.shape[1], piece_cols)]

    def matmul_piece(r0, c0):
        cols = slice(c0, c0 + piece_cols)
        w_piece = w_bf16[cols, :] if w_transposed else w_bf16[:, cols]
        u_scr[head + r0:head + r0 + piece_rows, cols] = _matmul(h_scr[r0:r0 + piece_rows, :], w_piece, w_transposed)

    def epilogue_piece(p0, c0):
        cols = slice(c0, c0 + piece_cols)
        for r0 in range(p0, p0 + piece_rows, ROW_SUBTILE):
            y = u_scr[head + r0:head + r0 + ROW_SUBTILE, cols] * cw_ref[kw - 1:kw, cols]
            for s in range(1, kw):
                y = y + u_scr[head + r0 - s:head + r0 - s + ROW_SUBTILE, cols] * cw_ref[kw - 1 - s:kw - s, cols]
            o_ref[r0:r0 + ROW_SUBTILE, cols] = finish(y, j, cols).astype(o_ref.dtype)

    matmul_piece(*pieces[0])
    for prev, cur in zip(pieces[:-1], pieces[1:]):
        matmul_piece(*cur)
        epilogue_piece(*prev)
    epilogue_piece(*pieces[-1])
    carry_scr[j] = u_scr[rows:rows + head, :]


def _ffn_up_kernel(x_ref, g_ref, w_ref, cw_ref, cb_ref, o_ref, h_scr, carry_scr, u_scr):
    _normalize_once(x_ref, g_ref, h_scr)
    _conv_proj_steps(h_scr, w_ref, cw_ref, o_ref, carry_scr, u_scr, piece_rows=h_scr.shape[0] // 2,
                     piece_cols=o_ref.shape[1], w_transposed=False,
                     finish=lambda y, jj, cols: y + cb_ref[:, cols])


def _gdn_conv_kernel(x_ref, g_ref, w_ref, cw_ref, o_ref, h_scr, carry_scr, u_scr, *, n_q_tiles, n_qk_tiles, head_dim,
                     q_scale):
    _normalize_once(x_ref, g_ref, h_scr)

    def finish(y, jj, cols):
        y = _silu(y)
        l2_scale = jnp.where(jj < n_q_tiles, q_scale, 1.0)
        heads = []
        for c in range(0, y.shape[1], head_dim):
            blk = y[:, c:c + head_dim]
            inv = lax.rsqrt(jnp.sum(blk * blk, axis=-1, keepdims=True) + RMS_EPS) * l2_scale
            heads.append(blk * jnp.where(jj < n_qk_tiles, inv, 1.0))
        return jnp.concatenate(heads, axis=1)

    _conv_proj_steps(h_scr, w_ref, cw_ref, o_ref, carry_scr, u_scr, piece_rows=ROW_SUBTILE,
                     piece_cols=o_ref.shape[1], w_transposed=False, finish=finish)


def _proj_call(name, kernel, x, gain, w3, layer, n_cols, extra_inputs, extra_specs, out_dtype, scratch,
               col_tile=COL_TILE, first_col_tile=0, transposed_tile=None, w_transposed=False):
    s_len, d = x.shape
    tm, tn = min(ROW_TILE, s_len), col_tile
    grid = (s_len // tm, pl.cdiv(n_cols, tn))
    if w_transposed:
        w_spec = pl.BlockSpec((None, tn, d), lambda i, j: (layer, first_col_tile + j, 0))
    else:
        w_spec = pl.BlockSpec((None, d, tn), lambda i, j: (layer, 0, first_col_tile + j))
    in_specs = [pl.BlockSpec((tm, d), lambda i, j: (i, 0)), pl.BlockSpec((1, d), lambda i, j: (0, 0)), w_spec]
    in_specs += list(extra_specs)
    if transposed_tile is None:
        out_shape = jax.ShapeDtypeStruct((s_len, n_cols), out_dtype)
        out_spec = pl.BlockSpec((tm, tn), lambda i, j: (i, j))
    else:
        out_shape = jax.ShapeDtypeStruct((s_len // transposed_tile, n_cols, transposed_tile), out_dtype)
        out_spec = pl.BlockSpec((tm // transposed_tile, tn, transposed_tile), lambda i, j: (i, j, 0))
    out_bytes = jnp.dtype(out_dtype).itemsize
    block_bytes = 2 * tm * d * 4 + tm * d * 2 + 2 * d * tn * 4 + d * tn * 2 + 2 * tm * tn * out_bytes + 3 * tm * tn * 4
    return pl.pallas_call(
        kernel,
        name=name,
        out_shape=out_shape,
        grid=grid,
        in_specs=in_specs,
        out_specs=out_spec,
        scratch_shapes=[pltpu.VMEM((tm, d), BF16)] + list(scratch),
        compiler_params=_params(("arbitrary", "arbitrary"), block_bytes),
    )(x, gain.reshape(1, d), w3, *extra_inputs)


def _conv_scratch(s_len, n_tiles):
    tm = min(ROW_TILE, s_len)
    return (pltpu.VMEM((n_tiles, V7X_SUBLANES, COL_TILE), F32), pltpu.VMEM((V7X_SUBLANES + tm, COL_TILE), F32))


def _mm_res_kernel(a_ref, w_ref, r_ref, o_ref, *w_scr):
    if w_scr:
        @pl.when(pl.program_id(0) == 0)
        def _():
            for r0 in range(0, w_ref.shape[0], COL_TILE):
                w_scr[0][r0:r0 + COL_TILE, :] = w_ref[r0:r0 + COL_TILE, :].astype(BF16)
        w = w_scr[0][...]
    else:
        w = w_ref[...]
    o_ref[...] = r_ref[...] + jnp.dot(a_ref[...], w, preferred_element_type=F32)


def _mm_res_call(a, w, res, layer=None):
    s_len, k = a.shape
    n = w.shape[-1]
    tm = min(OUT_ROW_TILE, s_len)
    once = pl.Buffered(1)
    if w.ndim == 3:
        w_spec = pl.BlockSpec((None, k, n), lambda i: (layer, 0, 0), pipeline_mode=once)
        scratch = [pltpu.VMEM((k, n), BF16)]
    else:
        w_spec = pl.BlockSpec((k, n), lambda i: (0, 0), pipeline_mode=once)
        scratch = []
    block_bytes = 2 * tm * k * 2 + k * n * w.dtype.itemsize + len(scratch) * k * n * 2 + 5 * tm * n * 4
    return pl.pallas_call(
        _mm_res_kernel,
        name="out_proj_residual",
        out_shape=jax.ShapeDtypeStruct((s_len, n), F32),
        grid=(s_len // tm,),
        in_specs=[pl.BlockSpec((tm, k), lambda i: (i, 0)), w_spec, pl.BlockSpec((tm, n), lambda i: (i, 0))],
        out_specs=pl.BlockSpec((tm, n), lambda i: (i, 0)),
        scratch_shapes=scratch,
        compiler_params=_params(("arbitrary",), block_bytes),
    )(a, w, res)


def _ffn_down_kernel(uv_ref, ug_ref, w_ref, r_ref, o_ref, g_scr):
    d_ff = w_ref.shape[0]
    mid = pl.cdiv(d_ff // 2, 2 * V7X_LANES) * 2 * V7X_LANES
    acc = r_ref[...]
    for k0, k1 in ((0, mid), (mid, d_ff)):
        for c0 in range(k0, k1, COL_TILE):
            cols = slice(c0, min(c0 + COL_TILE, k1))
            g_scr[:, cols] = (_silu(ug_ref[:, cols].astype(F32)) * uv_ref[:, cols].astype(F32)).astype(BF16)
        acc = acc + jnp.dot(g_scr[:, k0:k1], w_ref[k0:k1, :], preferred_element_type=F32)
    o_ref[...] = acc


def _ffn_down_call(u, w_down, layer, res):
    s_len = u.shape[0]
    _, d_ff, n = w_down.shape
    tm = min(DOWN_ROW_TILE, s_len)
    block_bytes = 4 * tm * d_ff * 2 + tm * d_ff * 2 + d_ff * n * 2 + 6 * tm * n * 4
    return pl.pallas_call(
        _ffn_down_kernel,
        name="ffn_down_residual",
        out_shape=jax.ShapeDtypeStruct((s_len, n), F32),
        grid=(s_len // tm,),
        in_specs=[
            pl.BlockSpec((tm, d_ff), lambda i: (i, 0)),
            pl.BlockSpec((tm, d_ff), lambda i: (i, 1)),
            pl.BlockSpec((None, d_ff, n), lambda i: (layer, 0, 0), pipeline_mode=pl.Buffered(1)),
            pl.BlockSpec((tm, n), lambda i: (i, 0)),
        ],
        out_specs=pl.BlockSpec((tm, n), lambda i: (i, 0)),
        scratch_shapes=[pltpu.VMEM((tm, d_ff), BF16)],
        compiler_params=_params(("arbitrary",), block_bytes),
    )(u, u, w_down, res)


def _attn_kernel(slopes_ref, qt_ref, k_ref, vt_ref, lq1_ref, lk1_ref, lq2_ref, lk2_ref, subln_ref, o_ref,
                 s_scr, kb_scr, bias_scr, m_scr, l_scr, acc_scr, *, tile, head_dim, lambda_init):
    h, qi = pl.program_id(0), pl.program_id(1)
    slope = slopes_ref[h]
    hd = head_dim

    @pl.when(qi == 0)
    def _():
        key_bias = slope * lax.broadcasted_iota(jnp.int32, (tile, hd), 0).astype(F32)
        lane = lax.broadcasted_iota(jnp.int32, (tile, hd), 1)
        part, parts = key_bias, []
        for _ in range(BIAS_PARTS):
            parts.append(part.astype(BF16).astype(F32))
            part = part - parts[-1]
        kb_cols = jnp.zeros((tile, hd), F32)
        for c in reversed(range(BIAS_PARTS)):
            kb_cols = jnp.where(lane == c, parts[c], kb_cols)
        kb_scr[...] = kb_cols.astype(BF16)
        key = lax.broadcasted_iota(jnp.int32, (tile, tile), 0)
        qry = lax.broadcasted_iota(jnp.int32, (tile, tile), 1)
        bias_scr[...] = jnp.where((key // CHUNK) <= (qry // CHUNK),
                                  -slope * (jnp.abs(key - qry) + key).astype(F32), -jnp.inf)

    ones_rows = jnp.where(lax.broadcasted_iota(jnp.int32, (hd, tile), 0) < BIAS_PARTS, 1.0, 0.0).astype(BF16)

    def issue_scores(kj, slot):
        k_blk = k_ref[pl.ds(pl.multiple_of(kj * tile, tile), tile), :]
        for m in range(2):
            rows = slice(m * hd, (m + 1) * hd)
            s_scr[slot, m] = jnp.dot(jnp.concatenate([k_blk[:, rows], kb_scr[...]], axis=1),
                                     jnp.concatenate([qt_ref[rows, :], ones_rows], axis=0),
                                     preferred_element_type=F32)

    issue_scores(qi, 1)
    issue_scores(0, 0)

    vt_diag = vt_ref[qi]
    for m in range(2):
        z = s_scr[1, m] + bias_scr[...]
        mx = jnp.max(z, axis=0, keepdims=True)
        p = jnp.exp2(z - mx)
        m_scr[m] = mx
        l_scr[m] = jnp.sum(p, axis=0, keepdims=True)
        acc_scr[m] = jnp.dot(vt_diag, p.astype(BF16), preferred_element_type=F32)

    q_pos = (lax.broadcasted_iota(jnp.int32, (1, tile), 1) + qi * tile).astype(F32)

    def kv_step(kj, slot, issue_next):
        if issue_next:
            issue_scores(jnp.minimum(kj + 1, qi - 1), 1 - slot)
        c = -slope * (q_pos - (kj * tile).astype(F32))
        vt_blk = vt_ref[kj]
        for m in range(2):
            z = s_scr[slot, m]
            m_old = m_scr[m]
            m_new = jnp.maximum(m_old, jnp.max(z, axis=0, keepdims=True) + c)
            p = jnp.exp2(z - (m_new - c))
            alpha = jnp.exp2(m_old - m_new)
            l_scr[m] = alpha * l_scr[m] + jnp.sum(p, axis=0, keepdims=True)
            acc_scr[m] = alpha * acc_scr[m] + jnp.dot(vt_blk, p.astype(BF16), preferred_element_type=F32)
            m_scr[m] = m_new

    def two_steps(t, carry):
        kv_step(2 * t, 0, True)
        kv_step(2 * t + 1, 1, True)
        return carry

    lax.fori_loop(0, qi // 2, two_steps, 0)

    @pl.when(lax.rem(qi, 2) == 1)
    def _():
        kv_step(qi - 1, 0, False)

    lam = (jnp.exp(jnp.sum(lq1_ref[...] * lk1_ref[...], axis=-1, keepdims=True))
           - jnp.exp(jnp.sum(lq2_ref[...] * lk2_ref[...], axis=-1, keepdims=True)) + lambda_init)
    o_t = acc_scr[0] / l_scr[0] - lam * (acc_scr[1] / l_scr[1])
    inv = lax.rsqrt(jnp.mean(o_t * o_t, axis=0, keepdims=True) + RMS_EPS)
    o_ref[...] = ((o_t * inv * subln_ref[...]) * (1.0 - lambda_init)).T.astype(o_ref.dtype)


def _attn_call(qt, k, vt, slopes, lq1, lk1, lq2, lk2, subln, *, n_heads, head_dim, lambda_init):
    n_tiles, _, tile = qt.shape
    s_len = k.shape[0]
    hw = 2 * head_dim
    vec = lambda a: a.reshape(1, -1)
    vec_spec = lambda n: pl.BlockSpec((1, n), lambda h, qi: (0, 0))
    block_bytes = 2 * (2 * s_len * hw * 2) + 4 * tile * hw * 2 + 2 * tile * hw * 4 + 5 * tile * tile * 4 \
        + 6 * tile * tile * 4
    return pl.pallas_call(
        functools.partial(_attn_kernel, tile=tile, head_dim=head_dim, lambda_init=lambda_init),
        name="diff_attention",
        out_shape=jax.ShapeDtypeStruct((s_len, n_heads * hw), BF16),
        grid=(n_heads, n_tiles),
        in_specs=[
            pl.BlockSpec(memory_space=pltpu.SMEM),
            pl.BlockSpec((None, hw, tile), lambda h, qi: (qi, h, 0)),
            pl.BlockSpec((s_len, hw), lambda h, qi: (0, h)),
            pl.BlockSpec((n_tiles, hw, tile), lambda h, qi: (0, h, 0)),
            vec_spec(head_dim), vec_spec(head_dim), vec_spec(head_dim), vec_spec(head_dim),
            pl.BlockSpec((hw, 1), lambda h, qi: (0, 0)),
        ],
        out_specs=pl.BlockSpec((tile, hw), lambda h, qi: (qi, h)),
        scratch_shapes=[
            pltpu.VMEM((2, 2, tile, tile), F32),
            pltpu.VMEM((tile, head_dim), BF16),
            pltpu.VMEM((tile, tile), F32),
            pltpu.VMEM((2, 1, tile), F32),
            pltpu.VMEM((2, 1, tile), F32),
            pltpu.VMEM((2, hw, tile), F32),
        ],
        compiler_params=_params(("arbitrary", "arbitrary"), block_bytes),
    )(slopes, qt, k, vt, vec(lq1), vec(lk1), vec(lq2), vec(lk2), subln.reshape(-1, 1))


def _gates_kernel(ba_ref, alog_ref, dtb_ref, beta_ref, gc_ref):
    rows = ba_ref.shape[0]
    beta_ref[...] = jax.nn.sigmoid(ba_ref[...])
    g = -jnp.exp(alog_ref[...]) * jax.nn.softplus(ba_ref[...] + dtb_ref[...])
    r = lax.broadcasted_iota(jnp.int32, (rows, rows), 0)
    c = lax.broadcasted_iota(jnp.int32, (rows, rows), 1)
    tri = jnp.where(r // CHUNK == c // CHUNK, jnp.where(c <= r, 1.0, 0.0), 0.0).astype(F32)
    gc_ref[...] = jnp.dot(tri, g, preferred_element_type=F32, precision=lax.Precision.HIGHEST)


def _gates_call(ba, a_log_row, dt_bias_row):
    s_len, width = ba.shape
    tm = min(GATE_ROWS, s_len)
    out = jax.ShapeDtypeStruct((s_len, width), F32)
    row_spec = pl.BlockSpec((tm, width), lambda i: (i, 0))
    vec_spec = pl.BlockSpec((1, width), lambda i: (0, 0))
    return pl.pallas_call(
        _gates_kernel,
        name="gdn_gates",
        out_shape=(out, out),
        grid=(s_len // tm,),
        in_specs=[row_spec, vec_spec, vec_spec],
        out_specs=(row_spec, row_spec),
        compiler_params=_params(("arbitrary",), 8 * tm * 2 * V7X_LANES * 4 + 4 * tm * tm * 4),
    )(ba, a_log_row, dt_bias_row)


def _split_bf16(x):
    hi = x.astype(BF16)
    return hi, (x - hi.astype(F32)).astype(BF16)


def _delta_kernel(q_ref, k_ref, v_ref, z_ref, beta_ref, gc_ref, gct_ref, ng_ref, o_ref, state_scr, *,
                  n_pairs, n_chunks, head_dim):
    hd = head_dim
    pairs = range(n_pairs)

    @pl.when(pl.program_id(1) == 0)
    def _():
        state_scr[...] = jnp.zeros(state_scr.shape, F32)

    ri = lax.broadcasted_iota(jnp.int32, (CHUNK, 2 * CHUNK), 0)
    lane = lax.broadcasted_iota(jnp.int32, (CHUNK, 2 * CHUNK), 1)
    left = lane < CHUNK
    ci = jnp.where(left, lane, lane - CHUNK)
    tril = ci <= ri
    strict = ci < ri
    eye = jnp.where(ci == ri, 1.0, 0.0).astype(F32)
    state_left = lax.broadcasted_iota(jnp.int32, (hd, 2 * hd), 1) < hd

    def block_diag(y, keep_left):
        zero = jnp.zeros_like(y)
        return jnp.concatenate([jnp.where(keep_left, y, zero), jnp.where(keep_left, zero, y)], axis=0)

    def pair_matmul(xh, xl, yh, yl):
        bh, bl = block_diag(yh, left), block_diag(yl, left)
        rhs = jnp.concatenate([jnp.concatenate([bh, bl], axis=1),
                               jnp.concatenate([bh, jnp.zeros_like(bl)], axis=1)], axis=0)
        out = jnp.dot(jnp.concatenate([xh, xl], axis=1), rhs, preferred_element_type=F32)
        return out[:, :2 * CHUNK] + out[:, 2 * CHUNK:]

    def chunk_step(c, carry):
        rows = pl.ds(pl.multiple_of(c * CHUNK, CHUNK), CHUNK)
        beta = beta_ref[rows, :]
        g_col = gc_ref[rows, :]
        g_rows = gct_ref[0, c]
        b_rows = gct_ref[1, c]
        lanes = lambda a, h: jnp.broadcast_to(a[:, h:h + 1], (CHUNK, hd))

        k = [k_ref[rows, p * hd:(p + 1) * hd] for p in pairs]
        q = [q_ref[rows, p * hd:(p + 1) * hd] for p in pairs]
        kq = [lax.dot_general(jnp.concatenate([k[p], q[p]], axis=0), jnp.concatenate([k[p], k[p]], axis=0), _NT,
                              preferred_element_type=F32) for p in pairs]
        g_row = [g_rows[p:p + 1, :] for p in pairs]
        lmat, qk_decay, q_dec, k_dec = [], [], [], []
        for p in pairs:
            g_a, g_b = lanes(g_col, 2 * p), lanes(g_col, 2 * p + 1)
            b_pk = jnp.where(left, lanes(beta, 2 * p), lanes(beta, 2 * p + 1))
            decay = jnp.exp(jnp.where(tril, jnp.where(left, g_a, g_b) - g_row[p], -jnp.inf))
            lmat.append(jnp.where(strict, kq[p][:CHUNK] * b_pk * decay, 0.0))
            qk_decay.append((kq[p][CHUNK:] * decay).astype(BF16))
            qf, kf = q[p].astype(F32), k[p].astype(F32)
            q_dec.append(jnp.concatenate([qf * jnp.exp(g_a), qf * jnp.exp(g_b)], axis=1).astype(BF16))
            ge_a, ge_b = g_row[p][:, CHUNK - 1:CHUNK], g_row[p][:, 2 * CHUNK - 1:2 * CHUNK]
            k_dec.append(((kf * jnp.exp(ge_a - g_a)).astype(BF16), (kf * jnp.exp(ge_b - g_b)).astype(BF16)))
        lsplit = [_split_bf16(lmat[p]) for p in pairs]
        x = [pair_matmul(*lsplit[p], *lsplit[p]) for p in pairs]
        acc = [eye - lmat[p] for p in pairs]
        power = 2
        while power * 2 < CHUNK:
            for p in pairs:
                sh, sl = _split_bf16(jnp.concatenate([acc[p], x[p]], axis=0))
                both = pair_matmul(sh, sl, sh[CHUNK:], sl[CHUNK:])
                acc[p] = acc[p] + both[:CHUNK]
                x[p] = both[CHUNK:]
            power *= 2
        tmat = [acc[p] + pair_matmul(*_split_bf16(acc[p]), *_split_bf16(x[p])) for p in pairs]
        zb = jnp.zeros((CHUNK, hd), BF16)
        uw = []
        for p in pairs:
            b_row = b_rows[p:p + 1, :]
            t_scaled = jnp.concatenate([tmat[p] * b_row, tmat[p] * (b_row * jnp.exp(g_row[p]))], axis=1)
            v_pk = v_ref[rows, 2 * p * hd:2 * (p + 1) * hd]
            rhs = jnp.concatenate([jnp.concatenate([v_pk[:, :hd], zb, zb, zb], axis=1),
                                   jnp.concatenate([zb, v_pk[:, hd:], zb, zb], axis=1),
                                   jnp.concatenate([zb, zb, k[p], zb], axis=1),
                                   jnp.concatenate([zb, zb, zb, k[p]], axis=1)], axis=0)
            uw.append(jnp.dot(t_scaled.astype(BF16), rhs, preferred_element_type=F32))
        s_old = [state_scr[p] for p in pairs]
        wq = []
        for p in pairs:
            wq.append(jnp.dot(jnp.concatenate([uw[p][:, 2 * hd:].astype(BF16), q_dec[p]], axis=0),
                              block_diag(s_old[p].astype(BF16), state_left), preferred_element_type=F32))
        o_pk = []
        for p in pairs:
            vn = (uw[p][:, :2 * hd] - wq[p][:CHUNK]).astype(BF16)
            vn_diag = jnp.concatenate([jnp.concatenate([vn[:, :hd], zb], axis=1),
                                       jnp.concatenate([zb, vn[:, hd:]], axis=1)], axis=0)
            o_pk.append(wq[p][CHUNK:] + jnp.dot(qk_decay[p], vn_diag, preferred_element_type=F32))
            ge_a, ge_b = g_row[p][:, CHUNK - 1:CHUNK], g_row[p][:, 2 * CHUNK - 1:2 * CHUNK]
            ds_a = lax.dot_general(k_dec[p][0], vn[:, :hd], _TN, preferred_element_type=F32)
            ds_b = lax.dot_general(k_dec[p][1], vn[:, hd:], _TN, preferred_element_type=F32)
            state_scr[p] = jnp.concatenate([s_old[p][:, :hd] * jnp.exp(ge_a) + ds_a,
                                            s_old[p][:, hd:] * jnp.exp(ge_b) + ds_b], axis=1)
        for p in pairs:
            wide_cols = slice(2 * p * hd, 2 * (p + 1) * hd)
            z_pk = z_ref[rows, wide_cols].astype(F32)
            gated = [_rms_rows(o_pk[p][:, h * hd:(h + 1) * hd], ng_ref[...]) * _silu(z_pk[:, h * hd:(h + 1) * hd])
                     for h in range(2)]
            o_ref[rows, wide_cols] = jnp.concatenate(gated, axis=1).astype(o_ref.dtype)
        return carry

    lax.fori_loop(0, n_chunks, chunk_step, 0)


def _delta_call(qkv, z, beta_g, gc_g, gct_g, norm_g, *, n_qk_heads, head_dim):
    s_len = qkv.shape[0]
    n_groups = n_qk_heads // DELTA_PAIRS
    heads = 2 * DELTA_PAIRS
    rows = DELTA_CHUNKS * CHUNK
    qk_w, v_w = DELTA_PAIRS * head_dim, heads * head_dim
    key_dim, value_dim = n_qk_heads * head_dim, 2 * n_qk_heads * head_dim
    k_off, v_off = key_dim // qk_w, 2 * key_dim // v_w
    block_bytes = 2 * rows * (2 * qk_w + 3 * v_w) * 2 + 6 * rows * V7X_LANES * 4 + heads * head_dim * head_dim * 4
    return pl.pallas_call(
        functools.partial(_delta_kernel, n_pairs=DELTA_PAIRS, n_chunks=DELTA_CHUNKS, head_dim=head_dim),
        name="gdn_delta_rule",
        out_shape=jax.ShapeDtypeStruct((s_len, value_dim), BF16),
        grid=(n_groups, s_len // rows),
        in_specs=[
            pl.BlockSpec((rows, qk_w), lambda g, b: (b, g)),
            pl.BlockSpec((rows, qk_w), lambda g, b: (b, k_off + g)),
            pl.BlockSpec((rows, v_w), lambda g, b: (b, v_off + g)),
            pl.BlockSpec((rows, v_w), lambda g, b: (b, g)),
            pl.BlockSpec((None, rows, heads), lambda g, b: (g, b, 0)),
            pl.BlockSpec((None, rows, heads), lambda g, b: (g, b, 0)),
            pl.BlockSpec((None, 2, DELTA_CHUNKS, DELTA_PAIRS, 2 * CHUNK), lambda g, b: (g, 0, b, 0, 0)),
            pl.BlockSpec((1, head_dim), lambda g, b: (0, 0)),
        ],
        out_specs=pl.BlockSpec((rows, v_w), lambda g, b: (b, g)),
        scratch_shapes=[pltpu.VMEM((DELTA_PAIRS, head_dim, 2 * head_dim), F32)],
        compiler_params=_params(("arbitrary", "arbitrary"), block_bytes),
    )(qkv, qkv, qkv, z, beta_g, gc_g, gct_g, norm_g.reshape(1, head_dim))


def _diff_attention_layer(x, gain, w_qkv, layer, q_norm, k_norm, lq1, lk1, lq2, lk2, subln, w_o, lambda_init):
    s_len = x.shape[0]
    head_dim = q_norm.shape[0]
    n_heads = w_o.shape[1] // subln.shape[0]
    qk_dim = 2 * n_heads * head_dim
    v_dim = w_qkv.shape[2] - 2 * qk_dim
    tile = min(ATTN_TILE, s_len)
    vec_spec = pl.BlockSpec((1, head_dim), lambda i, j: (0, 0))
    w_qkv_bf16 = w_qkv.astype(BF16)

    def proj(name, n_cols, first_col, head_gain, scale, transpose):
        head_norm = ((head_gain.reshape(1, -1),), (vec_spec,)) if head_gain is not None else ((), ())
        return _proj_call(
            name, functools.partial(_head_proj_kernel, head_dim=head_dim, scale=scale, transpose=transpose, tile=tile,
                                    w_transposed=False, piece_rows=ROW_SUBTILE),
            x, gain, w_qkv_bf16, layer, n_cols, *head_norm, BF16, (),
            first_col_tile=first_col // COL_TILE, transposed_tile=tile if transpose else None)

    qt = proj("attn_q_proj", qk_dim, 0, q_norm, head_dim ** -0.5 * LOG2_E, True)
    k = proj("attn_k_proj", qk_dim, qk_dim, k_norm, 1.0, False)
    vt = proj("attn_v_proj", v_dim, 2 * qk_dim, None, 1.0, True)
    slopes = jnp.exp2(-8.0 * jnp.arange(1, n_heads + 1, dtype=F32) / n_heads) * LOG2_E
    o = _attn_call(qt, k, vt, slopes, lq1, lk1, lq2, lk2, subln, n_heads=n_heads, head_dim=head_dim,
                   lambda_init=lambda_init)
    return _mm_res_call(o, w_o, x, layer)


def _gated_deltanet_layer(x, gain, w_in, layer, conv_w, a_log, dt_bias, norm_g, w_out):
    s_len, d = x.shape
    head_dim = norm_g.shape[0]
    n_v_heads = a_log.shape[0]
    n_qk_heads = n_v_heads // 2
    key_dim, value_dim = n_qk_heads * head_dim, n_v_heads * head_dim
    conv_dim = 2 * key_dim + value_dim
    main_cols = conv_dim + value_dim
    n_conv_tiles = conv_dim // COL_TILE
    w_main = w_in[:, :, :main_cols].astype(BF16)
    w_in_t = jnp.swapaxes(w_in, 1, 2)
    cw_spec = pl.BlockSpec((None, conv_w.shape[1], COL_TILE), lambda i, j: (layer, 0, j))
    qkv = _proj_call(
        "gdn_in_proj", functools.partial(_gdn_conv_kernel, n_q_tiles=key_dim // COL_TILE,
                                         n_qk_tiles=2 * key_dim // COL_TILE, head_dim=head_dim,
                                         q_scale=head_dim ** -0.5),
        x, gain, w_main, layer, conv_dim, (conv_w,), (cw_spec,), BF16, _conv_scratch(s_len, n_conv_tiles))
    z = _proj_call(
        "gdn_z_proj", functools.partial(_head_proj_kernel, head_dim=head_dim, scale=1.0, transpose=False, tile=None,
                                        w_transposed=False, piece_rows=min(ROW_TILE, s_len)),
        x, gain, w_main, layer, value_dim, (), (), BF16, (), first_col_tile=n_conv_tiles)

    gate_cols = 2 * n_v_heads
    assert main_cols % gate_cols == 0
    ba = _proj_call("gdn_gate_proj", functools.partial(_plain_proj_kernel, w_transposed=True), x, gain, w_in_t, layer,
                    gate_cols, (), (), F32, (), col_tile=gate_cols, first_col_tile=main_cols // gate_cols,
                    w_transposed=True)
    on_a_lanes = lambda v: jnp.pad(v, (n_v_heads, 0)).reshape(1, -1)
    beta, gc = _gates_call(ba, on_a_lanes(a_log), on_a_lanes(dt_bias))
    beta, gc = beta[:, :n_v_heads], gc[:, n_v_heads:2 * n_v_heads]

    heads = 2 * DELTA_PAIRS
    n_groups = n_v_heads // heads
    beta_g = beta.reshape(s_len, n_groups, heads).transpose(1, 0, 2)
    gc_g = gc.reshape(s_len, n_groups, heads).transpose(1, 0, 2)
    rows_of = lambda a: a.reshape(s_len // CHUNK, CHUNK, n_groups, DELTA_PAIRS, 2).transpose(
        2, 0, 3, 4, 1).reshape(n_groups, s_len // CHUNK, DELTA_PAIRS, 2 * CHUNK)
    gct_g = jnp.stack([rows_of(gc), rows_of(beta)], axis=1)
    o = _delta_call(qkv, z, beta_g, gc_g, gct_g, norm_g, n_qk_heads=n_qk_heads, head_dim=head_dim)
    return _mm_res_call(o, w_out[layer].astype(BF16), x)


def _conv_ffn_layer(x, gain, w_up, conv_w, conv_b, w_down, layer):
    n_cols = w_up.shape[2]
    n_tiles = pl.cdiv(n_cols, COL_TILE)
    tile_spec = lambda rows: pl.BlockSpec((None, rows, COL_TILE), lambda i, j: (layer, 0, j))
    u = _proj_call("ffn_up_proj", _ffn_up_kernel, x, gain, w_up, layer, n_cols, (conv_w, conv_b[:, None, :]),
                   (tile_spec(conv_w.shape[1]), tile_spec(1)), BF16, _conv_scratch(x.shape[0], n_tiles))
    return _ffn_down_call(u, w_down.astype(BF16), layer, x)


def kernel(x, mixer_norm, ffn_norm, diff_w_qkv, diff_q_norm, diff_k_norm, diff_lambda_q1, diff_lambda_k1,
           diff_lambda_q2, diff_lambda_k2, diff_subln, diff_w_o, gdn_w_in, gdn_conv_w, gdn_A_log, gdn_dt_bias,
           gdn_norm, gdn_w_out, ffn_w_up, ffn_conv_w, ffn_conv_b, ffn_w_down):
    b_sz, s_len, d = x.shape
    depth = mixer_norm.shape[0]
    outs = []
    for b in range(b_sz):
        xb = x.reshape(s_len, d) if b_sz == 1 else x[b]
        for i in range(depth):
            j = i // 2
            if i % 2 == 0:
                lambda_init = 0.8 - 0.6 * math.exp(-0.3 * i)
                xb = _diff_attention_layer(xb, mixer_norm[i], diff_w_qkv, j, diff_q_norm[j], diff_k_norm[j],
                                           diff_lambda_q1[j], diff_lambda_k1[j], diff_lambda_q2[j],
                                           diff_lambda_k2[j], diff_subln[j], diff_w_o, lambda_init)
            else:
                xb = _gated_deltanet_layer(xb, mixer_norm[i], gdn_w_in, j, gdn_conv_w, gdn_A_log[j], gdn_dt_bias[j],
                                           gdn_norm[j], gdn_w_out)
            xb = _conv_ffn_layer(xb, ffn_norm[i], ffn_w_up, ffn_conv_w, ffn_conv_b, ffn_w_down, i)
        outs.append(xb)
    return outs[0].reshape(x.shape) if b_sz == 1 else jnp.stack(outs, axis=0)
```

```python
import functools
import math

import jax
import jax.numpy as jnp
from jax import lax
from jax.experimental import pallas as pl
from jax.experimental.pallas import tpu as pltpu

F32 = jnp.float32
BF16 = jnp.bfloat16

RMS_EPS = 1e-6
LOG2_E = math.log2(math.e)
BIAS_PARTS = 3
CHUNK = 64
V7X_VMEM_BYTES = 64 * 1024 * 1024
V7X_LANES = 128
V7X_SUBLANES = 8
COMPILER_TEMP_BYTES = 12 * 1024 * 1024

ROW_TILE = 1024
ROW_SUBTILE = 256
OUT_ROW_TILE = 512
DOWN_ROW_TILE = 256
COL_TILE = 512
ATTN_TILE = 512
DELTA_PAIRS = 16
DELTA_CHUNKS = 8
GATE_ROWS = 256

_NT = (((1,), (1,)), ((), ()))
_TN = (((0,), (0,)), ((), ()))


def _params(semantics, block_bytes):
    limit = min(int(block_bytes) + COMPILER_TEMP_BYTES, V7X_VMEM_BYTES - (4 << 20))
    return pltpu.CompilerParams(dimension_semantics=semantics, vmem_limit_bytes=limit)


def _rms_rows(x, gain):
    return x * lax.rsqrt(jnp.mean(x * x, axis=-1, keepdims=True) + RMS_EPS) * gain


def _silu(x):
    return x * jax.nn.sigmoid(x)


def _matmul(h, w_bf16, w_transposed):
    if w_transposed:
        return lax.dot_general(h, w_bf16, _NT, preferred_element_type=F32)
    return jnp.dot(h, w_bf16, preferred_element_type=F32)


def _normalize_once(x_ref, g_ref, h_scr):
    @pl.when(pl.program_id(1) == 0)
    def _():
        h_scr[...] = _rms_rows(x_ref[...], g_ref[...]).astype(BF16)


def _pipelined_rows(h_scr, w_bf16, w_transposed, piece_rows, epilogue):
    n_sub = h_scr.shape[0] // piece_rows
    piece = lambda r: _matmul(h_scr[r * piece_rows:(r + 1) * piece_rows, :], w_bf16, w_transposed)
    u_prev = piece(0)
    for r in range(1, n_sub):
        u_next = piece(r)
        epilogue(r - 1, u_prev)
        u_prev = u_next
    epilogue(n_sub - 1, u_prev)


def _head_proj_kernel(x_ref, g_ref, w_ref, *refs, head_dim, scale, transpose, tile, w_transposed, piece_rows):
    normalize = len(refs) == 3
    gain_ref = refs[0] if normalize else None
    o_ref, h_scr = refs[-2:]
    _normalize_once(x_ref, g_ref, h_scr)

    def epilogue(r, acc):
        if normalize:
            acc = jnp.concatenate([_rms_rows(acc[:, c:c + head_dim], gain_ref[...]) * scale
                                   for c in range(0, acc.shape[1], head_dim)], axis=1)
        r0 = r * piece_rows
        if transpose:
            o_ref[r0 // tile, :, r0 % tile:r0 % tile + piece_rows] = acc.T.astype(o_ref.dtype)
        else:
            o_ref[r0:r0 + piece_rows, :] = acc.astype(o_ref.dtype)

    _pipelined_rows(h_scr, w_ref[...].astype(BF16), w_transposed, piece_rows, epilogue)


def _plain_proj_kernel(x_ref, g_ref, w_ref, o_ref, h_scr, *, w_transposed):
    _normalize_once(x_ref, g_ref, h_scr)
    o_ref[...] = _matmul(h_scr[...], w_ref[...].astype(BF16), w_transposed).astype(o_ref.dtype)


def _conv_proj_steps(h_scr, w_ref, cw_ref, o_ref, carry_scr, u_scr, *, piece_rows, piece_cols, w_transposed, finish):
    i, j = pl.program_id(0), pl.program_id(1)
    rows, kw = h_scr.shape[0], cw_ref.shape[0]
    head = V7X_SUBLANES

    @pl.when(jnp.logical_and(i == 0, j == 0))
    def _():
        carry_scr[...] = jnp.zeros(carry_scr.shape, F32)

    u_scr[0:head, :] = carry_scr[j]
    w_bf16 = w_ref[...].astype(BF16)
    pieces = [(r0, c0) for r0 in range(0, rows, piece_rows) for c0 in range(0, o_ref.shape[1], piece_cols)]

    def matmul_piece(r0, c0):
        cols = slice(c0, c0 + piece_cols)
        w_piece = w_bf16[cols, :] if w_transposed else w_bf16[:, cols]
        u_scr[head + r0:head + r0 + piece_rows, cols] = _matmul(h_scr[r0:r0 + piece_rows, :], w_piece, w_transposed)

    def epilogue_piece(p0, c0):
        cols = slice(c0, c0 + piece_cols)
        for r0 in range(p0, p0 + piece_rows, ROW_SUBTILE):
            y = u_scr[head + r0:head + r0 + ROW_SUBTILE, cols] * cw_ref[kw - 1:kw, cols]
            for s in range(1, kw):
                y = y + u_scr[head + r0 - s:head + r0 - s + ROW_SUBTILE, cols] * cw_ref[kw - 1 - s:kw - s, cols]
            o_ref[r0:r0 + ROW_SUBTILE, cols] = finish(y, j, cols).astype(o_ref.dtype)

    matmul_piece(*pieces[0])
    for prev, cur in zip(pieces[:-1], pieces[1:]):
        matmul_piece(*cur)
        epilogue_piece(*prev)
    epilogue_piece(*pieces[-1])
    carry_scr[j] = u_scr[rows:rows + head, :]


def _ffn_up_kernel(x_ref, g_ref, w_ref, cw_ref, cb_ref, o_ref, h_scr, carry_scr, u_scr):
    _normalize_once(x_ref, g_ref, h_scr)
    _conv_proj_steps(h_scr, w_ref, cw_ref, o_ref, carry_scr, u_scr, piece_rows=h_scr.shape[0] // 2,
                     piece_cols=o_ref.shape[1], w_transposed=False,
                     finish=lambda y, jj, cols: y + cb_ref[:, cols])


def _gdn_conv_kernel(x_ref, g_ref, w_ref, cw_ref, o_ref, h_scr, carry_scr, u_scr, *, n_q_tiles, n_qk_tiles, head_dim,
                     q_scale):
    _normalize_once(x_ref, g_ref, h_scr)

    def finish(y, jj, cols):
        y = _silu(y)
        l2_scale = jnp.where(jj < n_q_tiles, q_scale, 1.0)
        heads = []
        for c in range(0, y.shape[1], head_dim):
            blk = y[:, c:c + head_dim]
            inv = lax.rsqrt(jnp.sum(blk * blk, axis=-1, keepdims=True) + RMS_EPS) * l2_scale
            heads.append(blk * jnp.where(jj < n_qk_tiles, inv, 1.0))
        return jnp.concatenate(heads, axis=1)

    _conv_proj_steps(h_scr, w_ref, cw_ref, o_ref, carry_scr, u_scr, piece_rows=ROW_SUBTILE,
                     piece_cols=o_ref.shape[1], w_transposed=True, finish=finish)


def _proj_call(name, kernel, x, gain, w3, layer, n_cols, extra_inputs, extra_specs, out_dtype, scratch,
               col_tile=COL_TILE, first_col_tile=0, transposed_tile=None, w_transposed=False):
    s_len, d = x.shape
    tm, tn = min(ROW_TILE, s_len), col_tile
    grid = (s_len // tm, pl.cdiv(n_cols, tn))
    if w_transposed:
        w_spec = pl.BlockSpec((None, tn, d), lambda i, j: (layer, first_col_tile + j, 0))
    else:
        w_spec = pl.BlockSpec((None, d, tn), lambda i, j: (layer, 0, first_col_tile + j))
    in_specs = [pl.BlockSpec((tm, d), lambda i, j: (i, 0)), pl.BlockSpec((1, d), lambda i, j: (0, 0)), w_spec]
    in_specs += list(extra_specs)
    if transposed_tile is None:
        out_shape = jax.ShapeDtypeStruct((s_len, n_cols), out_dtype)
        out_spec = pl.BlockSpec((tm, tn), lambda i, j: (i, j))
    else:
        out_shape = jax.ShapeDtypeStruct((s_len // transposed_tile, n_cols, transposed_tile), out_dtype)
        out_spec = pl.BlockSpec((tm // transposed_tile, tn, transposed_tile), lambda i, j: (i, j, 0))
    out_bytes = jnp.dtype(out_dtype).itemsize
    block_bytes = 2 * tm * d * 4 + tm * d * 2 + 2 * d * tn * 4 + d * tn * 2 + 2 * tm * tn * out_bytes + 3 * tm * tn * 4
    return pl.pallas_call(
        kernel,
        name=name,
        out_shape=out_shape,
        grid=grid,
        in_specs=in_specs,
        out_specs=out_spec,
        scratch_shapes=[pltpu.VMEM((tm, d), BF16)] + list(scratch),
        compiler_params=_params(("arbitrary", "arbitrary"), block_bytes),
    )(x, gain.reshape(1, d), w3, *extra_inputs)


def _conv_scratch(s_len, n_tiles):
    tm = min(ROW_TILE, s_len)
    return (pltpu.VMEM((n_tiles, V7X_SUBLANES, COL_TILE), F32), pltpu.VMEM((V7X_SUBLANES + tm, COL_TILE), F32))


def _mm_res_kernel(a_ref, w_ref, r_ref, o_ref, *w_scr):
    if w_scr:
        @pl.when(pl.program_id(0) == 0)
        def _():
            for r0 in range(0, w_ref.shape[0], COL_TILE):
                w_scr[0][r0:r0 + COL_TILE, :] = w_ref[r0:r0 + COL_TILE, :].astype(BF16)
        w = w_scr[0][...]
    else:
        w = w_ref[...]
    o_ref[...] = r_ref[...] + jnp.dot(a_ref[...], w, preferred_element_type=F32)


def _mm_res_call(a, w, res, layer=None):
    s_len, k = a.shape
    n = w.shape[-1]
    tm = min(OUT_ROW_TILE, s_len)
    once = pl.Buffered(1)
    if w.ndim == 3:
        w_spec = pl.BlockSpec((None, k, n), lambda i: (layer, 0, 0), pipeline_mode=once)
        scratch = [pltpu.VMEM((k, n), BF16)]
    else:
        w_spec = pl.BlockSpec((k, n), lambda i: (0, 0), pipeline_mode=once)
        scratch = []
    block_bytes = 2 * tm * k * 2 + k * n * w.dtype.itemsize + len(scratch) * k * n * 2 + 5 * tm * n * 4
    return pl.pallas_call(
        _mm_res_kernel,
        name="out_proj_residual",
        out_shape=jax.ShapeDtypeStruct((s_len, n), F32),
        grid=(s_len // tm,),
        in_specs=[pl.BlockSpec((tm, k), lambda i: (i, 0)), w_spec, pl.BlockSpec((tm, n), lambda i: (i, 0))],
        out_specs=pl.BlockSpec((tm, n), lambda i: (i, 0)),
        scratch_shapes=scratch,
        compiler_params=_params(("arbitrary",), block_bytes),
    )(a, w, res)


def _ffn_down_kernel(uv_ref, ug_ref, w_ref, r_ref, o_ref, g_scr):
    d_ff = w_ref.shape[0]
    mid = pl.cdiv(d_ff // 2, 2 * V7X_LANES) * 2 * V7X_LANES
    acc = r_ref[...]
    for k0, k1 in ((0, mid), (mid, d_ff)):
        for c0 in range(k0, k1, COL_TILE):
            cols = slice(c0, min(c0 + COL_TILE, k1))
            g_scr[:, cols] = (_silu(ug_ref[:, cols].astype(F32)) * uv_ref[:, cols].astype(F32)).astype(BF16)
        acc = acc + jnp.dot(g_scr[:, k0:k1], w_ref[k0:k1, :], preferred_element_type=F32)
    o_ref[...] = acc


def _ffn_down_call(u, w_down, layer, res):
    s_len = u.shape[0]
    _, d_ff, n = w_down.shape
    tm = min(DOWN_ROW_TILE, s_len)
    block_bytes = 4 * tm * d_ff * 2 + tm * d_ff * 2 + d_ff * n * 2 + 6 * tm * n * 4
    return pl.pallas_call(
        _ffn_down_kernel,
        name="ffn_down_residual",
        out_shape=jax.ShapeDtypeStruct((s_len, n), F32),
        grid=(s_len // tm,),
        in_specs=[
            pl.BlockSpec((tm, d_ff), lambda i: (i, 0)),
            pl.BlockSpec((tm, d_ff), lambda i: (i, 1)),
            pl.BlockSpec((None, d_ff, n), lambda i: (layer, 0, 0), pipeline_mode=pl.Buffered(1)),
            pl.BlockSpec((tm, n), lambda i: (i, 0)),
        ],
        out_specs=pl.BlockSpec((tm, n), lambda i: (i, 0)),
        scratch_shapes=[pltpu.VMEM((tm, d_ff), BF16)],
        compiler_params=_params(("arbitrary",), block_bytes),
    )(u, u, w_down, res)


def _attn_kernel(slopes_ref, qt_ref, k_ref, vt_ref, lq1_ref, lk1_ref, lq2_ref, lk2_ref, subln_ref, o_ref,
                 s_scr, kb_scr, bias_scr, m_scr, l_scr, acc_scr, *, tile, head_dim, lambda_init):
    h, qi = pl.program_id(0), pl.program_id(1)
    slope = slopes_ref[h]
    hd = head_dim

    @pl.when(qi == 0)
    def _():
        key_bias = slope * lax.broadcasted_iota(jnp.int32, (tile, hd), 0).astype(F32)
        lane = lax.broadcasted_iota(jnp.int32, (tile, hd), 1)
        part, parts = key_bias, []
        for _ in range(BIAS_PARTS):
            parts.append(part.astype(BF16).astype(F32))
            part = part - parts[-1]
        kb_cols = jnp.zeros((tile, hd), F32)
        for c in reversed(range(BIAS_PARTS)):
            kb_cols = jnp.where(lane == c, parts[c], kb_cols)
        kb_scr[...] = kb_cols.astype(BF16)
        key = lax.broadcasted_iota(jnp.int32, (tile, tile), 0)
        qry = lax.broadcasted_iota(jnp.int32, (tile, tile), 1)
        bias_scr[...] = jnp.where((key // CHUNK) <= (qry // CHUNK),
                                  -slope * (jnp.abs(key - qry) + key).astype(F32), -jnp.inf)

    ones_rows = jnp.where(lax.broadcasted_iota(jnp.int32, (hd, tile), 0) < BIAS_PARTS, 1.0, 0.0).astype(BF16)

    def issue_scores(kj, slot):
        k_blk = k_ref[pl.ds(pl.multiple_of(kj * tile, tile), tile), :]
        for m in range(2):
            rows = slice(m * hd, (m + 1) * hd)
            s_scr[slot, m] = jnp.dot(jnp.concatenate([k_blk[:, rows], kb_scr[...]], axis=1),
                                     jnp.concatenate([qt_ref[rows, :], ones_rows], axis=0),
                                     preferred_element_type=F32)

    issue_scores(qi, 1)
    issue_scores(0, 0)

    vt_diag = vt_ref[qi]
    for m in range(2):
        z = s_scr[1, m] + bias_scr[...]
        mx = jnp.max(z, axis=0, keepdims=True)
        p = jnp.exp2(z - mx)
        m_scr[m] = mx
        l_scr[m] = jnp.sum(p, axis=0, keepdims=True)
        acc_scr[m] = jnp.dot(vt_diag, p.astype(BF16), preferred_element_type=F32)

    q_pos = (lax.broadcasted_iota(jnp.int32, (1, tile), 1) + qi * tile).astype(F32)

    def kv_step(kj, slot, issue_next):
        if issue_next:
            issue_scores(jnp.minimum(kj + 1, qi - 1), 1 - slot)
        c = -slope * (q_pos - (kj * tile).astype(F32))
        vt_blk = vt_ref[kj]
        for m in range(2):
            z = s_scr[slot, m]
            m_old = m_scr[m]
            m_new = jnp.maximum(m_old, jnp.max(z, axis=0, keepdims=True) + c)
            p = jnp.exp2(z - (m_new - c))
            alpha = jnp.exp2(m_old - m_new)
            l_scr[m] = alpha * l_scr[m] + jnp.sum(p, axis=0, keepdims=True)
            acc_scr[m] = alpha * acc_scr[m] + jnp.dot(vt_blk, p.astype(BF16), preferred_element_type=F32)
            m_scr[m] = m_new

    def two_steps(t, carry):
        kv_step(2 * t, 0, True)
        kv_step(2 * t + 1, 1, True)
        return carry

    lax.fori_loop(0, qi // 2, two_steps, 0)

    @pl.when(lax.rem(qi, 2) == 1)
    def _():
        kv_step(qi - 1, 0, False)

    lam = (jnp.exp(jnp.sum(lq1_ref[...] * lk1_ref[...], axis=-1, keepdims=True))
           - jnp.exp(jnp.sum(lq2_ref[...] * lk2_ref[...], axis=-1, keepdims=True)) + lambda_init)
    o_t = acc_scr[0] / l_scr[0] - lam * (acc_scr[1] / l_scr[1])
    inv = lax.rsqrt(jnp.mean(o_t * o_t, axis=0, keepdims=True) + RMS_EPS)
    o_ref[...] = ((o_t * inv * subln_ref[...]) * (1.0 - lambda_init)).T.astype(o_ref.dtype)


def _attn_call(qt, k, vt, slopes, lq1, lk1, lq2, lk2, subln, *, n_heads, head_dim, lambda_init):
    n_tiles, _, tile = qt.shape
    s_len = k.shape[0]
    hw = 2 * head_dim
    vec = lambda a: a.reshape(1, -1)
    vec_spec = lambda n: pl.BlockSpec((1, n), lambda h, qi: (0, 0))
    block_bytes = 2 * (2 * s_len * hw * 2) + 4 * tile * hw * 2 + 2 * tile * hw * 4 + 5 * tile * tile * 4 \
        + 6 * tile * tile * 4
    return pl.pallas_call(
        functools.partial(_attn_kernel, tile=tile, head_dim=head_dim, lambda_init=lambda_init),
        name="diff_attention",
        out_shape=jax.ShapeDtypeStruct((s_len, n_heads * hw), BF16),
        grid=(n_heads, n_tiles),
        in_specs=[
            pl.BlockSpec(memory_space=pltpu.SMEM),
            pl.BlockSpec((None, hw, tile), lambda h, qi: (qi, h, 0)),
            pl.BlockSpec((s_len, hw), lambda h, qi: (0, h)),
            pl.BlockSpec((n_tiles, hw, tile), lambda h, qi: (0, h, 0)),
            vec_spec(head_dim), vec_spec(head_dim), vec_spec(head_dim), vec_spec(head_dim),
            pl.BlockSpec((hw, 1), lambda h, qi: (0, 0)),
        ],
        out_specs=pl.BlockSpec((tile, hw), lambda h, qi: (qi, h)),
        scratch_shapes=[
            pltpu.VMEM((2, 2, tile, tile), F32),
            pltpu.VMEM((tile, head_dim), BF16),
            pltpu.VMEM((tile, tile), F32),
            pltpu.VMEM((2, 1, tile), F32),
            pltpu.VMEM((2, 1, tile), F32),
            pltpu.VMEM((2, hw, tile), F32),
        ],
        compiler_params=_params(("arbitrary", "arbitrary"), block_bytes),
    )(slopes, qt, k, vt, vec(lq1), vec(lk1), vec(lq2), vec(lk2), subln.reshape(-1, 1))


def _gates_kernel(ba_ref, alog_ref, dtb_ref, beta_ref, gc_ref):
    rows = ba_ref.shape[0]
    beta_ref[...] = jax.nn.sigmoid(ba_ref[...])
    g = -jnp.exp(alog_ref[...]) * jax.nn.softplus(ba_ref[...] + dtb_ref[...])
    r = lax.broadcasted_iota(jnp.int32, (rows, rows), 0)
    c = lax.broadcasted_iota(jnp.int32, (rows, rows), 1)
    tri = jnp.where(r // CHUNK == c // CHUNK, jnp.where(c <= r, 1.0, 0.0), 0.0).astype(F32)
    gc_ref[...] = jnp.dot(tri, g, preferred_element_type=F32, precision=lax.Precision.HIGHEST)


def _gates_call(ba, a_log_row, dt_bias_row):
    s_len, width = ba.shape
    tm = min(GATE_ROWS, s_len)
    out = jax.ShapeDtypeStruct((s_len, width), F32)
    row_spec = pl.BlockSpec((tm, width), lambda i: (i, 0))
    vec_spec = pl.BlockSpec((1, width), lambda i: (0, 0))
    return pl.pallas_call(
        _gates_kernel,
        name="gdn_gates",
        out_shape=(out, out),
        grid=(s_len // tm,),
        in_specs=[row_spec, vec_spec, vec_spec],
        out_specs=(row_spec, row_spec),
        compiler_params=_params(("arbitrary",), 8 * tm * 2 * V7X_LANES * 4 + 4 * tm * tm * 4),
    )(ba, a_log_row, dt_bias_row)


def _split_bf16(x):
    hi = x.astype(BF16)
    return hi, (x - hi.astype(F32)).astype(BF16)


def _delta_kernel(q_ref, k_ref, v_ref, z_ref, beta_ref, gc_ref, gct_ref, ng_ref, o_ref, state_scr, lmat_scr,
                  qkd_scr, qdec_scr, kdec_scr, *,
                  n_pairs, n_chunks, head_dim):
    hd = head_dim
    pairs = range(n_pairs)

    @pl.when(pl.program_id(1) == 0)
    def _():
        state_scr[...] = jnp.zeros(state_scr.shape, F32)

    ri = lax.broadcasted_iota(jnp.int32, (CHUNK, 2 * CHUNK), 0)
    lane = lax.broadcasted_iota(jnp.int32, (CHUNK, 2 * CHUNK), 1)
    left = lane < CHUNK
    ci = jnp.where(left, lane, lane - CHUNK)
    tril = ci <= ri
    strict = ci < ri
    eye = jnp.where(ci == ri, 1.0, 0.0).astype(F32)
    state_left = lax.broadcasted_iota(jnp.int32, (hd, 2 * hd), 1) < hd

    def block_diag(y, keep_left):
        zero = jnp.zeros_like(y)
        return jnp.concatenate([jnp.where(keep_left, y, zero), jnp.where(keep_left, zero, y)], axis=0)

    def pair_matmul(xh, xl, yh, yl):
        bh, bl = block_diag(yh, left), block_diag(yl, left)
        rhs = jnp.concatenate([jnp.concatenate([bh, bl], axis=1),
                               jnp.concatenate([bh, jnp.zeros_like(bl)], axis=1)], axis=0)
        out = jnp.dot(jnp.concatenate([xh, xl], axis=1), rhs, preferred_element_type=F32)
        return out[:, :2 * CHUNK] + out[:, 2 * CHUNK:]

    lanes = lambda a, h: jnp.broadcast_to(a[:, h:h + 1], (CHUNK, hd))

    def prepare_pair(c, p):
        rows = pl.ds(pl.multiple_of(c * CHUNK, CHUNK), CHUNK)
        beta, g_col, g_row = beta_ref[rows, :], gc_ref[rows, :], gct_ref[0, c][p:p + 1, :]
        k, q = k_ref[rows, p * hd:(p + 1) * hd], q_ref[rows, p * hd:(p + 1) * hd]
        kq = lax.dot_general(jnp.concatenate([k, q], axis=0), jnp.concatenate([k, k], axis=0), _NT,
                             preferred_element_type=F32)
        g_a, g_b = lanes(g_col, 2 * p), lanes(g_col, 2 * p + 1)
        b_pk = jnp.where(left, lanes(beta, 2 * p), lanes(beta, 2 * p + 1))
        decay = jnp.exp(jnp.where(tril, jnp.where(left, g_a, g_b) - g_row, -jnp.inf))
        lmat_scr[p] = jnp.where(strict, kq[:CHUNK] * b_pk * decay, 0.0)
        qkd_scr[p] = (kq[CHUNK:] * decay).astype(BF16)
        qf, kf = q.astype(F32), k.astype(F32)
        qdec_scr[p] = jnp.concatenate([qf * jnp.exp(g_a), qf * jnp.exp(g_b)], axis=1).astype(BF16)
        ge_a, ge_b = g_row[:, CHUNK - 1:CHUNK], g_row[:, 2 * CHUNK - 1:2 * CHUNK]
        kdec_scr[p] = jnp.concatenate([kf * jnp.exp(ge_a - g_a), kf * jnp.exp(ge_b - g_b)], axis=1).astype(BF16)

    for p in pairs:
        prepare_pair(0, p)

    levels = []
    power = 2
    while power * 2 < CHUNK:
        levels.append(power)
        power *= 2
    pairs_per_level = -(-n_pairs // len(levels))

    def chunk_step(c, carry):
        rows = pl.ds(pl.multiple_of(c * CHUNK, CHUNK), CHUNK)
        g_rows = gct_ref[0, c]
        b_rows = gct_ref[1, c]
        g_row = [g_rows[p:p + 1, :] for p in pairs]
        k = [k_ref[rows, p * hd:(p + 1) * hd] for p in pairs]
        lmat = [lmat_scr[p] for p in pairs]
        qk_decay = [qkd_scr[p] for p in pairs]
        q_dec = [qdec_scr[p] for p in pairs]
        k_dec = [(kdec_scr[p][:, :hd], kdec_scr[p][:, hd:]) for p in pairs]
        c_next = jnp.minimum(c + 1, n_chunks - 1)
        lsplit = [_split_bf16(lmat[p]) for p in pairs]
        x = [pair_matmul(*lsplit[p], *lsplit[p]) for p in pairs]
        acc = [eye - lmat[p] for p in pairs]
        for level in range(len(levels)):
            for p in pairs:
                sh, sl = _split_bf16(jnp.concatenate([acc[p], x[p]], axis=0))
                both = pair_matmul(sh, sl, sh[CHUNK:], sl[CHUNK:])
                acc[p] = acc[p] + both[:CHUNK]
                x[p] = both[CHUNK:]
                if p // pairs_per_level == level:
                    prepare_pair(c_next, p)
        tmat = [acc[p] + pair_matmul(*_split_bf16(acc[p]), *_split_bf16(x[p])) for p in pairs]
        zb = jnp.zeros((CHUNK, hd), BF16)
        uw = []
        for p in pairs:
            b_row = b_rows[p:p + 1, :]
            t_scaled = jnp.concatenate([tmat[p] * b_row, tmat[p] * (b_row * jnp.exp(g_row[p]))], axis=1)
            v_pk = v_ref[rows, 2 * p * hd:2 * (p + 1) * hd]
            rhs = jnp.concatenate([jnp.concatenate([v_pk[:, :hd], zb, zb, zb], axis=1),
                                   jnp.concatenate([zb, v_pk[:, hd:], zb, zb], axis=1),
                                   jnp.concatenate([zb, zb, k[p], zb], axis=1),
                                   jnp.concatenate([zb, zb, zb, k[p]], axis=1)], axis=0)
            uw.append(jnp.dot(t_scaled.astype(BF16), rhs, preferred_element_type=F32))
        s_old = [state_scr[p] for p in pairs]
        wq = []
        for p in pairs:
            wq.append(jnp.dot(jnp.concatenate([uw[p][:, 2 * hd:].astype(BF16), q_dec[p]], axis=0),
                              block_diag(s_old[p].astype(BF16), state_left), preferred_element_type=F32))
        o_pk = []
        for p in pairs:
            vn = (uw[p][:, :2 * hd] - wq[p][:CHUNK]).astype(BF16)
            vn_diag = jnp.concatenate([jnp.concatenate([vn[:, :hd], zb], axis=1),
                                       jnp.concatenate([zb, vn[:, hd:]], axis=1)], axis=0)
            o_pk.append(wq[p][CHUNK:] + jnp.dot(qk_decay[p], vn_diag, preferred_element_type=F32))
            ge_a, ge_b = g_row[p][:, CHUNK - 1:CHUNK], g_row[p][:, 2 * CHUNK - 1:2 * CHUNK]
            ds_a = lax.dot_general(k_dec[p][0], vn[:, :hd], _TN, preferred_element_type=F32)
            ds_b = lax.dot_general(k_dec[p][1], vn[:, hd:], _TN, preferred_element_type=F32)
            state_scr[p] = jnp.concatenate([s_old[p][:, :hd] * jnp.exp(ge_a) + ds_a,
                                            s_old[p][:, hd:] * jnp.exp(ge_b) + ds_b], axis=1)
        for p in pairs:
            wide_cols = slice(2 * p * hd, 2 * (p + 1) * hd)
            z_pk = z_ref[rows, wide_cols].astype(F32)
            gated = [_rms_rows(o_pk[p][:, h * hd:(h + 1) * hd], ng_ref[...]) * _silu(z_pk[:, h * hd:(h + 1) * hd])
                     for h in range(2)]
            o_ref[rows, wide_cols] = jnp.concatenate(gated, axis=1).astype(o_ref.dtype)
        return carry

    lax.fori_loop(0, n_chunks, chunk_step, 0)


def _delta_call(qkv, z, beta_g, gc_g, gct_g, norm_g, *, n_qk_heads, head_dim):
    s_len = qkv.shape[0]
    n_groups = n_qk_heads // DELTA_PAIRS
    heads = 2 * DELTA_PAIRS
    rows = DELTA_CHUNKS * CHUNK
    qk_w, v_w = DELTA_PAIRS * head_dim, heads * head_dim
    key_dim, value_dim = n_qk_heads * head_dim, 2 * n_qk_heads * head_dim
    k_off, v_off = key_dim // qk_w, 2 * key_dim // v_w
    block_bytes = 2 * rows * (2 * qk_w + 3 * v_w) * 2 + 6 * rows * V7X_LANES * 4 + heads * head_dim * head_dim * 4
    return pl.pallas_call(
        functools.partial(_delta_kernel, n_pairs=DELTA_PAIRS, n_chunks=DELTA_CHUNKS, head_dim=head_dim),
        name="gdn_delta_rule",
        out_shape=jax.ShapeDtypeStruct((s_len, value_dim), BF16),
        grid=(n_groups, s_len // rows),
        in_specs=[
            pl.BlockSpec((rows, qk_w), lambda g, b: (b, g)),
            pl.BlockSpec((rows, qk_w), lambda g, b: (b, k_off + g)),
            pl.BlockSpec((rows, v_w), lambda g, b: (b, v_off + g)),
            pl.BlockSpec((rows, v_w), lambda g, b: (b, g)),
            pl.BlockSpec((None, rows, heads), lambda g, b: (g, b, 0)),
            pl.BlockSpec((None, rows, heads), lambda g, b: (g, b, 0)),
            pl.BlockSpec((None, 2, DELTA_CHUNKS, DELTA_PAIRS, 2 * CHUNK), lambda g, b: (g, 0, b, 0, 0)),
            pl.BlockSpec((1, head_dim), lambda g, b: (0, 0)),
        ],
        out_specs=pl.BlockSpec((rows, v_w), lambda g, b: (b, g)),
        scratch_shapes=[pltpu.VMEM((DELTA_PAIRS, head_dim, 2 * head_dim), F32),
                        pltpu.VMEM((DELTA_PAIRS, CHUNK, 2 * CHUNK), F32),
                        pltpu.VMEM((DELTA_PAIRS, CHUNK, 2 * CHUNK), BF16),
                        pltpu.VMEM((DELTA_PAIRS, CHUNK, 2 * head_dim), BF16),
                        pltpu.VMEM((DELTA_PAIRS, CHUNK, 2 * head_dim), BF16)],
        compiler_params=_params(("arbitrary", "arbitrary"), block_bytes),
    )(qkv, qkv, qkv, z, beta_g, gc_g, gct_g, norm_g.reshape(1, head_dim))


def _diff_attention_layer(x, gain, w_qkv, layer, q_norm, k_norm, lq1, lk1, lq2, lk2, subln, w_o, lambda_init):
    s_len = x.shape[0]
    head_dim = q_norm.shape[0]
    n_heads = w_o.shape[1] // subln.shape[0]
    qk_dim = 2 * n_heads * head_dim
    v_dim = w_qkv.shape[2] - 2 * qk_dim
    tile = min(ATTN_TILE, s_len)
    vec_spec = pl.BlockSpec((1, head_dim), lambda i, j: (0, 0))

    def proj(name, n_cols, first_col, head_gain, scale, transpose):
        head_norm = ((head_gain.reshape(1, -1),), (vec_spec,)) if head_gain is not None else ((), ())
        return _proj_call(
            name, functools.partial(_head_proj_kernel, head_dim=head_dim, scale=scale, transpose=transpose, tile=tile,
                                    w_transposed=False, piece_rows=ROW_SUBTILE),
            x, gain, w_qkv, layer, n_cols, *head_norm, BF16, (),
            first_col_tile=first_col // COL_TILE, transposed_tile=tile if transpose else None)

    qt = proj("attn_q_proj", qk_dim, 0, q_norm, head_dim ** -0.5 * LOG2_E, True)
    k = proj("attn_k_proj", qk_dim, qk_dim, k_norm, 1.0, False)
    vt = proj("attn_v_proj", v_dim, 2 * qk_dim, None, 1.0, True)
    slopes = jnp.exp2(-8.0 * jnp.arange(1, n_heads + 1, dtype=F32) / n_heads) * LOG2_E
    o = _attn_call(qt, k, vt, slopes, lq1, lk1, lq2, lk2, subln, n_heads=n_heads, head_dim=head_dim,
                   lambda_init=lambda_init)
    return _mm_res_call(o, w_o, x, layer)


def _gated_deltanet_layer(x, gain, w_in, layer, conv_w, a_log, dt_bias, norm_g, w_out):
    s_len, d = x.shape
    head_dim = norm_g.shape[0]
    n_v_heads = a_log.shape[0]
    n_qk_heads = n_v_heads // 2
    key_dim, value_dim = n_qk_heads * head_dim, n_v_heads * head_dim
    conv_dim = 2 * key_dim + value_dim
    main_cols = conv_dim + value_dim
    n_conv_tiles = conv_dim // COL_TILE
    w_in_t = jnp.swapaxes(w_in, 1, 2)
    cw_spec = pl.BlockSpec((None, conv_w.shape[1], COL_TILE), lambda i, j: (layer, 0, j))
    qkv = _proj_call(
        "gdn_in_proj", functools.partial(_gdn_conv_kernel, n_q_tiles=key_dim // COL_TILE,
                                         n_qk_tiles=2 * key_dim // COL_TILE, head_dim=head_dim,
                                         q_scale=head_dim ** -0.5),
        x, gain, w_in_t, layer, conv_dim, (conv_w,), (cw_spec,), BF16, _conv_scratch(s_len, n_conv_tiles),
        w_transposed=True)
    z = _proj_call(
        "gdn_z_proj", functools.partial(_head_proj_kernel, head_dim=head_dim, scale=1.0, transpose=False, tile=None,
                                        w_transposed=True, piece_rows=min(ROW_TILE, s_len)),
        x, gain, w_in_t, layer, value_dim, (), (), BF16, (), first_col_tile=n_conv_tiles, w_transposed=True)

    gate_cols = 2 * n_v_heads
    assert main_cols % gate_cols == 0
    ba = _proj_call("gdn_gate_proj", functools.partial(_plain_proj_kernel, w_transposed=True), x, gain, w_in_t, layer,
                    gate_cols, (), (), F32, (), col_tile=gate_cols, first_col_tile=main_cols // gate_cols,
                    w_transposed=True)
    on_a_lanes = lambda v: jnp.pad(v, (n_v_heads, 0)).reshape(1, -1)
    beta, gc = _gates_call(ba, on_a_lanes(a_log), on_a_lanes(dt_bias))
    beta, gc = beta[:, :n_v_heads], gc[:, n_v_heads:2 * n_v_heads]

    heads = 2 * DELTA_PAIRS
    n_groups = n_v_heads // heads
    beta_g = beta.reshape(s_len, n_groups, heads).transpose(1, 0, 2)
    gc_g = gc.reshape(s_len, n_groups, heads).transpose(1, 0, 2)
    rows_of = lambda a: a.reshape(s_len // CHUNK, CHUNK, n_groups, DELTA_PAIRS, 2).transpose(
        2, 0, 3, 4, 1).reshape(n_groups, s_len // CHUNK, DELTA_PAIRS, 2 * CHUNK)
    gct_g = jnp.stack([rows_of(gc), rows_of(beta)], axis=1)
    o = _delta_call(qkv, z, beta_g, gc_g, gct_g, norm_g, n_qk_heads=n_qk_heads, head_dim=head_dim)
    return _mm_res_call(o, w_out[layer].astype(BF16), x)


def _conv_ffn_layer(x, gain, w_up, conv_w, conv_b, w_down, layer):
    n_cols = w_up.shape[2]
    n_tiles = pl.cdiv(n_cols, COL_TILE)
    tile_spec = lambda rows: pl.BlockSpec((None, rows, COL_TILE), lambda i, j: (layer, 0, j))
    u = _proj_call("ffn_up_proj", _ffn_up_kernel, x, gain, w_up, layer, n_cols, (conv_w, conv_b[:, None, :]),
                   (tile_spec(conv_w.shape[1]), tile_spec(1)), BF16, _conv_scratch(x.shape[0], n_tiles))
    return _ffn_down_call(u, w_down.astype(BF16), layer, x)


def kernel(x, mixer_norm, ffn_norm, diff_w_qkv, diff_q_norm, diff_k_norm, diff_lambda_q1, diff_lambda_k1,
           diff_lambda_q2, diff_lambda_k2, diff_subln, diff_w_o, gdn_w_in, gdn_conv_w, gdn_A_log, gdn_dt_bias,
           gdn_norm, gdn_w_out, ffn_w_up, ffn_conv_w, ffn_conv_b, ffn_w_down):
    b_sz, s_len, d = x.shape
    depth = mixer_norm.shape[0]
    outs = []
    for b in range(b_sz):
        xb = x.reshape(s_len, d) if b_sz == 1 else x[b]
        for i in range(depth):
            j = i // 2
            if i % 2 == 0:
                lambda_init = 0.8 - 0.6 * math.exp(-0.3 * i)
                xb = _diff_attention_layer(xb, mixer_norm[i], diff_w_qkv, j, diff_q_norm[j], diff_k_norm[j],
                                           diff_lambda_q1[j], diff_lambda_k1[j], diff_lambda_q2[j],
                                           diff_lambda_k2[j], diff_subln[j], diff_w_o, lambda_init)
            else:
                xb = _gated_deltanet_layer(xb, mixer_norm[i], gdn_w_in, j, gdn_conv_w, gdn_A_log[j], gdn_dt_bias[j],
                                           gdn_norm[j], gdn_w_out)
            xb = _conv_ffn_layer(xb, ffn_norm[i], ffn_w_up, ffn_conv_w, ffn_conv_b, ffn_w_down, i)
        outs.append(xb)
    return outs[0].reshape(x.shape) if b_sz == 1 else jnp.stack(outs, axis=0)
```

```python
import functools
import math

import jax
import jax.numpy as jnp
from jax import lax
from jax.experimental import pallas as pl
from jax.experimental.pallas import tpu as pltpu

F32 = jnp.float32
BF16 = jnp.bfloat16

RMS_EPS = 1e-6
LOG2_E = math.log2(math.e)
BIAS_PARTS = 3
CHUNK = 64
V7X_VMEM_BYTES = 64 * 1024 * 1024
V7X_LANES = 128
V7X_SUBLANES = 8
COMPILER_TEMP_BYTES = 12 * 1024 * 1024

ROW_TILE = 1024
ROW_SUBTILE = 256
OUT_ROW_TILE = 512
DOWN_ROW_TILE = 256
COL_TILE = 512
PLAIN_COL_TILE = 1024
ATTN_TILE = 512
DELTA_PAIRS = 16
DELTA_CHUNKS = 8
GATE_ROWS = 256

_NT = (((1,), (1,)), ((), ()))
_TN = (((0,), (0,)), ((), ()))


def _params(semantics, block_bytes):
    limit = min(int(block_bytes) + COMPILER_TEMP_BYTES, V7X_VMEM_BYTES - (4 << 20))
    return pltpu.CompilerParams(dimension_semantics=semantics, vmem_limit_bytes=limit)


def _rms_rows(x, gain):
    return x * lax.rsqrt(jnp.mean(x * x, axis=-1, keepdims=True) + RMS_EPS) * gain


def _silu(x):
    return x * jax.nn.sigmoid(x)


def _matmul(h, w_bf16, w_transposed):
    if w_transposed:
        return lax.dot_general(h, w_bf16, _NT, preferred_element_type=F32)
    return jnp.dot(h, w_bf16, preferred_element_type=F32)


def _normalize_once(x_ref, g_ref, h_scr):
    @pl.when(pl.program_id(1) == 0)
    def _():
        h_scr[...] = _rms_rows(x_ref[...], g_ref[...]).astype(BF16)


def _pipelined_rows(h_scr, w_bf16, w_transposed, piece_rows, epilogue):
    n_sub = h_scr.shape[0] // piece_rows
    piece = lambda r: _matmul(h_scr[r * piece_rows:(r + 1) * piece_rows, :], w_bf16, w_transposed)
    u_prev = piece(0)
    for r in range(1, n_sub):
        u_next = piece(r)
        epilogue(r - 1, u_prev)
        u_prev = u_next
    epilogue(n_sub - 1, u_prev)


def _head_proj_kernel(x_ref, g_ref, w_ref, *refs, head_dim, scale, transpose, tile, w_transposed, piece_rows):
    normalize = len(refs) == 3
    gain_ref = refs[0] if normalize else None
    o_ref, h_scr = refs[-2:]
    _normalize_once(x_ref, g_ref, h_scr)

    def epilogue(r, acc):
        if normalize:
            acc = jnp.concatenate([_rms_rows(acc[:, c:c + head_dim], gain_ref[...]) * scale
                                   for c in range(0, acc.shape[1], head_dim)], axis=1)
        r0 = r * piece_rows
        if transpose:
            o_ref[r0 // tile, :, r0 % tile:r0 % tile + piece_rows] = acc.T.astype(o_ref.dtype)
        else:
            o_ref[r0:r0 + piece_rows, :] = acc.astype(o_ref.dtype)

    _pipelined_rows(h_scr, w_ref[...].astype(BF16), w_transposed, piece_rows, epilogue)


def _plain_proj_kernel(x_ref, g_ref, w_ref, o_ref, h_scr, *, w_transposed):
    _normalize_once(x_ref, g_ref, h_scr)
    o_ref[...] = _matmul(h_scr[...], w_ref[...].astype(BF16), w_transposed).astype(o_ref.dtype)


def _conv_proj_steps(h_scr, w_ref, cw_ref, o_ref, carry_scr, u_scr, *, piece_rows, piece_cols, w_transposed, finish):
    i, j = pl.program_id(0), pl.program_id(1)
    rows, kw = h_scr.shape[0], cw_ref.shape[0]
    head = V7X_SUBLANES

    @pl.when(jnp.logical_and(i == 0, j == 0))
    def _():
        carry_scr[...] = jnp.zeros(carry_scr.shape, F32)

    u_scr[0:head, :] = carry_scr[j]
    w_bf16 = w_ref[...].astype(BF16)
    pieces = [(r0, c0) for r0 in range(0, rows, piece_rows) for c0 in range(0, o_ref.shape[1], piece_cols)]

    def matmul_piece(r0, c0):
        cols = slice(c0, c0 + piece_cols)
        w_piece = w_bf16[cols, :] if w_transposed else w_bf16[:, cols]
        u_scr[head + r0:head + r0 + piece_rows, cols] = _matmul(h_scr[r0:r0 + piece_rows, :], w_piece, w_transposed)

    def epilogue_piece(p0, c0):
        cols = slice(c0, c0 + piece_cols)
        for r0 in range(p0, p0 + piece_rows, ROW_SUBTILE):
            y = u_scr[head + r0:head + r0 + ROW_SUBTILE, cols] * cw_ref[kw - 1:kw, cols]
            for s in range(1, kw):
                y = y + u_scr[head + r0 - s:head + r0 - s + ROW_SUBTILE, cols] * cw_ref[kw - 1 - s:kw - s, cols]
            o_ref[r0:r0 + ROW_SUBTILE, cols] = finish(y, j, cols).astype(o_ref.dtype)

    matmul_piece(*pieces[0])
    for prev, cur in zip(pieces[:-1], pieces[1:]):
        matmul_piece(*cur)
        epilogue_piece(*prev)
    epilogue_piece(*pieces[-1])
    carry_scr[j] = u_scr[rows:rows + head, :]


def _ffn_up_kernel(x_ref, g_ref, w_ref, cw_ref, cb_ref, o_ref, h_scr, carry_scr, u_scr):
    _normalize_once(x_ref, g_ref, h_scr)
    _conv_proj_steps(h_scr, w_ref, cw_ref, o_ref, carry_scr, u_scr, piece_rows=h_scr.shape[0] // 2,
                     piece_cols=o_ref.shape[1], w_transposed=False,
                     finish=lambda y, jj, cols: y + cb_ref[:, cols])


def _gdn_conv_kernel(x_ref, g_ref, w_ref, cw_ref, o_ref, h_scr, carry_scr, u_scr, *, n_q_tiles, n_qk_tiles, head_dim,
                     q_scale):
    _normalize_once(x_ref, g_ref, h_scr)

    def finish(y, jj, cols):
        y = _silu(y)
        l2_scale = jnp.where(jj < n_q_tiles, q_scale, 1.0)
        heads = []
        for c in range(0, y.shape[1], head_dim):
            blk = y[:, c:c + head_dim]
            inv = lax.rsqrt(jnp.sum(blk * blk, axis=-1, keepdims=True) + RMS_EPS) * l2_scale
            heads.append(blk * jnp.where(jj < n_qk_tiles, inv, 1.0))
        return jnp.concatenate(heads, axis=1)

    _conv_proj_steps(h_scr, w_ref, cw_ref, o_ref, carry_scr, u_scr, piece_rows=ROW_SUBTILE,
                     piece_cols=o_ref.shape[1], w_transposed=True, finish=finish)


def _proj_call(name, kernel, x, gain, w3, layer, n_cols, extra_inputs, extra_specs, out_dtype, scratch,
               col_tile=COL_TILE, first_col_tile=0, transposed_tile=None, w_transposed=False):
    s_len, d = x.shape
    tm, tn = min(ROW_TILE, s_len), col_tile
    grid = (s_len // tm, pl.cdiv(n_cols, tn))
    if w_transposed:
        w_spec = pl.BlockSpec((None, tn, d), lambda i, j: (layer, first_col_tile + j, 0))
    else:
        w_spec = pl.BlockSpec((None, d, tn), lambda i, j: (layer, 0, first_col_tile + j))
    in_specs = [pl.BlockSpec((tm, d), lambda i, j: (i, 0)), pl.BlockSpec((1, d), lambda i, j: (0, 0)), w_spec]
    in_specs += list(extra_specs)
    if transposed_tile is None:
        out_shape = jax.ShapeDtypeStruct((s_len, n_cols), out_dtype)
        out_spec = pl.BlockSpec((tm, tn), lambda i, j: (i, j))
    else:
        out_shape = jax.ShapeDtypeStruct((s_len // transposed_tile, n_cols, transposed_tile), out_dtype)
        out_spec = pl.BlockSpec((tm // transposed_tile, tn, transposed_tile), lambda i, j: (i, j, 0))
    out_bytes = jnp.dtype(out_dtype).itemsize
    block_bytes = 2 * tm * d * 4 + tm * d * 2 + 2 * d * tn * 4 + d * tn * 2 + 2 * tm * tn * out_bytes + 3 * tm * tn * 4
    return pl.pallas_call(
        kernel,
        name=name,
        out_shape=out_shape,
        grid=grid,
        in_specs=in_specs,
        out_specs=out_spec,
        scratch_shapes=[pltpu.VMEM((tm, d), BF16)] + list(scratch),
        compiler_params=_params(("arbitrary", "arbitrary"), block_bytes),
    )(x, gain.reshape(1, d), w3, *extra_inputs)


def _conv_scratch(s_len, n_tiles):
    tm = min(ROW_TILE, s_len)
    return (pltpu.VMEM((n_tiles, V7X_SUBLANES, COL_TILE), F32), pltpu.VMEM((V7X_SUBLANES + tm, COL_TILE), F32))


def _mm_res_kernel(a_ref, w_ref, r_ref, o_ref, *w_scr):
    if w_scr:
        @pl.when(pl.program_id(0) == 0)
        def _():
            for r0 in range(0, w_ref.shape[0], COL_TILE):
                w_scr[0][r0:r0 + COL_TILE, :] = w_ref[r0:r0 + COL_TILE, :].astype(BF16)
        w = w_scr[0][...]
    else:
        w = w_ref[...]
    o_ref[...] = r_ref[...] + jnp.dot(a_ref[...], w, preferred_element_type=F32)


def _mm_res_call(a, w, res, layer=None):
    s_len, k = a.shape
    n = w.shape[-1]
    tm = min(OUT_ROW_TILE, s_len)
    once = pl.Buffered(1)
    if w.ndim == 3:
        w_spec = pl.BlockSpec((None, k, n), lambda i: (layer, 0, 0), pipeline_mode=once)
        scratch = [pltpu.VMEM((k, n), BF16)]
    else:
        w_spec = pl.BlockSpec((k, n), lambda i: (0, 0), pipeline_mode=once)
        scratch = []
    block_bytes = 2 * tm * k * 2 + k * n * w.dtype.itemsize + len(scratch) * k * n * 2 + 5 * tm * n * 4
    return pl.pallas_call(
        _mm_res_kernel,
        name="out_proj_residual",
        out_shape=jax.ShapeDtypeStruct((s_len, n), F32),
        grid=(s_len // tm,),
        in_specs=[pl.BlockSpec((tm, k), lambda i: (i, 0)), w_spec, pl.BlockSpec((tm, n), lambda i: (i, 0))],
        out_specs=pl.BlockSpec((tm, n), lambda i: (i, 0)),
        scratch_shapes=scratch,
        compiler_params=_params(("arbitrary",), block_bytes),
    )(a, w, res)


def _ffn_down_kernel(uv_ref, ug_ref, w_ref, r_ref, o_ref, g_scr):
    d_ff = w_ref.shape[0]
    mid = pl.cdiv(d_ff // 2, 2 * V7X_LANES) * 2 * V7X_LANES
    acc = r_ref[...]
    for k0, k1 in ((0, mid), (mid, d_ff)):
        for c0 in range(k0, k1, COL_TILE):
            cols = slice(c0, min(c0 + COL_TILE, k1))
            g_scr[:, cols] = (_silu(ug_ref[:, cols].astype(F32)) * uv_ref[:, cols].astype(F32)).astype(BF16)
        acc = acc + jnp.dot(g_scr[:, k0:k1], w_ref[k0:k1, :], preferred_element_type=F32)
    o_ref[...] = acc


def _ffn_down_call(u, w_down, layer, res):
    s_len = u.shape[0]
    _, d_ff, n = w_down.shape
    tm = min(DOWN_ROW_TILE, s_len)
    block_bytes = 4 * tm * d_ff * 2 + tm * d_ff * 2 + d_ff * n * 2 + 6 * tm * n * 4
    return pl.pallas_call(
        _ffn_down_kernel,
        name="ffn_down_residual",
        out_shape=jax.ShapeDtypeStruct((s_len, n), F32),
        grid=(s_len // tm,),
        in_specs=[
            pl.BlockSpec((tm, d_ff), lambda i: (i, 0)),
            pl.BlockSpec((tm, d_ff), lambda i: (i, 1)),
            pl.BlockSpec((None, d_ff, n), lambda i: (layer, 0, 0), pipeline_mode=pl.Buffered(1)),
            pl.BlockSpec((tm, n), lambda i: (i, 0)),
        ],
        out_specs=pl.BlockSpec((tm, n), lambda i: (i, 0)),
        scratch_shapes=[pltpu.VMEM((tm, d_ff), BF16)],
        compiler_params=_params(("arbitrary",), block_bytes),
    )(u, u, w_down, res)


def _attn_kernel(slopes_ref, qt_ref, k_ref, vt_ref, lq1_ref, lk1_ref, lq2_ref, lk2_ref, subln_ref, o_ref,
                 s_scr, kb_scr, bias_scr, m_scr, l_scr, acc_scr, *, tile, head_dim, lambda_init):
    h, qi = pl.program_id(0), pl.program_id(1)
    slope = slopes_ref[h]
    hd = head_dim

    @pl.when(qi == 0)
    def _():
        key_bias = slope * lax.broadcasted_iota(jnp.int32, (tile, hd), 0).astype(F32)
        lane = lax.broadcasted_iota(jnp.int32, (tile, hd), 1)
        part, parts = key_bias, []
        for _ in range(BIAS_PARTS):
            parts.append(part.astype(BF16).astype(F32))
            part = part - parts[-1]
        kb_cols = jnp.zeros((tile, hd), F32)
        for c in reversed(range(BIAS_PARTS)):
            kb_cols = jnp.where(lane == c, parts[c], kb_cols)
        kb_scr[...] = kb_cols.astype(BF16)
        key = lax.broadcasted_iota(jnp.int32, (tile, tile), 0)
        qry = lax.broadcasted_iota(jnp.int32, (tile, tile), 1)
        bias_scr[...] = jnp.where((key // CHUNK) <= (qry // CHUNK),
                                  -slope * (jnp.abs(key - qry) + key).astype(F32), -jnp.inf)

    ones_rows = jnp.where(lax.broadcasted_iota(jnp.int32, (hd, tile), 0) < BIAS_PARTS, 1.0, 0.0).astype(BF16)

    def issue_scores(kj, slot):
        k_blk = k_ref[pl.ds(pl.multiple_of(kj * tile, tile), tile), :]
        for m in range(2):
            rows = slice(m * hd, (m + 1) * hd)
            s_scr[slot, m] = jnp.dot(jnp.concatenate([k_blk[:, rows], kb_scr[...]], axis=1),
                                     jnp.concatenate([qt_ref[rows, :], ones_rows], axis=0),
                                     preferred_element_type=F32)

    issue_scores(qi, 1)
    issue_scores(0, 0)

    vt_diag = vt_ref[qi]
    for m in range(2):
        z = s_scr[1, m] + bias_scr[...]
        mx = jnp.max(z, axis=0, keepdims=True)
        p = jnp.exp2(z - mx)
        m_scr[m] = mx
        l_scr[m] = jnp.sum(p, axis=0, keepdims=True)
        acc_scr[m] = jnp.dot(vt_diag, p.astype(BF16), preferred_element_type=F32)

    q_pos = (lax.broadcasted_iota(jnp.int32, (1, tile), 1) + qi * tile).astype(F32)

    def kv_step(kj, slot, issue_next):
        if issue_next:
            issue_scores(jnp.minimum(kj + 1, qi - 1), 1 - slot)
        c = -slope * (q_pos - (kj * tile).astype(F32))
        vt_blk = vt_ref[kj]
        for m in range(2):
            z = s_scr[slot, m]
            m_old = m_scr[m]
            m_new = jnp.maximum(m_old, jnp.max(z, axis=0, keepdims=True) + c)
            p = jnp.exp2(z - (m_new - c))
            alpha = jnp.exp2(m_old - m_new)
            l_scr[m] = alpha * l_scr[m] + jnp.sum(p, axis=0, keepdims=True)
            acc_scr[m] = alpha * acc_scr[m] + jnp.dot(vt_blk, p.astype(BF16), preferred_element_type=F32)
            m_scr[m] = m_new

    def two_steps(t, carry):
        kv_step(2 * t, 0, True)
        kv_step(2 * t + 1, 1, True)
        return carry

    lax.fori_loop(0, qi // 2, two_steps, 0)

    @pl.when(lax.rem(qi, 2) == 1)
    def _():
        kv_step(qi - 1, 0, False)

    lam = (jnp.exp(jnp.sum(lq1_ref[...] * lk1_ref[...], axis=-1, keepdims=True))
           - jnp.exp(jnp.sum(lq2_ref[...] * lk2_ref[...], axis=-1, keepdims=True)) + lambda_init)
    o_t = acc_scr[0] / l_scr[0] - lam * (acc_scr[1] / l_scr[1])
    inv = lax.rsqrt(jnp.mean(o_t * o_t, axis=0, keepdims=True) + RMS_EPS)
    o_ref[...] = ((o_t * inv * subln_ref[...]) * (1.0 - lambda_init)).T.astype(o_ref.dtype)


def _attn_call(qt, k, vt, slopes, lq1, lk1, lq2, lk2, subln, *, n_heads, head_dim, lambda_init):
    n_tiles, _, tile = qt.shape
    s_len = k.shape[0]
    hw = 2 * head_dim
    vec = lambda a: a.reshape(1, -1)
    vec_spec = lambda n: pl.BlockSpec((1, n), lambda h, qi: (0, 0))
    block_bytes = 2 * (2 * s_len * hw * 2) + 4 * tile * hw * 2 + 2 * tile * hw * 4 + 5 * tile * tile * 4 \
        + 6 * tile * tile * 4
    return pl.pallas_call(
        functools.partial(_attn_kernel, tile=tile, head_dim=head_dim, lambda_init=lambda_init),
        name="diff_attention",
        out_shape=jax.ShapeDtypeStruct((s_len, n_heads * hw), BF16),
        grid=(n_heads, n_tiles),
        in_specs=[
            pl.BlockSpec(memory_space=pltpu.SMEM),
            pl.BlockSpec((None, hw, tile), lambda h, qi: (qi, h, 0)),
            pl.BlockSpec((s_len, hw), lambda h, qi: (0, h)),
            pl.BlockSpec((n_tiles, hw, tile), lambda h, qi: (0, h, 0)),
            vec_spec(head_dim), vec_spec(head_dim), vec_spec(head_dim), vec_spec(head_dim),
            pl.BlockSpec((hw, 1), lambda h, qi: (0, 0)),
        ],
        out_specs=pl.BlockSpec((tile, hw), lambda h, qi: (qi, h)),
        scratch_shapes=[
            pltpu.VMEM((2, 2, tile, tile), F32),
            pltpu.VMEM((tile, head_dim), BF16),
            pltpu.VMEM((tile, tile), F32),
            pltpu.VMEM((2, 1, tile), F32),
            pltpu.VMEM((2, 1, tile), F32),
            pltpu.VMEM((2, hw, tile), F32),
        ],
        compiler_params=_params(("arbitrary", "arbitrary"), block_bytes),
    )(slopes, qt, k, vt, vec(lq1), vec(lk1), vec(lq2), vec(lk2), subln.reshape(-1, 1))


def _gates_kernel(ba_ref, alog_ref, dtb_ref, beta_ref, gc_ref):
    rows = ba_ref.shape[0]
    beta_ref[...] = jax.nn.sigmoid(ba_ref[...])
    g = -jnp.exp(alog_ref[...]) * jax.nn.softplus(ba_ref[...] + dtb_ref[...])
    r = lax.broadcasted_iota(jnp.int32, (rows, rows), 0)
    c = lax.broadcasted_iota(jnp.int32, (rows, rows), 1)
    tri = jnp.where(r // CHUNK == c // CHUNK, jnp.where(c <= r, 1.0, 0.0), 0.0).astype(F32)
    gc_ref[...] = jnp.dot(tri, g, preferred_element_type=F32, precision=lax.Precision.HIGHEST)


def _gates_call(ba, a_log_row, dt_bias_row):
    s_len, width = ba.shape
    tm = min(GATE_ROWS, s_len)
    out = jax.ShapeDtypeStruct((s_len, width), F32)
    row_spec = pl.BlockSpec((tm, width), lambda i: (i, 0))
    vec_spec = pl.BlockSpec((1, width), lambda i: (0, 0))
    return pl.pallas_call(
        _gates_kernel,
        name="gdn_gates",
        out_shape=(out, out),
        grid=(s_len // tm,),
        in_specs=[row_spec, vec_spec, vec_spec],
        out_specs=(row_spec, row_spec),
        compiler_params=_params(("arbitrary",), 8 * tm * 2 * V7X_LANES * 4 + 4 * tm * tm * 4),
    )(ba, a_log_row, dt_bias_row)


def _split_bf16(x):
    hi = x.astype(BF16)
    return hi, (x - hi.astype(F32)).astype(BF16)


def _delta_kernel(q_ref, k_ref, v_ref, z_ref, beta_ref, gc_ref, gct_ref, ng_ref, o_ref, state_scr, lmat_scr,
                  qkd_scr, qdec_scr, kdec_scr, *,
                  n_pairs, n_chunks, head_dim):
    hd = head_dim
    pairs = range(n_pairs)

    @pl.when(pl.program_id(1) == 0)
    def _():
        state_scr[...] = jnp.zeros(state_scr.shape, F32)

    ri = lax.broadcasted_iota(jnp.int32, (CHUNK, 2 * CHUNK), 0)
    lane = lax.broadcasted_iota(jnp.int32, (CHUNK, 2 * CHUNK), 1)
    left = lane < CHUNK
    ci = jnp.where(left, lane, lane - CHUNK)
    tril = ci <= ri
    strict = ci < ri
    eye = jnp.where(ci == ri, 1.0, 0.0).astype(F32)
    state_left = lax.broadcasted_iota(jnp.int32, (hd, 2 * hd), 1) < hd

    def block_diag(y, keep_left):
        zero = jnp.zeros_like(y)
        return jnp.concatenate([jnp.where(keep_left, y, zero), jnp.where(keep_left, zero, y)], axis=0)

    def pair_matmul(xh, xl, yh, yl):
        bh, bl = block_diag(yh, left), block_diag(yl, left)
        rhs = jnp.concatenate([jnp.concatenate([bh, bl], axis=1),
                               jnp.concatenate([bh, jnp.zeros_like(bl)], axis=1)], axis=0)
        out = jnp.dot(jnp.concatenate([xh, xl], axis=1), rhs, preferred_element_type=F32)
        return out[:, :2 * CHUNK] + out[:, 2 * CHUNK:]

    lanes = lambda a, h: jnp.broadcast_to(a[:, h:h + 1], (CHUNK, hd))

    def prepare_pair(c, p):
        rows = pl.ds(pl.multiple_of(c * CHUNK, CHUNK), CHUNK)
        beta, g_col, g_row = beta_ref[rows, :], gc_ref[rows, :], gct_ref[0, c][p:p + 1, :]
        k, q = k_ref[rows, p * hd:(p + 1) * hd], q_ref[rows, p * hd:(p + 1) * hd]
        kq = lax.dot_general(jnp.concatenate([k, q], axis=0), jnp.concatenate([k, k], axis=0), _NT,
                             preferred_element_type=F32)
        g_a, g_b = lanes(g_col, 2 * p), lanes(g_col, 2 * p + 1)
        b_pk = jnp.where(left, lanes(beta, 2 * p), lanes(beta, 2 * p + 1))
        decay = jnp.exp(jnp.where(tril, jnp.where(left, g_a, g_b) - g_row, -jnp.inf))
        lmat_scr[p] = jnp.where(strict, kq[:CHUNK] * b_pk * decay, 0.0)
        qkd_scr[p] = (kq[CHUNK:] * decay).astype(BF16)
        qf, kf = q.astype(F32), k.astype(F32)
        qdec_scr[p] = jnp.concatenate([qf * jnp.exp(g_a), qf * jnp.exp(g_b)], axis=1).astype(BF16)
        ge_a, ge_b = g_row[:, CHUNK - 1:CHUNK], g_row[:, 2 * CHUNK - 1:2 * CHUNK]
        kdec_scr[p] = jnp.concatenate([kf * jnp.exp(ge_a - g_a), kf * jnp.exp(ge_b - g_b)], axis=1).astype(BF16)

    for p in pairs:
        prepare_pair(0, p)

    levels = []
    power = 2
    while power * 2 < CHUNK:
        levels.append(power)
        power *= 2
    pairs_per_level = -(-n_pairs // len(levels))

    def chunk_step(c, carry):
        rows = pl.ds(pl.multiple_of(c * CHUNK, CHUNK), CHUNK)
        g_rows = gct_ref[0, c]
        b_rows = gct_ref[1, c]
        g_row = [g_rows[p:p + 1, :] for p in pairs]
        k = [k_ref[rows, p * hd:(p + 1) * hd] for p in pairs]
        lmat = [lmat_scr[p] for p in pairs]
        qk_decay = [qkd_scr[p] for p in pairs]
        q_dec = [qdec_scr[p] for p in pairs]
        k_dec = [(kdec_scr[p][:, :hd], kdec_scr[p][:, hd:]) for p in pairs]
        c_next = jnp.minimum(c + 1, n_chunks - 1)
        lsplit = [_split_bf16(lmat[p]) for p in pairs]
        x = [pair_matmul(*lsplit[p], *lsplit[p]) for p in pairs]
        acc = [eye - lmat[p] for p in pairs]
        for level in range(len(levels)):
            for p in pairs:
                sh, sl = _split_bf16(jnp.concatenate([acc[p], x[p]], axis=0))
                both = pair_matmul(sh, sl, sh[CHUNK:], sl[CHUNK:])
                acc[p] = acc[p] + both[:CHUNK]
                x[p] = both[CHUNK:]
                if p // pairs_per_level == level:
                    prepare_pair(c_next, p)
        tmat = [acc[p] + pair_matmul(*_split_bf16(acc[p]), *_split_bf16(x[p])) for p in pairs]
        zb = jnp.zeros((CHUNK, hd), BF16)
        uw = []
        for p in pairs:
            b_row = b_rows[p:p + 1, :]
            t_scaled = jnp.concatenate([tmat[p] * b_row, tmat[p] * (b_row * jnp.exp(g_row[p]))], axis=1)
            v_pk = v_ref[rows, 2 * p * hd:2 * (p + 1) * hd]
            rhs = jnp.concatenate([jnp.concatenate([v_pk[:, :hd], zb, zb, zb], axis=1),
                                   jnp.concatenate([zb, v_pk[:, hd:], zb, zb], axis=1),
                                   jnp.concatenate([zb, zb, k[p], zb], axis=1),
                                   jnp.concatenate([zb, zb, zb, k[p]], axis=1)], axis=0)
            uw.append(jnp.dot(t_scaled.astype(BF16), rhs, preferred_element_type=F32))
        s_old = [state_scr[p] for p in pairs]
        wq = []
        for p in pairs:
            wq.append(jnp.dot(jnp.concatenate([uw[p][:, 2 * hd:].astype(BF16), q_dec[p]], axis=0),
                              block_diag(s_old[p].astype(BF16), state_left), preferred_element_type=F32))
        o_pk = []
        for p in pairs:
            vn = (uw[p][:, :2 * hd] - wq[p][:CHUNK]).astype(BF16)
            vn_diag = jnp.concatenate([jnp.concatenate([vn[:, :hd], zb], axis=1),
                                       jnp.concatenate([zb, vn[:, hd:]], axis=1)], axis=0)
            o_pk.append(wq[p][CHUNK:] + jnp.dot(qk_decay[p], vn_diag, preferred_element_type=F32))
            ge_a, ge_b = g_row[p][:, CHUNK - 1:CHUNK], g_row[p][:, 2 * CHUNK - 1:2 * CHUNK]
            ds_a = lax.dot_general(k_dec[p][0], vn[:, :hd], _TN, preferred_element_type=F32)
            ds_b = lax.dot_general(k_dec[p][1], vn[:, hd:], _TN, preferred_element_type=F32)
            state_scr[p] = jnp.concatenate([s_old[p][:, :hd] * jnp.exp(ge_a) + ds_a,
                                            s_old[p][:, hd:] * jnp.exp(ge_b) + ds_b], axis=1)
        for p in pairs:
            wide_cols = slice(2 * p * hd, 2 * (p + 1) * hd)
            z_pk = z_ref[rows, wide_cols].astype(F32)
            gated = [_rms_rows(o_pk[p][:, h * hd:(h + 1) * hd], ng_ref[...]) * _silu(z_pk[:, h * hd:(h + 1) * hd])
                     for h in range(2)]
            o_ref[rows, wide_cols] = jnp.concatenate(gated, axis=1).astype(o_ref.dtype)
        return carry

    lax.fori_loop(0, n_chunks, chunk_step, 0)


def _delta_call(qkv, z, beta_g, gc_g, gct_g, norm_g, *, n_qk_heads, head_dim):
    s_len = qkv.shape[0]
    n_groups = n_qk_heads // DELTA_PAIRS
    heads = 2 * DELTA_PAIRS
    rows = DELTA_CHUNKS * CHUNK
    qk_w, v_w = DELTA_PAIRS * head_dim, heads * head_dim
    key_dim, value_dim = n_qk_heads * head_dim, 2 * n_qk_heads * head_dim
    k_off, v_off = key_dim // qk_w, 2 * key_dim // v_w
    block_bytes = 2 * rows * (2 * qk_w + 3 * v_w) * 2 + 6 * rows * V7X_LANES * 4 + heads * head_dim * head_dim * 4
    return pl.pallas_call(
        functools.partial(_delta_kernel, n_pairs=DELTA_PAIRS, n_chunks=DELTA_CHUNKS, head_dim=head_dim),
        name="gdn_delta_rule",
        out_shape=jax.ShapeDtypeStruct((s_len, value_dim), BF16),
        grid=(n_groups, s_len // rows),
        in_specs=[
            pl.BlockSpec((rows, qk_w), lambda g, b: (b, g)),
            pl.BlockSpec((rows, qk_w), lambda g, b: (b, k_off + g)),
            pl.BlockSpec((rows, v_w), lambda g, b: (b, v_off + g)),
            pl.BlockSpec((rows, v_w), lambda g, b: (b, g)),
            pl.BlockSpec((None, rows, heads), lambda g, b: (g, b, 0)),
            pl.BlockSpec((None, rows, heads), lambda g, b: (g, b, 0)),
            pl.BlockSpec((None, 2, DELTA_CHUNKS, DELTA_PAIRS, 2 * CHUNK), lambda g, b: (g, 0, b, 0, 0)),
            pl.BlockSpec((1, head_dim), lambda g, b: (0, 0)),
        ],
        out_specs=pl.BlockSpec((rows, v_w), lambda g, b: (b, g)),
        scratch_shapes=[pltpu.VMEM((DELTA_PAIRS, head_dim, 2 * head_dim), F32),
                        pltpu.VMEM((DELTA_PAIRS, CHUNK, 2 * CHUNK), F32),
                        pltpu.VMEM((DELTA_PAIRS, CHUNK, 2 * CHUNK), BF16),
                        pltpu.VMEM((DELTA_PAIRS, CHUNK, 2 * head_dim), BF16),
                        pltpu.VMEM((DELTA_PAIRS, CHUNK, 2 * head_dim), BF16)],
        compiler_params=_params(("arbitrary", "arbitrary"), block_bytes),
    )(qkv, qkv, qkv, z, beta_g, gc_g, gct_g, norm_g.reshape(1, head_dim))


def _diff_attention_layer(x, gain, w_qkv, layer, q_norm, k_norm, lq1, lk1, lq2, lk2, subln, w_o, lambda_init):
    s_len = x.shape[0]
    head_dim = q_norm.shape[0]
    n_heads = w_o.shape[1] // subln.shape[0]
    qk_dim = 2 * n_heads * head_dim
    v_dim = w_qkv.shape[2] - 2 * qk_dim
    tile = min(ATTN_TILE, s_len)
    vec_spec = pl.BlockSpec((1, head_dim), lambda i, j: (0, 0))

    def proj(name, n_cols, first_col, head_gain, scale, transpose):
        head_norm = ((head_gain.reshape(1, -1),), (vec_spec,)) if head_gain is not None else ((), ())
        return _proj_call(
            name, functools.partial(_head_proj_kernel, head_dim=head_dim, scale=scale, transpose=transpose, tile=tile,
                                    w_transposed=False, piece_rows=ROW_SUBTILE),
            x, gain, w_qkv, layer, n_cols, *head_norm, BF16, (),
            col_tile=PLAIN_COL_TILE, first_col_tile=first_col // PLAIN_COL_TILE,
            transposed_tile=tile if transpose else None)

    qt = proj("attn_q_proj", qk_dim, 0, q_norm, head_dim ** -0.5 * LOG2_E, True)
    k = proj("attn_k_proj", qk_dim, qk_dim, k_norm, 1.0, False)
    vt = proj("attn_v_proj", v_dim, 2 * qk_dim, None, 1.0, True)
    slopes = jnp.exp2(-8.0 * jnp.arange(1, n_heads + 1, dtype=F32) / n_heads) * LOG2_E
    o = _attn_call(qt, k, vt, slopes, lq1, lk1, lq2, lk2, subln, n_heads=n_heads, head_dim=head_dim,
                   lambda_init=lambda_init)
    return _mm_res_call(o, w_o, x, layer)


def _gated_deltanet_layer(x, gain, w_in, layer, conv_w, a_log, dt_bias, norm_g, w_out):
    s_len, d = x.shape
    head_dim = norm_g.shape[0]
    n_v_heads = a_log.shape[0]
    n_qk_heads = n_v_heads // 2
    key_dim, value_dim = n_qk_heads * head_dim, n_v_heads * head_dim
    conv_dim = 2 * key_dim + value_dim
    main_cols = conv_dim + value_dim
    n_conv_tiles = conv_dim // COL_TILE
    w_in_t = jnp.swapaxes(w_in, 1, 2)
    cw_spec = pl.BlockSpec((None, conv_w.shape[1], COL_TILE), lambda i, j: (layer, 0, j))
    qkv = _proj_call(
        "gdn_in_proj", functools.partial(_gdn_conv_kernel, n_q_tiles=key_dim // COL_TILE,
                                         n_qk_tiles=2 * key_dim // COL_TILE, head_dim=head_dim,
                                         q_scale=head_dim ** -0.5),
        x, gain, w_in_t, layer, conv_dim, (conv_w,), (cw_spec,), BF16, _conv_scratch(s_len, n_conv_tiles),
        w_transposed=True)
    z = _proj_call(
        "gdn_z_proj", functools.partial(_head_proj_kernel, head_dim=head_dim, scale=1.0, transpose=False, tile=None,
                                        w_transposed=True, piece_rows=min(ROW_TILE, s_len)),
        x, gain, w_in_t, layer, value_dim, (), (), BF16, (), col_tile=PLAIN_COL_TILE,
        first_col_tile=conv_dim // PLAIN_COL_TILE, w_transposed=True)

    gate_cols = 2 * n_v_heads
    assert main_cols % gate_cols == 0
    ba = _proj_call("gdn_gate_proj", functools.partial(_plain_proj_kernel, w_transposed=True), x, gain, w_in_t, layer,
                    gate_cols, (), (), F32, (), col_tile=gate_cols, first_col_tile=main_cols // gate_cols,
                    w_transposed=True)
    on_a_lanes = lambda v: jnp.pad(v, (n_v_heads, 0)).reshape(1, -1)
    beta, gc = _gates_call(ba, on_a_lanes(a_log), on_a_lanes(dt_bias))
    beta, gc = beta[:, :n_v_heads], gc[:, n_v_heads:2 * n_v_heads]

    heads = 2 * DELTA_PAIRS
    n_groups = n_v_heads // heads
    beta_g = beta.reshape(s_len, n_groups, heads).transpose(1, 0, 2)
    gc_g = gc.reshape(s_len, n_groups, heads).transpose(1, 0, 2)
    rows_of = lambda a: a.reshape(s_len // CHUNK, CHUNK, n_groups, DELTA_PAIRS, 2).transpose(
        2, 0, 3, 4, 1).reshape(n_groups, s_len // CHUNK, DELTA_PAIRS, 2 * CHUNK)
    gct_g = jnp.stack([rows_of(gc), rows_of(beta)], axis=1)
    o = _delta_call(qkv, z, beta_g, gc_g, gct_g, norm_g, n_qk_heads=n_qk_heads, head_dim=head_dim)
    return _mm_res_call(o, w_out[layer].astype(BF16), x)


def _conv_ffn_layer(x, gain, w_up, conv_w, conv_b, w_down, layer):
    n_cols = w_up.shape[2]
    n_tiles = pl.cdiv(n_cols, COL_TILE)
    tile_spec = lambda rows: pl.BlockSpec((None, rows, COL_TILE), lambda i, j: (layer, 0, j))
    u = _proj_call("ffn_up_proj", _ffn_up_kernel, x, gain, w_up, layer, n_cols, (conv_w, conv_b[:, None, :]),
                   (tile_spec(conv_w.shape[1]), tile_spec(1)), BF16, _conv_scratch(x.shape[0], n_tiles))
    return _ffn_down_call(u, w_down.astype(BF16), layer, x)


def kernel(x, mixer_norm, ffn_norm, diff_w_qkv, diff_q_norm, diff_k_norm, diff_lambda_q1, diff_lambda_k1,
           diff_lambda_q2, diff_lambda_k2, diff_subln, diff_w_o, gdn_w_in, gdn_conv_w, gdn_A_log, gdn_dt_bias,
           gdn_norm, gdn_w_out, ffn_w_up, ffn_conv_w, ffn_conv_b, ffn_w_down):
    b_sz, s_len, d = x.shape
    depth = mixer_norm.shape[0]
    outs = []
    for b in range(b_sz):
        xb = x.reshape(s_len, d) if b_sz == 1 else x[b]
        for i in range(depth):
            j = i // 2
            if i % 2 == 0:
                lambda_init = 0.8 - 0.6 * math.exp(-0.3 * i)
                xb = _diff_attention_layer(xb, mixer_norm[i], diff_w_qkv, j, diff_q_norm[j], diff_k_norm[j],
                                           diff_lambda_q1[j], diff_lambda_k1[j], diff_lambda_q2[j],
                                           diff_lambda_k2[j], diff_subln[j], diff_w_o, lambda_init)
            else:
                xb = _gated_deltanet_layer(xb, mixer_norm[i], gdn_w_in, j, gdn_conv_w, gdn_A_log[j], gdn_dt_bias[j],
                                           gdn_norm[j], gdn_w_out)
            xb = _conv_ffn_layer(xb, ffn_norm[i], ffn_w_up, ffn_conv_w, ffn_conv_b, ffn_w_down, i)
        outs.append(xb)
    return outs[0].reshape(x.shape) if b_sz == 1 else jnp.stack(outs, axis=0)
```

```python
import functools
import math

import jax
import jax.numpy as jnp
from jax import lax
from jax.experimental import pallas as pl
from jax.experimental.pallas import tpu as pltpu

F32 = jnp.float32
BF16 = jnp.bfloat16

RMS_EPS = 1e-6
LOG2_E = math.log2(math.e)
BIAS_PARTS = 3
CHUNK = 64
V7X_VMEM_BYTES = 64 * 1024 * 1024
V7X_LANES = 128
V7X_SUBLANES = 8
COMPILER_TEMP_BYTES = 12 * 1024 * 1024

ROW_TILE = 1024
ROW_SUBTILE = 256
OUT_ROW_TILE = 512
DOWN_ROW_TILE = 256
COL_TILE = 1024
PLAIN_COL_TILE = 1024
ATTN_TILE = 512
DELTA_PAIRS = 16
DELTA_CHUNKS = 8
GATE_ROWS = 256

_NT = (((1,), (1,)), ((), ()))
_TN = (((0,), (0,)), ((), ()))


def _params(semantics, block_bytes):
    limit = min(int(block_bytes) + COMPILER_TEMP_BYTES, V7X_VMEM_BYTES - (4 << 20))
    return pltpu.CompilerParams(dimension_semantics=semantics, vmem_limit_bytes=limit)


def _rms_rows(x, gain):
    return x * lax.rsqrt(jnp.mean(x * x, axis=-1, keepdims=True) + RMS_EPS) * gain


def _silu(x):
    return x * jax.nn.sigmoid(x)


def _matmul(h, w_bf16, w_transposed):
    if w_transposed:
        return lax.dot_general(h, w_bf16, _NT, preferred_element_type=F32)
    return jnp.dot(h, w_bf16, preferred_element_type=F32)


def _normalize_once(x_ref, g_ref, h_scr):
    @pl.when(pl.program_id(1) == 0)
    def _():
        h_scr[...] = _rms_rows(x_ref[...], g_ref[...]).astype(BF16)


def _pipelined_rows(h_scr, w_bf16, w_transposed, piece_rows, epilogue):
    n_sub = h_scr.shape[0] // piece_rows
    piece = lambda r: _matmul(h_scr[r * piece_rows:(r + 1) * piece_rows, :], w_bf16, w_transposed)
    u_prev = piece(0)
    for r in range(1, n_sub):
        u_next = piece(r)
        epilogue(r - 1, u_prev)
        u_prev = u_next
    epilogue(n_sub - 1, u_prev)


def _head_proj_kernel(x_ref, g_ref, w_ref, *refs, head_dim, scale, transpose, tile, w_transposed, piece_rows):
    normalize = len(refs) == 3
    gain_ref = refs[0] if normalize else None
    o_ref, h_scr = refs[-2:]
    _normalize_once(x_ref, g_ref, h_scr)

    def epilogue(r, acc):
        if normalize:
            acc = jnp.concatenate([_rms_rows(acc[:, c:c + head_dim], gain_ref[...]) * scale
                                   for c in range(0, acc.shape[1], head_dim)], axis=1)
        r0 = r * piece_rows
        if transpose:
            o_ref[r0 // tile, :, r0 % tile:r0 % tile + piece_rows] = acc.T.astype(o_ref.dtype)
        else:
            o_ref[r0:r0 + piece_rows, :] = acc.astype(o_ref.dtype)

    _pipelined_rows(h_scr, w_ref[...].astype(BF16), w_transposed, piece_rows, epilogue)


def _plain_proj_kernel(x_ref, g_ref, w_ref, o_ref, h_scr, *, w_transposed):
    _normalize_once(x_ref, g_ref, h_scr)
    o_ref[...] = _matmul(h_scr[...], w_ref[...].astype(BF16), w_transposed).astype(o_ref.dtype)


def _conv_proj_steps(h_scr, w_ref, cw_ref, o_ref, carry_scr, u_scr, *, piece_rows, piece_cols, w_transposed, finish):
    i, j = pl.program_id(0), pl.program_id(1)
    rows, kw = h_scr.shape[0], cw_ref.shape[0]
    head = V7X_SUBLANES

    @pl.when(jnp.logical_and(i == 0, j == 0))
    def _():
        carry_scr[...] = jnp.zeros(carry_scr.shape, F32)

    u_scr[0:head, :] = carry_scr[j]
    w_bf16 = w_ref[...].astype(BF16)
    pieces = [(r0, c0) for r0 in range(0, rows, piece_rows) for c0 in range(0, o_ref.shape[1], piece_cols)]

    def matmul_piece(r0, c0):
        cols = slice(c0, c0 + piece_cols)
        w_piece = w_bf16[cols, :] if w_transposed else w_bf16[:, cols]
        u_scr[head + r0:head + r0 + piece_rows, cols] = _matmul(h_scr[r0:r0 + piece_rows, :], w_piece, w_transposed)

    def epilogue_piece(p0, c0):
        cols = slice(c0, c0 + piece_cols)
        for r0 in range(p0, p0 + piece_rows, ROW_SUBTILE):
            y = u_scr[head + r0:head + r0 + ROW_SUBTILE, cols] * cw_ref[kw - 1:kw, cols]
            for s in range(1, kw):
                y = y + u_scr[head + r0 - s:head + r0 - s + ROW_SUBTILE, cols] * cw_ref[kw - 1 - s:kw - s, cols]
            o_ref[r0:r0 + ROW_SUBTILE, cols] = finish(y, j, cols).astype(o_ref.dtype)

    matmul_piece(*pieces[0])
    for prev, cur in zip(pieces[:-1], pieces[1:]):
        matmul_piece(*cur)
        epilogue_piece(*prev)
    epilogue_piece(*pieces[-1])
    carry_scr[j] = u_scr[rows:rows + head, :]


def _ffn_up_kernel(x_ref, g_ref, w_ref, cw_ref, cb_ref, o_ref, h_scr, carry_scr, u_scr):
    _normalize_once(x_ref, g_ref, h_scr)
    _conv_proj_steps(h_scr, w_ref, cw_ref, o_ref, carry_scr, u_scr, piece_rows=h_scr.shape[0] // 2,
                     piece_cols=o_ref.shape[1], w_transposed=False,
                     finish=lambda y, jj, cols: y + cb_ref[:, cols])


def _gdn_conv_kernel(x_ref, g_ref, w_ref, cw_ref, o_ref, h_scr, carry_scr, u_scr, *, n_q_tiles, n_qk_tiles, head_dim,
                     q_scale):
    _normalize_once(x_ref, g_ref, h_scr)

    def finish(y, jj, cols):
        y = _silu(y)
        l2_scale = jnp.where(jj < n_q_tiles, q_scale, 1.0)
        heads = []
        for c in range(0, y.shape[1], head_dim):
            blk = y[:, c:c + head_dim]
            inv = lax.rsqrt(jnp.sum(blk * blk, axis=-1, keepdims=True) + RMS_EPS) * l2_scale
            heads.append(blk * jnp.where(jj < n_qk_tiles, inv, 1.0))
        return jnp.concatenate(heads, axis=1)

    _conv_proj_steps(h_scr, w_ref, cw_ref, o_ref, carry_scr, u_scr, piece_rows=ROW_SUBTILE,
                     piece_cols=o_ref.shape[1], w_transposed=True, finish=finish)


def _proj_call(name, kernel, x, gain, w3, layer, n_cols, extra_inputs, extra_specs, out_dtype, scratch,
               col_tile=COL_TILE, first_col_tile=0, transposed_tile=None, w_transposed=False):
    s_len, d = x.shape
    tm, tn = min(ROW_TILE, s_len), col_tile
    grid = (s_len // tm, pl.cdiv(n_cols, tn))
    if w_transposed:
        w_spec = pl.BlockSpec((None, tn, d), lambda i, j: (layer, first_col_tile + j, 0))
    else:
        w_spec = pl.BlockSpec((None, d, tn), lambda i, j: (layer, 0, first_col_tile + j))
    in_specs = [pl.BlockSpec((tm, d), lambda i, j: (i, 0)), pl.BlockSpec((1, d), lambda i, j: (0, 0)), w_spec]
    in_specs += list(extra_specs)
    if transposed_tile is None:
        out_shape = jax.ShapeDtypeStruct((s_len, n_cols), out_dtype)
        out_spec = pl.BlockSpec((tm, tn), lambda i, j: (i, j))
    else:
        out_shape = jax.ShapeDtypeStruct((s_len // transposed_tile, n_cols, transposed_tile), out_dtype)
        out_spec = pl.BlockSpec((tm // transposed_tile, tn, transposed_tile), lambda i, j: (i, j, 0))
    out_bytes = jnp.dtype(out_dtype).itemsize
    block_bytes = 2 * tm * d * 4 + tm * d * 2 + 2 * d * tn * 4 + d * tn * 2 + 2 * tm * tn * out_bytes + 3 * tm * tn * 4
    return pl.pallas_call(
        kernel,
        name=name,
        out_shape=out_shape,
        grid=grid,
        in_specs=in_specs,
        out_specs=out_spec,
        scratch_shapes=[pltpu.VMEM((tm, d), BF16)] + list(scratch),
        compiler_params=_params(("arbitrary", "arbitrary"), block_bytes),
    )(x, gain.reshape(1, d), w3, *extra_inputs)


def _conv_scratch(s_len, n_tiles):
    tm = min(ROW_TILE, s_len)
    return (pltpu.VMEM((n_tiles, V7X_SUBLANES, COL_TILE), F32), pltpu.VMEM((V7X_SUBLANES + tm, COL_TILE), F32))


def _mm_res_kernel(a_ref, w_ref, r_ref, o_ref, *w_scr):
    if w_scr:
        @pl.when(pl.program_id(0) == 0)
        def _():
            for r0 in range(0, w_ref.shape[0], COL_TILE):
                w_scr[0][r0:r0 + COL_TILE, :] = w_ref[r0:r0 + COL_TILE, :].astype(BF16)
        w = w_scr[0][...]
    else:
        w = w_ref[...]
    o_ref[...] = r_ref[...] + jnp.dot(a_ref[...], w, preferred_element_type=F32)


def _mm_res_call(a, w, res, layer=None):
    s_len, k = a.shape
    n = w.shape[-1]
    tm = min(OUT_ROW_TILE, s_len)
    once = pl.Buffered(1)
    if w.ndim == 3:
        w_spec = pl.BlockSpec((None, k, n), lambda i: (layer, 0, 0), pipeline_mode=once)
        scratch = [pltpu.VMEM((k, n), BF16)]
    else:
        w_spec = pl.BlockSpec((k, n), lambda i: (0, 0), pipeline_mode=once)
        scratch = []
    block_bytes = 2 * tm * k * 2 + k * n * w.dtype.itemsize + len(scratch) * k * n * 2 + 5 * tm * n * 4
    return pl.pallas_call(
        _mm_res_kernel,
        name="out_proj_residual",
        out_shape=jax.ShapeDtypeStruct((s_len, n), F32),
        grid=(s_len // tm,),
        in_specs=[pl.BlockSpec((tm, k), lambda i: (i, 0)), w_spec, pl.BlockSpec((tm, n), lambda i: (i, 0))],
        out_specs=pl.BlockSpec((tm, n), lambda i: (i, 0)),
        scratch_shapes=scratch,
        compiler_params=_params(("arbitrary",), block_bytes),
    )(a, w, res)


def _ffn_down_kernel(uv_ref, ug_ref, w_ref, r_ref, o_ref, g_scr):
    d_ff = w_ref.shape[0]
    mid = pl.cdiv(d_ff // 2, 2 * V7X_LANES) * 2 * V7X_LANES
    acc = r_ref[...]
    for k0, k1 in ((0, mid), (mid, d_ff)):
        for c0 in range(k0, k1, COL_TILE):
            cols = slice(c0, min(c0 + COL_TILE, k1))
            g_scr[:, cols] = (_silu(ug_ref[:, cols].astype(F32)) * uv_ref[:, cols].astype(F32)).astype(BF16)
        acc = acc + jnp.dot(g_scr[:, k0:k1], w_ref[k0:k1, :], preferred_element_type=F32)
    o_ref[...] = acc


def _ffn_down_call(u, w_down, layer, res):
    s_len = u.shape[0]
    _, d_ff, n = w_down.shape
    tm = min(DOWN_ROW_TILE, s_len)
    block_bytes = 4 * tm * d_ff * 2 + tm * d_ff * 2 + d_ff * n * 2 + 6 * tm * n * 4
    return pl.pallas_call(
        _ffn_down_kernel,
        name="ffn_down_residual",
        out_shape=jax.ShapeDtypeStruct((s_len, n), F32),
        grid=(s_len // tm,),
        in_specs=[
            pl.BlockSpec((tm, d_ff), lambda i: (i, 0)),
            pl.BlockSpec((tm, d_ff), lambda i: (i, 1)),
            pl.BlockSpec((None, d_ff, n), lambda i: (layer, 0, 0), pipeline_mode=pl.Buffered(1)),
            pl.BlockSpec((tm, n), lambda i: (i, 0)),
        ],
        out_specs=pl.BlockSpec((tm, n), lambda i: (i, 0)),
        scratch_shapes=[pltpu.VMEM((tm, d_ff), BF16)],
        compiler_params=_params(("arbitrary",), block_bytes),
    )(u, u, w_down, res)


def _attn_kernel(slopes_ref, qt_ref, k_ref, vt_ref, lq1_ref, lk1_ref, lq2_ref, lk2_ref, subln_ref, o_ref,
                 s_scr, kb_scr, bias_scr, m_scr, l_scr, acc_scr, *, tile, head_dim, lambda_init):
    h, qi = pl.program_id(0), pl.program_id(1)
    slope = slopes_ref[h]
    hd = head_dim

    @pl.when(qi == 0)
    def _():
        key_bias = slope * lax.broadcasted_iota(jnp.int32, (tile, hd), 0).astype(F32)
        lane = lax.broadcasted_iota(jnp.int32, (tile, hd), 1)
        part, parts = key_bias, []
        for _ in range(BIAS_PARTS):
            parts.append(part.astype(BF16).astype(F32))
            part = part - parts[-1]
        kb_cols = jnp.zeros((tile, hd), F32)
        for c in reversed(range(BIAS_PARTS)):
            kb_cols = jnp.where(lane == c, parts[c], kb_cols)
        kb_scr[...] = kb_cols.astype(BF16)
        key = lax.broadcasted_iota(jnp.int32, (tile, tile), 0)
        qry = lax.broadcasted_iota(jnp.int32, (tile, tile), 1)
        bias_scr[...] = jnp.where((key // CHUNK) <= (qry // CHUNK),
                                  -slope * (jnp.abs(key - qry) + key).astype(F32), -jnp.inf)

    ones_rows = jnp.where(lax.broadcasted_iota(jnp.int32, (hd, tile), 0) < BIAS_PARTS, 1.0, 0.0).astype(BF16)

    def issue_scores(kj, slot):
        k_blk = k_ref[pl.ds(pl.multiple_of(kj * tile, tile), tile), :]
        for m in range(2):
            rows = slice(m * hd, (m + 1) * hd)
            s_scr[slot, m] = jnp.dot(jnp.concatenate([k_blk[:, rows], kb_scr[...]], axis=1),
                                     jnp.concatenate([qt_ref[rows, :], ones_rows], axis=0),
                                     preferred_element_type=F32)

    issue_scores(qi, 1)
    issue_scores(0, 0)

    vt_diag = vt_ref[qi]
    for m in range(2):
        z = s_scr[1, m] + bias_scr[...]
        mx = jnp.max(z, axis=0, keepdims=True)
        p = jnp.exp2(z - mx)
        m_scr[m] = mx
        l_scr[m] = jnp.sum(p, axis=0, keepdims=True)
        acc_scr[m] = jnp.dot(vt_diag, p.astype(BF16), preferred_element_type=F32)

    q_pos = (lax.broadcasted_iota(jnp.int32, (1, tile), 1) + qi * tile).astype(F32)

    def kv_step(kj, slot, issue_next):
        if issue_next:
            issue_scores(jnp.minimum(kj + 1, qi - 1), 1 - slot)
        c = -slope * (q_pos - (kj * tile).astype(F32))
        vt_blk = vt_ref[kj]
        for m in range(2):
            z = s_scr[slot, m]
            m_old = m_scr[m]
            m_new = jnp.maximum(m_old, jnp.max(z, axis=0, keepdims=True) + c)
            p = jnp.exp2(z - (m_new - c))
            alpha = jnp.exp2(m_old - m_new)
            l_scr[m] = alpha * l_scr[m] + jnp.sum(p, axis=0, keepdims=True)
            acc_scr[m] = alpha * acc_scr[m] + jnp.dot(vt_blk, p.astype(BF16), preferred_element_type=F32)
            m_scr[m] = m_new

    def two_steps(t, carry):
        kv_step(2 * t, 0, True)
        kv_step(2 * t + 1, 1, True)
        return carry

    lax.fori_loop(0, qi // 2, two_steps, 0)

    @pl.when(lax.rem(qi, 2) == 1)
    def _():
        kv_step(qi - 1, 0, False)

    lam = (jnp.exp(jnp.sum(lq1_ref[...] * lk1_ref[...], axis=-1, keepdims=True))
           - jnp.exp(jnp.sum(lq2_ref[...] * lk2_ref[...], axis=-1, keepdims=True)) + lambda_init)
    o_t = acc_scr[0] / l_scr[0] - lam * (acc_scr[1] / l_scr[1])
    inv = lax.rsqrt(jnp.mean(o_t * o_t, axis=0, keepdims=True) + RMS_EPS)
    o_ref[...] = ((o_t * inv * subln_ref[...]) * (1.0 - lambda_init)).T.astype(o_ref.dtype)


def _attn_call(qt, k, vt, slopes, lq1, lk1, lq2, lk2, subln, *, n_heads, head_dim, lambda_init):
    n_tiles, _, tile = qt.shape
    s_len = k.shape[0]
    hw = 2 * head_dim
    vec = lambda a: a.reshape(1, -1)
    vec_spec = lambda n: pl.BlockSpec((1, n), lambda h, qi: (0, 0))
    block_bytes = 2 * (2 * s_len * hw * 2) + 4 * tile * hw * 2 + 2 * tile * hw * 4 + 5 * tile * tile * 4 \
        + 6 * tile * tile * 4
    return pl.pallas_call(
        functools.partial(_attn_kernel, tile=tile, head_dim=head_dim, lambda_init=lambda_init),
        name="diff_attention",
        out_shape=jax.ShapeDtypeStruct((s_len, n_heads * hw), BF16),
        grid=(n_heads, n_tiles),
        in_specs=[
            pl.BlockSpec(memory_space=pltpu.SMEM),
            pl.BlockSpec((None, hw, tile), lambda h, qi: (qi, h, 0)),
            pl.BlockSpec((s_len, hw), lambda h, qi: (0, h)),
            pl.BlockSpec((n_tiles, hw, tile), lambda h, qi: (0, h, 0)),
            vec_spec(head_dim), vec_spec(head_dim), vec_spec(head_dim), vec_spec(head_dim),
            pl.BlockSpec((hw, 1), lambda h, qi: (0, 0)),
        ],
        out_specs=pl.BlockSpec((tile, hw), lambda h, qi: (qi, h)),
        scratch_shapes=[
            pltpu.VMEM((2, 2, tile, tile), F32),
            pltpu.VMEM((tile, head_dim), BF16),
            pltpu.VMEM((tile, tile), F32),
            pltpu.VMEM((2, 1, tile), F32),
            pltpu.VMEM((2, 1, tile), F32),
            pltpu.VMEM((2, hw, tile), F32),
        ],
        compiler_params=_params(("arbitrary", "arbitrary"), block_bytes),
    )(slopes, qt, k, vt, vec(lq1), vec(lk1), vec(lq2), vec(lk2), subln.reshape(-1, 1))


def _gates_kernel(ba_ref, alog_ref, dtb_ref, beta_ref, gc_ref):
    rows = ba_ref.shape[0]
    beta_ref[...] = jax.nn.sigmoid(ba_ref[...])
    g = -jnp.exp(alog_ref[...]) * jax.nn.softplus(ba_ref[...] + dtb_ref[...])
    r = lax.broadcasted_iota(jnp.int32, (rows, rows), 0)
    c = lax.broadcasted_iota(jnp.int32, (rows, rows), 1)
    tri = jnp.where(r // CHUNK == c // CHUNK, jnp.where(c <= r, 1.0, 0.0), 0.0).astype(F32)
    gc_ref[...] = jnp.dot(tri, g, preferred_element_type=F32, precision=lax.Precision.HIGHEST)


def _gates_call(ba, a_log_row, dt_bias_row):
    s_len, width = ba.shape
    tm = min(GATE_ROWS, s_len)
    out = jax.ShapeDtypeStruct((s_len, width), F32)
    row_spec = pl.BlockSpec((tm, width), lambda i: (i, 0))
    vec_spec = pl.BlockSpec((1, width), lambda i: (0, 0))
    return pl.pallas_call(
        _gates_kernel,
        name="gdn_gates",
        out_shape=(out, out),
        grid=(s_len // tm,),
        in_specs=[row_spec, vec_spec, vec_spec],
        out_specs=(row_spec, row_spec),
        compiler_params=_params(("arbitrary",), 8 * tm * 2 * V7X_LANES * 4 + 4 * tm * tm * 4),
    )(ba, a_log_row, dt_bias_row)


def _split_bf16(x):
    hi = x.astype(BF16)
    return hi, (x - hi.astype(F32)).astype(BF16)


def _delta_kernel(q_ref, k_ref, v_ref, z_ref, beta_ref, gc_ref, gct_ref, ng_ref, o_ref, state_scr, lmat_scr,
                  qkd_scr, qdec_scr, kdec_scr, *,
                  n_pairs, n_chunks, head_dim):
    hd = head_dim
    pairs = range(n_pairs)

    @pl.when(pl.program_id(1) == 0)
    def _():
        state_scr[...] = jnp.zeros(state_scr.shape, F32)

    ri = lax.broadcasted_iota(jnp.int32, (CHUNK, 2 * CHUNK), 0)
    lane = lax.broadcasted_iota(jnp.int32, (CHUNK, 2 * CHUNK), 1)
    left = lane < CHUNK
    ci = jnp.where(left, lane, lane - CHUNK)
    tril = ci <= ri
    strict = ci < ri
    eye = jnp.where(ci == ri, 1.0, 0.0).astype(F32)
    state_left = lax.broadcasted_iota(jnp.int32, (hd, 2 * hd), 1) < hd

    def block_diag(y, keep_left):
        zero = jnp.zeros_like(y)
        return jnp.concatenate([jnp.where(keep_left, y, zero), jnp.where(keep_left, zero, y)], axis=0)

    def pair_matmul(xh, xl, yh, yl):
        bh, bl = block_diag(yh, left), block_diag(yl, left)
        rhs = jnp.concatenate([jnp.concatenate([bh, bl], axis=1),
                               jnp.concatenate([bh, jnp.zeros_like(bl)], axis=1)], axis=0)
        out = jnp.dot(jnp.concatenate([xh, xl], axis=1), rhs, preferred_element_type=F32)
        return out[:, :2 * CHUNK] + out[:, 2 * CHUNK:]

    lanes = lambda a, h: jnp.broadcast_to(a[:, h:h + 1], (CHUNK, hd))

    def prepare_pair(c, p):
        rows = pl.ds(pl.multiple_of(c * CHUNK, CHUNK), CHUNK)
        beta, g_col, g_row = beta_ref[rows, :], gc_ref[rows, :], gct_ref[0, c][p:p + 1, :]
        k, q = k_ref[rows, p * hd:(p + 1) * hd], q_ref[rows, p * hd:(p + 1) * hd]
        kq = lax.dot_general(jnp.concatenate([k, q], axis=0), jnp.concatenate([k, k], axis=0), _NT,
                             preferred_element_type=F32)
        g_a, g_b = lanes(g_col, 2 * p), lanes(g_col, 2 * p + 1)
        b_pk = jnp.where(left, lanes(beta, 2 * p), lanes(beta, 2 * p + 1))
        decay = jnp.exp(jnp.where(tril, jnp.where(left, g_a, g_b) - g_row, -jnp.inf))
        lmat_scr[p] = jnp.where(strict, kq[:CHUNK] * b_pk * decay, 0.0)
        qkd_scr[p] = (kq[CHUNK:] * decay).astype(BF16)
        qf, kf = q.astype(F32), k.astype(F32)
        qdec_scr[p] = jnp.concatenate([qf * jnp.exp(g_a), qf * jnp.exp(g_b)], axis=1).astype(BF16)
        ge_a, ge_b = g_row[:, CHUNK - 1:CHUNK], g_row[:, 2 * CHUNK - 1:2 * CHUNK]
        kdec_scr[p] = jnp.concatenate([kf * jnp.exp(ge_a - g_a), kf * jnp.exp(ge_b - g_b)], axis=1).astype(BF16)

    for p in pairs:
        prepare_pair(0, p)

    levels = []
    power = 2
    while power * 2 < CHUNK:
        levels.append(power)
        power *= 2
    pairs_per_level = -(-n_pairs // len(levels))

    def chunk_step(c, carry):
        rows = pl.ds(pl.multiple_of(c * CHUNK, CHUNK), CHUNK)
        g_rows = gct_ref[0, c]
        b_rows = gct_ref[1, c]
        g_row = [g_rows[p:p + 1, :] for p in pairs]
        k = [k_ref[rows, p * hd:(p + 1) * hd] for p in pairs]
        lmat = [lmat_scr[p] for p in pairs]
        qk_decay = [qkd_scr[p] for p in pairs]
        q_dec = [qdec_scr[p] for p in pairs]
        k_dec = [(kdec_scr[p][:, :hd], kdec_scr[p][:, hd:]) for p in pairs]
        c_next = jnp.minimum(c + 1, n_chunks - 1)
        lsplit = [_split_bf16(lmat[p]) for p in pairs]
        x = [pair_matmul(*lsplit[p], *lsplit[p]) for p in pairs]
        acc = [eye - lmat[p] for p in pairs]
        for level in range(len(levels)):
            for p in pairs:
                sh, sl = _split_bf16(jnp.concatenate([acc[p], x[p]], axis=0))
                both = pair_matmul(sh, sl, sh[CHUNK:], sl[CHUNK:])
                acc[p] = acc[p] + both[:CHUNK]
                x[p] = both[CHUNK:]
                if p // pairs_per_level == level:
                    prepare_pair(c_next, p)
        tmat = [acc[p] + pair_matmul(*_split_bf16(acc[p]), *_split_bf16(x[p])) for p in pairs]
        zb = jnp.zeros((CHUNK, hd), BF16)
        uw = []
        for p in pairs:
            b_row = b_rows[p:p + 1, :]
            t_scaled = jnp.concatenate([tmat[p] * b_row, tmat[p] * (b_row * jnp.exp(g_row[p]))], axis=1)
            v_pk = v_ref[rows, 2 * p * hd:2 * (p + 1) * hd]
            rhs = jnp.concatenate([jnp.concatenate([v_pk[:, :hd], zb, zb, zb], axis=1),
                                   jnp.concatenate([zb, v_pk[:, hd:], zb, zb], axis=1),
                                   jnp.concatenate([zb, zb, k[p], zb], axis=1),
                                   jnp.concatenate([zb, zb, zb, k[p]], axis=1)], axis=0)
            uw.append(jnp.dot(t_scaled.astype(BF16), rhs, preferred_element_type=F32))
        s_old = [state_scr[p] for p in pairs]
        wq = []
        for p in pairs:
            wq.append(jnp.dot(jnp.concatenate([uw[p][:, 2 * hd:].astype(BF16), q_dec[p]], axis=0),
                              block_diag(s_old[p].astype(BF16), state_left), preferred_element_type=F32))
        o_pk = []
        for p in pairs:
            vn = (uw[p][:, :2 * hd] - wq[p][:CHUNK]).astype(BF16)
            vn_diag = jnp.concatenate([jnp.concatenate([vn[:, :hd], zb], axis=1),
                                       jnp.concatenate([zb, vn[:, hd:]], axis=1)], axis=0)
            o_pk.append(wq[p][CHUNK:] + jnp.dot(qk_decay[p], vn_diag, preferred_element_type=F32))
            ge_a, ge_b = g_row[p][:, CHUNK - 1:CHUNK], g_row[p][:, 2 * CHUNK - 1:2 * CHUNK]
            ds_a = lax.dot_general(k_dec[p][0], vn[:, :hd], _TN, preferred_element_type=F32)
            ds_b = lax.dot_general(k_dec[p][1], vn[:, hd:], _TN, preferred_element_type=F32)
            state_scr[p] = jnp.concatenate([s_old[p][:, :hd] * jnp.exp(ge_a) + ds_a,
                                            s_old[p][:, hd:] * jnp.exp(ge_b) + ds_b], axis=1)
        for p in pairs:
            wide_cols = slice(2 * p * hd, 2 * (p + 1) * hd)
            z_pk = z_ref[rows, wide_cols].astype(F32)
            gated = [_rms_rows(o_pk[p][:, h * hd:(h + 1) * hd], ng_ref[...]) * _silu(z_pk[:, h * hd:(h + 1) * hd])
                     for h in range(2)]
            o_ref[rows, wide_cols] = jnp.concatenate(gated, axis=1).astype(o_ref.dtype)
        return carry

    lax.fori_loop(0, n_chunks, chunk_step, 0)


def _delta_call(qkv, z, beta_g, gc_g, gct_g, norm_g, *, n_qk_heads, head_dim):
    s_len = qkv.shape[0]
    n_groups = n_qk_heads // DELTA_PAIRS
    heads = 2 * DELTA_PAIRS
    rows = DELTA_CHUNKS * CHUNK
    qk_w, v_w = DELTA_PAIRS * head_dim, heads * head_dim
    key_dim, value_dim = n_qk_heads * head_dim, 2 * n_qk_heads * head_dim
    k_off, v_off = key_dim // qk_w, 2 * key_dim // v_w
    block_bytes = 2 * rows * (2 * qk_w + 3 * v_w) * 2 + 6 * rows * V7X_LANES * 4 + heads * head_dim * head_dim * 4
    return pl.pallas_call(
        functools.partial(_delta_kernel, n_pairs=DELTA_PAIRS, n_chunks=DELTA_CHUNKS, head_dim=head_dim),
        name="gdn_delta_rule",
        out_shape=jax.ShapeDtypeStruct((s_len, value_dim), BF16),
        grid=(n_groups, s_len // rows),
        in_specs=[
            pl.BlockSpec((rows, qk_w), lambda g, b: (b, g)),
            pl.BlockSpec((rows, qk_w), lambda g, b: (b, k_off + g)),
            pl.BlockSpec((rows, v_w), lambda g, b: (b, v_off + g)),
            pl.BlockSpec((rows, v_w), lambda g, b: (b, g)),
            pl.BlockSpec((None, rows, heads), lambda g, b: (g, b, 0)),
            pl.BlockSpec((None, rows, heads), lambda g, b: (g, b, 0)),
            pl.BlockSpec((None, 2, DELTA_CHUNKS, DELTA_PAIRS, 2 * CHUNK), lambda g, b: (g, 0, b, 0, 0)),
            pl.BlockSpec((1, head_dim), lambda g, b: (0, 0)),
        ],
        out_specs=pl.BlockSpec((rows, v_w), lambda g, b: (b, g)),
        scratch_shapes=[pltpu.VMEM((DELTA_PAIRS, head_dim, 2 * head_dim), F32),
                        pltpu.VMEM((DELTA_PAIRS, CHUNK, 2 * CHUNK), F32),
                        pltpu.VMEM((DELTA_PAIRS, CHUNK, 2 * CHUNK), BF16),
                        pltpu.VMEM((DELTA_PAIRS, CHUNK, 2 * head_dim), BF16),
                        pltpu.VMEM((DELTA_PAIRS, CHUNK, 2 * head_dim), BF16)],
        compiler_params=_params(("arbitrary", "arbitrary"), block_bytes),
    )(qkv, qkv, qkv, z, beta_g, gc_g, gct_g, norm_g.reshape(1, head_dim))


def _diff_attention_layer(x, gain, w_qkv, layer, q_norm, k_norm, lq1, lk1, lq2, lk2, subln, w_o, lambda_init):
    s_len = x.shape[0]
    head_dim = q_norm.shape[0]
    n_heads = w_o.shape[1] // subln.shape[0]
    qk_dim = 2 * n_heads * head_dim
    v_dim = w_qkv.shape[2] - 2 * qk_dim
    tile = min(ATTN_TILE, s_len)
    vec_spec = pl.BlockSpec((1, head_dim), lambda i, j: (0, 0))

    def proj(name, n_cols, first_col, head_gain, scale, transpose):
        head_norm = ((head_gain.reshape(1, -1),), (vec_spec,)) if head_gain is not None else ((), ())
        return _proj_call(
            name, functools.partial(_head_proj_kernel, head_dim=head_dim, scale=scale, transpose=transpose, tile=tile,
                                    w_transposed=False, piece_rows=ROW_SUBTILE),
            x, gain, w_qkv, layer, n_cols, *head_norm, BF16, (),
            col_tile=PLAIN_COL_TILE, first_col_tile=first_col // PLAIN_COL_TILE,
            transposed_tile=tile if transpose else None)

    qt = proj("attn_q_proj", qk_dim, 0, q_norm, head_dim ** -0.5 * LOG2_E, True)
    k = proj("attn_k_proj", qk_dim, qk_dim, k_norm, 1.0, False)
    vt = proj("attn_v_proj", v_dim, 2 * qk_dim, None, 1.0, True)
    slopes = jnp.exp2(-8.0 * jnp.arange(1, n_heads + 1, dtype=F32) / n_heads) * LOG2_E
    o = _attn_call(qt, k, vt, slopes, lq1, lk1, lq2, lk2, subln, n_heads=n_heads, head_dim=head_dim,
                   lambda_init=lambda_init)
    return _mm_res_call(o, w_o, x, layer)


def _gated_deltanet_layer(x, gain, w_in, layer, conv_w, a_log, dt_bias, norm_g, w_out):
    s_len, d = x.shape
    head_dim = norm_g.shape[0]
    n_v_heads = a_log.shape[0]
    n_qk_heads = n_v_heads // 2
    key_dim, value_dim = n_qk_heads * head_dim, n_v_heads * head_dim
    conv_dim = 2 * key_dim + value_dim
    main_cols = conv_dim + value_dim
    n_conv_tiles = conv_dim // COL_TILE
    w_in_t = jnp.swapaxes(w_in, 1, 2)
    cw_spec = pl.BlockSpec((None, conv_w.shape[1], COL_TILE), lambda i, j: (layer, 0, j))
    qkv = _proj_call(
        "gdn_in_proj", functools.partial(_gdn_conv_kernel, n_q_tiles=key_dim // COL_TILE,
                                         n_qk_tiles=2 * key_dim // COL_TILE, head_dim=head_dim,
                                         q_scale=head_dim ** -0.5),
        x, gain, w_in_t, layer, conv_dim, (conv_w,), (cw_spec,), BF16, _conv_scratch(s_len, n_conv_tiles),
        w_transposed=True)
    z = _proj_call(
        "gdn_z_proj", functools.partial(_head_proj_kernel, head_dim=head_dim, scale=1.0, transpose=False, tile=None,
                                        w_transposed=True, piece_rows=min(ROW_TILE, s_len)),
        x, gain, w_in_t, layer, value_dim, (), (), BF16, (), col_tile=PLAIN_COL_TILE,
        first_col_tile=conv_dim // PLAIN_COL_TILE, w_transposed=True)

    gate_cols = 2 * n_v_heads
    assert main_cols % gate_cols == 0
    ba = _proj_call("gdn_gate_proj", functools.partial(_plain_proj_kernel, w_transposed=True), x, gain, w_in_t, layer,
                    gate_cols, (), (), F32, (), col_tile=gate_cols, first_col_tile=main_cols // gate_cols,
                    w_transposed=True)
    on_a_lanes = lambda v: jnp.pad(v, (n_v_heads, 0)).reshape(1, -1)
    beta, gc = _gates_call(ba, on_a_lanes(a_log), on_a_lanes(dt_bias))
    beta, gc = beta[:, :n_v_heads], gc[:, n_v_heads:2 * n_v_heads]

    heads = 2 * DELTA_PAIRS
    n_groups = n_v_heads // heads
    beta_g = beta.reshape(s_len, n_groups, heads).transpose(1, 0, 2)
    gc_g = gc.reshape(s_len, n_groups, heads).transpose(1, 0, 2)
    rows_of = lambda a: a.reshape(s_len // CHUNK, CHUNK, n_groups, DELTA_PAIRS, 2).transpose(
        2, 0, 3, 4, 1).reshape(n_groups, s_len // CHUNK, DELTA_PAIRS, 2 * CHUNK)
    gct_g = jnp.stack([rows_of(gc), rows_of(beta)], axis=1)
    o = _delta_call(qkv, z, beta_g, gc_g, gct_g, norm_g, n_qk_heads=n_qk_heads, head_dim=head_dim)
    return _mm_res_call(o, w_out[layer].astype(BF16), x)


def _conv_ffn_layer(x, gain, w_up, conv_w, conv_b, w_down, layer):
    n_cols = w_up.shape[2]
    n_tiles = pl.cdiv(n_cols, COL_TILE)
    tile_spec = lambda rows: pl.BlockSpec((None, rows, COL_TILE), lambda i, j: (layer, 0, j))
    u = _proj_call("ffn_up_proj", _ffn_up_kernel, x, gain, w_up, layer, n_cols, (conv_w, conv_b[:, None, :]),
                   (tile_spec(conv_w.shape[1]), tile_spec(1)), BF16, _conv_scratch(x.shape[0], n_tiles))
    return _ffn_down_call(u, w_down.astype(BF16), layer, x)


def kernel(x, mixer_norm, ffn_norm, diff_w_qkv, diff_q_norm, diff_k_norm, diff_lambda_q1, diff_lambda_k1,
           diff_lambda_q2, diff_lambda_k2, diff_subln, diff_w_o, gdn_w_in, gdn_conv_w, gdn_A_log, gdn_dt_bias,
           gdn_norm, gdn_w_out, ffn_w_up, ffn_conv_w, ffn_conv_b, ffn_w_down):
    b_sz, s_len, d = x.shape
    depth = mixer_norm.shape[0]
    outs = []
    for b in range(b_sz):
        xb = x.reshape(s_len, d) if b_sz == 1 else x[b]
        for i in range(depth):
            j = i // 2
            if i % 2 == 0:
                lambda_init = 0.8 - 0.6 * math.exp(-0.3 * i)
                xb = _diff_attention_layer(xb, mixer_norm[i], diff_w_qkv, j, diff_q_norm[j], diff_k_norm[j],
                                           diff_lambda_q1[j], diff_lambda_k1[j], diff_lambda_q2[j],
                                           diff_lambda_k2[j], diff_subln[j], diff_w_o, lambda_init)
            else:
                xb = _gated_deltanet_layer(xb, mixer_norm[i], gdn_w_in, j, gdn_conv_w, gdn_A_log[j], gdn_dt_bias[j],
                                           gdn_norm[j], gdn_w_out)
            xb = _conv_ffn_layer(xb, ffn_norm[i], ffn_w_up, ffn_conv_w, ffn_conv_b, ffn_w_down, i)
        outs.append(xb)
    return outs[0].reshape(x.shape) if b_sz == 1 else jnp.stack(outs, axis=0)
```

```python
import functools
import math

import jax
import jax.numpy as jnp
from jax import lax
from jax.experimental import pallas as pl
from jax.experimental.pallas import tpu as pltpu

F32 = jnp.float32
BF16 = jnp.bfloat16

RMS_EPS = 1e-6
LOG2_E = math.log2(math.e)
BIAS_PARTS = 3
CHUNK = 64
V7X_VMEM_BYTES = 64 * 1024 * 1024
V7X_LANES = 128
V7X_SUBLANES = 8
COMPILER_TEMP_BYTES = 12 * 1024 * 1024

ROW_TILE = 1024
ROW_SUBTILE = 256
OUT_ROW_TILE = 512
DOWN_ROW_TILE = 256
COL_TILE = 1024
PLAIN_COL_TILE = 1024
ATTN_TILE = 512
DELTA_PAIRS = 16
DELTA_CHUNKS = 8
GATE_ROWS = 256

_NT = (((1,), (1,)), ((), ()))
_TN = (((0,), (0,)), ((), ()))


def _params(semantics, block_bytes):
    limit = min(int(block_bytes) + COMPILER_TEMP_BYTES, V7X_VMEM_BYTES - (4 << 20))
    return pltpu.CompilerParams(dimension_semantics=semantics, vmem_limit_bytes=limit)


def _rms_rows(x, gain):
    return x * lax.rsqrt(jnp.mean(x * x, axis=-1, keepdims=True) + RMS_EPS) * gain


def _silu(x):
    return x * jax.nn.sigmoid(x)


def _matmul(h, w_bf16, w_transposed):
    if w_transposed:
        return lax.dot_general(h, w_bf16, _NT, preferred_element_type=F32)
    return jnp.dot(h, w_bf16, preferred_element_type=F32)


def _normalize_once(x_ref, g_ref, h_scr):
    @pl.when(pl.program_id(1) == 0)
    def _():
        h_scr[...] = _rms_rows(x_ref[...], g_ref[...]).astype(BF16)


def _first_tile_or_later(step):
    @pl.when(pl.program_id(1) == 0)
    def _():
        step(True)

    @pl.when(pl.program_id(1) != 0)
    def _():
        step(False)


def _normalize_rows(x_ref, g_ref, h_scr, r0, n_rows):
    h_scr[r0:r0 + n_rows, :] = _rms_rows(x_ref[r0:r0 + n_rows, :], g_ref[...]).astype(BF16)


def _pipelined_rows(h_scr, w_bf16, w_transposed, piece_rows, epilogue, before_piece=None):
    n_sub = h_scr.shape[0] // piece_rows

    def piece(r):
        if before_piece is not None:
            before_piece(r)
        return _matmul(h_scr[r * piece_rows:(r + 1) * piece_rows, :], w_bf16, w_transposed)

    u_prev = piece(0)
    for r in range(1, n_sub):
        u_next = piece(r)
        epilogue(r - 1, u_prev)
        u_prev = u_next
    epilogue(n_sub - 1, u_prev)


def _head_proj_kernel(x_ref, g_ref, w_ref, *refs, head_dim, scale, transpose, tile, w_transposed, piece_rows):
    normalize = len(refs) == 3
    gain_ref = refs[0] if normalize else None
    o_ref, h_scr = refs[-2:]

    def epilogue(r, acc):
        if normalize:
            acc = jnp.concatenate([_rms_rows(acc[:, c:c + head_dim], gain_ref[...]) * scale
                                   for c in range(0, acc.shape[1], head_dim)], axis=1)
        r0 = r * piece_rows
        if transpose:
            o_ref[r0 // tile, :, r0 % tile:r0 % tile + piece_rows] = acc.T.astype(o_ref.dtype)
        else:
            o_ref[r0:r0 + piece_rows, :] = acc.astype(o_ref.dtype)

    def step(first):
        fill = (lambda r: _normalize_rows(x_ref, g_ref, h_scr, r * piece_rows, piece_rows)) if first else None
        _pipelined_rows(h_scr, w_ref[...].astype(BF16), w_transposed, piece_rows, epilogue, fill)

    _first_tile_or_later(step)


def _plain_proj_kernel(x_ref, g_ref, w_ref, o_ref, h_scr, *, w_transposed):
    _normalize_once(x_ref, g_ref, h_scr)
    o_ref[...] = _matmul(h_scr[...], w_ref[...].astype(BF16), w_transposed).astype(o_ref.dtype)


def _conv_proj_steps(x_ref, g_ref, h_scr, w_ref, cw_ref, o_ref, carry_scr, u_scr, *, piece_rows, piece_cols,
                     w_transposed, finish):
    i, j = pl.program_id(0), pl.program_id(1)
    rows, kw = h_scr.shape[0], cw_ref.shape[0]
    head = V7X_SUBLANES

    @pl.when(jnp.logical_and(i == 0, j == 0))
    def _():
        carry_scr[...] = jnp.zeros(carry_scr.shape, F32)

    u_scr[0:head, :] = carry_scr[j]
    pieces = [(r0, c0) for r0 in range(0, rows, piece_rows) for c0 in range(0, o_ref.shape[1], piece_cols)]

    def matmul_piece(first, w_bf16, r0, c0):
        if first and c0 == 0:
            _normalize_rows(x_ref, g_ref, h_scr, r0, piece_rows)
        cols = slice(c0, c0 + piece_cols)
        w_piece = w_bf16[cols, :] if w_transposed else w_bf16[:, cols]
        u_scr[head + r0:head + r0 + piece_rows, cols] = _matmul(h_scr[r0:r0 + piece_rows, :], w_piece, w_transposed)

    def epilogue_piece(p0, c0):
        cols = slice(c0, c0 + piece_cols)
        for r0 in range(p0, p0 + piece_rows, ROW_SUBTILE):
            y = u_scr[head + r0:head + r0 + ROW_SUBTILE, cols] * cw_ref[kw - 1:kw, cols]
            for s in range(1, kw):
                y = y + u_scr[head + r0 - s:head + r0 - s + ROW_SUBTILE, cols] * cw_ref[kw - 1 - s:kw - s, cols]
            o_ref[r0:r0 + ROW_SUBTILE, cols] = finish(y, j, cols).astype(o_ref.dtype)

    def step(first):
        w_bf16 = w_ref[...].astype(BF16)
        matmul_piece(first, w_bf16, *pieces[0])
        for prev, cur in zip(pieces[:-1], pieces[1:]):
            matmul_piece(first, w_bf16, *cur)
            epilogue_piece(*prev)
        epilogue_piece(*pieces[-1])
        carry_scr[j] = u_scr[rows:rows + head, :]

    _first_tile_or_later(step)


def _ffn_up_kernel(x_ref, g_ref, w_ref, cw_ref, cb_ref, o_ref, h_scr, carry_scr, u_scr):
    _conv_proj_steps(x_ref, g_ref, h_scr, w_ref, cw_ref, o_ref, carry_scr, u_scr, piece_rows=h_scr.shape[0] // 2,
                     piece_cols=o_ref.shape[1], w_transposed=False,
                     finish=lambda y, jj, cols: y + cb_ref[:, cols])


def _gdn_conv_kernel(x_ref, g_ref, w_ref, cw_ref, o_ref, h_scr, carry_scr, u_scr, *, n_q_tiles, n_qk_tiles, head_dim,
                     q_scale):
    def finish(y, jj, cols):
        y = _silu(y)
        l2_scale = jnp.where(jj < n_q_tiles, q_scale, 1.0)
        heads = []
        for c in range(0, y.shape[1], head_dim):
            blk = y[:, c:c + head_dim]
            inv = lax.rsqrt(jnp.sum(blk * blk, axis=-1, keepdims=True) + RMS_EPS) * l2_scale
            heads.append(blk * jnp.where(jj < n_qk_tiles, inv, 1.0))
        return jnp.concatenate(heads, axis=1)

    _conv_proj_steps(x_ref, g_ref, h_scr, w_ref, cw_ref, o_ref, carry_scr, u_scr, piece_rows=ROW_SUBTILE,
                     piece_cols=o_ref.shape[1], w_transposed=True, finish=finish)


def _proj_call(name, kernel, x, gain, w3, layer, n_cols, extra_inputs, extra_specs, out_dtype, scratch,
               col_tile=COL_TILE, first_col_tile=0, transposed_tile=None, w_transposed=False):
    s_len, d = x.shape
    tm, tn = min(ROW_TILE, s_len), col_tile
    grid = (s_len // tm, pl.cdiv(n_cols, tn))
    if w_transposed:
        w_spec = pl.BlockSpec((None, tn, d), lambda i, j: (layer, first_col_tile + j, 0))
    else:
        w_spec = pl.BlockSpec((None, d, tn), lambda i, j: (layer, 0, first_col_tile + j))
    in_specs = [pl.BlockSpec((tm, d), lambda i, j: (i, 0)), pl.BlockSpec((1, d), lambda i, j: (0, 0)), w_spec]
    in_specs += list(extra_specs)
    if transposed_tile is None:
        out_shape = jax.ShapeDtypeStruct((s_len, n_cols), out_dtype)
        out_spec = pl.BlockSpec((tm, tn), lambda i, j: (i, j))
    else:
        out_shape = jax.ShapeDtypeStruct((s_len // transposed_tile, n_cols, transposed_tile), out_dtype)
        out_spec = pl.BlockSpec((tm // transposed_tile, tn, transposed_tile), lambda i, j: (i, j, 0))
    out_bytes = jnp.dtype(out_dtype).itemsize
    block_bytes = 2 * tm * d * 4 + tm * d * 2 + 2 * d * tn * 4 + d * tn * 2 + 2 * tm * tn * out_bytes + 3 * tm * tn * 4
    return pl.pallas_call(
        kernel,
        name=name,
        out_shape=out_shape,
        grid=grid,
        in_specs=in_specs,
        out_specs=out_spec,
        scratch_shapes=[pltpu.VMEM((tm, d), BF16)] + list(scratch),
        compiler_params=_params(("arbitrary", "arbitrary"), block_bytes),
    )(x, gain.reshape(1, d), w3, *extra_inputs)


def _conv_scratch(s_len, n_tiles):
    tm = min(ROW_TILE, s_len)
    return (pltpu.VMEM((n_tiles, V7X_SUBLANES, COL_TILE), F32), pltpu.VMEM((V7X_SUBLANES + tm, COL_TILE), F32))


def _mm_res_kernel(a_ref, w_ref, r_ref, o_ref, *w_scr):
    if w_scr:
        @pl.when(pl.program_id(0) == 0)
        def _():
            for r0 in range(0, w_ref.shape[0], COL_TILE):
                w_scr[0][r0:r0 + COL_TILE, :] = w_ref[r0:r0 + COL_TILE, :].astype(BF16)
        w = w_scr[0][...]
    else:
        w = w_ref[...]
    o_ref[...] = r_ref[...] + jnp.dot(a_ref[...], w, preferred_element_type=F32)


def _mm_res_call(a, w, res, layer=None):
    s_len, k = a.shape
    n = w.shape[-1]
    tm = min(OUT_ROW_TILE, s_len)
    once = pl.Buffered(1)
    if w.ndim == 3:
        w_spec = pl.BlockSpec((None, k, n), lambda i: (layer, 0, 0), pipeline_mode=once)
        scratch = [pltpu.VMEM((k, n), BF16)]
    else:
        w_spec = pl.BlockSpec((k, n), lambda i: (0, 0), pipeline_mode=once)
        scratch = []
    block_bytes = 2 * tm * k * 2 + k * n * w.dtype.itemsize + len(scratch) * k * n * 2 + 5 * tm * n * 4
    return pl.pallas_call(
        _mm_res_kernel,
        name="out_proj_residual",
        out_shape=jax.ShapeDtypeStruct((s_len, n), F32),
        grid=(s_len // tm,),
        in_specs=[pl.BlockSpec((tm, k), lambda i: (i, 0)), w_spec, pl.BlockSpec((tm, n), lambda i: (i, 0))],
        out_specs=pl.BlockSpec((tm, n), lambda i: (i, 0)),
        scratch_shapes=scratch,
        compiler_params=_params(("arbitrary",), block_bytes),
    )(a, w, res)


def _ffn_down_kernel(uv_ref, ug_ref, w_ref, r_ref, o_ref, g_scr):
    d_ff = w_ref.shape[0]
    mid = pl.cdiv(d_ff // 2, 2 * V7X_LANES) * 2 * V7X_LANES
    acc = r_ref[...]
    for k0, k1 in ((0, mid), (mid, d_ff)):
        for c0 in range(k0, k1, COL_TILE):
            cols = slice(c0, min(c0 + COL_TILE, k1))
            g_scr[:, cols] = (_silu(ug_ref[:, cols].astype(F32)) * uv_ref[:, cols].astype(F32)).astype(BF16)
        acc = acc + jnp.dot(g_scr[:, k0:k1], w_ref[k0:k1, :], preferred_element_type=F32)
    o_ref[...] = acc


def _ffn_down_call(u, w_down, layer, res):
    s_len = u.shape[0]
    _, d_ff, n = w_down.shape
    tm = min(DOWN_ROW_TILE, s_len)
    block_bytes = 4 * tm * d_ff * 2 + tm * d_ff * 2 + d_ff * n * 2 + 6 * tm * n * 4
    return pl.pallas_call(
        _ffn_down_kernel,
        name="ffn_down_residual",
        out_shape=jax.ShapeDtypeStruct((s_len, n), F32),
        grid=(s_len // tm,),
        in_specs=[
            pl.BlockSpec((tm, d_ff), lambda i: (i, 0)),
            pl.BlockSpec((tm, d_ff), lambda i: (i, 1)),
            pl.BlockSpec((None, d_ff, n), lambda i: (layer, 0, 0), pipeline_mode=pl.Buffered(1)),
            pl.BlockSpec((tm, n), lambda i: (i, 0)),
        ],
        out_specs=pl.BlockSpec((tm, n), lambda i: (i, 0)),
        scratch_shapes=[pltpu.VMEM((tm, d_ff), BF16)],
        compiler_params=_params(("arbitrary",), block_bytes),
    )(u, u, w_down, res)


def _attn_kernel(slopes_ref, qt_ref, k_ref, vt_ref, lq1_ref, lk1_ref, lq2_ref, lk2_ref, subln_ref, o_ref,
                 s_scr, kb_scr, bias_scr, m_scr, l_scr, acc_scr, *, tile, head_dim, lambda_init):
    h, qi = pl.program_id(0), pl.program_id(1)
    slope = slopes_ref[h]
    hd = head_dim

    @pl.when(qi == 0)
    def _():
        key_bias = slope * lax.broadcasted_iota(jnp.int32, (tile, hd), 0).astype(F32)
        lane = lax.broadcasted_iota(jnp.int32, (tile, hd), 1)
        part, parts = key_bias, []
        for _ in range(BIAS_PARTS):
            parts.append(part.astype(BF16).astype(F32))
            part = part - parts[-1]
        kb_cols = jnp.zeros((tile, hd), F32)
        for c in reversed(range(BIAS_PARTS)):
            kb_cols = jnp.where(lane == c, parts[c], kb_cols)
        kb_scr[...] = kb_cols.astype(BF16)
        key = lax.broadcasted_iota(jnp.int32, (tile, tile), 0)
        qry = lax.broadcasted_iota(jnp.int32, (tile, tile), 1)
        bias_scr[...] = jnp.where((key // CHUNK) <= (qry // CHUNK),
                                  -slope * (jnp.abs(key - qry) + key).astype(F32), -jnp.inf)

    ones_rows = jnp.where(lax.broadcasted_iota(jnp.int32, (hd, tile), 0) < BIAS_PARTS, 1.0, 0.0).astype(BF16)

    def issue_scores(kj, slot):
        k_blk = k_ref[pl.ds(pl.multiple_of(kj * tile, tile), tile), :]
        for m in range(2):
            rows = slice(m * hd, (m + 1) * hd)
            s_scr[slot, m] = jnp.dot(jnp.concatenate([k_blk[:, rows], kb_scr[...]], axis=1),
                                     jnp.concatenate([qt_ref[rows, :], ones_rows], axis=0),
                                     preferred_element_type=F32)

    issue_scores(qi, 1)
    issue_scores(0, 0)

    vt_diag = vt_ref[qi]
    for m in range(2):
        z = s_scr[1, m] + bias_scr[...]
        mx = jnp.max(z, axis=0, keepdims=True)
        p = jnp.exp2(z - mx)
        m_scr[m] = mx
        l_scr[m] = jnp.sum(p, axis=0, keepdims=True)
        acc_scr[m] = jnp.dot(vt_diag, p.astype(BF16), preferred_element_type=F32)

    q_pos = (lax.broadcasted_iota(jnp.int32, (1, tile), 1) + qi * tile).astype(F32)

    def kv_step(kj, slot, issue_next):
        if issue_next:
            issue_scores(jnp.minimum(kj + 1, qi - 1), 1 - slot)
        c = -slope * (q_pos - (kj * tile).astype(F32))
        vt_blk = vt_ref[kj]
        for m in range(2):
            z = s_scr[slot, m]
            m_old = m_scr[m]
            m_new = jnp.maximum(m_old, jnp.max(z, axis=0, keepdims=True) + c)
            p = jnp.exp2(z - (m_new - c))
            alpha = jnp.exp2(m_old - m_new)
            l_scr[m] = alpha * l_scr[m] + jnp.sum(p, axis=0, keepdims=True)
            acc_scr[m] = alpha * acc_scr[m] + jnp.dot(vt_blk, p.astype(BF16), preferred_element_type=F32)
            m_scr[m] = m_new

    def two_steps(t, carry):
        kv_step(2 * t, 0, True)
        kv_step(2 * t + 1, 1, True)
        return carry

    lax.fori_loop(0, qi // 2, two_steps, 0)

    @pl.when(lax.rem(qi, 2) == 1)
    def _():
        kv_step(qi - 1, 0, False)

    lam = (jnp.exp(jnp.sum(lq1_ref[...] * lk1_ref[...], axis=-1, keepdims=True))
           - jnp.exp(jnp.sum(lq2_ref[...] * lk2_ref[...], axis=-1, keepdims=True)) + lambda_init)
    o_t = acc_scr[0] / l_scr[0] - lam * (acc_scr[1] / l_scr[1])
    inv = lax.rsqrt(jnp.mean(o_t * o_t, axis=0, keepdims=True) + RMS_EPS)
    o_ref[...] = ((o_t * inv * subln_ref[...]) * (1.0 - lambda_init)).T.astype(o_ref.dtype)


def _attn_call(qt, k, vt, slopes, lq1, lk1, lq2, lk2, subln, *, n_heads, head_dim, lambda_init):
    n_tiles, _, tile = qt.shape
    s_len = k.shape[0]
    hw = 2 * head_dim
    vec = lambda a: a.reshape(1, -1)
    vec_spec = lambda n: pl.BlockSpec((1, n), lambda h, qi: (0, 0))
    block_bytes = 2 * (2 * s_len * hw * 2) + 4 * tile * hw * 2 + 2 * tile * hw * 4 + 5 * tile * tile * 4 \
        + 6 * tile * tile * 4
    return pl.pallas_call(
        functools.partial(_attn_kernel, tile=tile, head_dim=head_dim, lambda_init=lambda_init),
        name="diff_attention",
        out_shape=jax.ShapeDtypeStruct((s_len, n_heads * hw), BF16),
        grid=(n_heads, n_tiles),
        in_specs=[
            pl.BlockSpec(memory_space=pltpu.SMEM),
            pl.BlockSpec((None, hw, tile), lambda h, qi: (qi, h, 0)),
            pl.BlockSpec((s_len, hw), lambda h, qi: (0, h)),
            pl.BlockSpec((n_tiles, hw, tile), lambda h, qi: (0, h, 0)),
            vec_spec(head_dim), vec_spec(head_dim), vec_spec(head_dim), vec_spec(head_dim),
            pl.BlockSpec((hw, 1), lambda h, qi: (0, 0)),
        ],
        out_specs=pl.BlockSpec((tile, hw), lambda h, qi: (qi, h)),
        scratch_shapes=[
            pltpu.VMEM((2, 2, tile, tile), F32),
            pltpu.VMEM((tile, head_dim), BF16),
            pltpu.VMEM((tile, tile), F32),
            pltpu.VMEM((2, 1, tile), F32),
            pltpu.VMEM((2, 1, tile), F32),
            pltpu.VMEM((2, hw, tile), F32),
        ],
        compiler_params=_params(("arbitrary", "arbitrary"), block_bytes),
    )(slopes, qt, k, vt, vec(lq1), vec(lk1), vec(lq2), vec(lk2), subln.reshape(-1, 1))


def _gates_kernel(ba_ref, alog_ref, dtb_ref, beta_ref, gc_ref):
    rows = ba_ref.shape[0]
    beta_ref[...] = jax.nn.sigmoid(ba_ref[...])
    g = -jnp.exp(alog_ref[...]) * jax.nn.softplus(ba_ref[...] + dtb_ref[...])
    r = lax.broadcasted_iota(jnp.int32, (rows, rows), 0)
    c = lax.broadcasted_iota(jnp.int32, (rows, rows), 1)
    tri = jnp.where(r // CHUNK == c // CHUNK, jnp.where(c <= r, 1.0, 0.0), 0.0).astype(F32)
    gc_ref[...] = jnp.dot(tri, g, preferred_element_type=F32, precision=lax.Precision.HIGHEST)


def _gates_call(ba, a_log_row, dt_bias_row):
    s_len, width = ba.shape
    tm = min(GATE_ROWS, s_len)
    out = jax.ShapeDtypeStruct((s_len, width), F32)
    row_spec = pl.BlockSpec((tm, width), lambda i: (i, 0))
    vec_spec = pl.BlockSpec((1, width), lambda i: (0, 0))
    return pl.pallas_call(
        _gates_kernel,
        name="gdn_gates",
        out_shape=(out, out),
        grid=(s_len // tm,),
        in_specs=[row_spec, vec_spec, vec_spec],
        out_specs=(row_spec, row_spec),
        compiler_params=_params(("arbitrary",), 8 * tm * 2 * V7X_LANES * 4 + 4 * tm * tm * 4),
    )(ba, a_log_row, dt_bias_row)


def _split_bf16(x):
    hi = x.astype(BF16)
    return hi, (x - hi.astype(F32)).astype(BF16)


def _delta_kernel(q_ref, k_ref, v_ref, z_ref, beta_ref, gc_ref, gct_ref, ng_ref, o_ref, state_scr, lmat_scr,
                  qkd_scr, qdec_scr, kdec_scr, *,
                  n_pairs, n_chunks, head_dim):
    hd = head_dim
    pairs = range(n_pairs)

    @pl.when(pl.program_id(1) == 0)
    def _():
        state_scr[...] = jnp.zeros(state_scr.shape, F32)

    ri = lax.broadcasted_iota(jnp.int32, (CHUNK, 2 * CHUNK), 0)
    lane = lax.broadcasted_iota(jnp.int32, (CHUNK, 2 * CHUNK), 1)
    left = lane < CHUNK
    ci = jnp.where(left, lane, lane - CHUNK)
    tril = ci <= ri
    strict = ci < ri
    eye = jnp.where(ci == ri, 1.0, 0.0).astype(F32)
    state_left = lax.broadcasted_iota(jnp.int32, (hd, 2 * hd), 1) < hd

    def block_diag(y, keep_left):
        zero = jnp.zeros_like(y)
        return jnp.concatenate([jnp.where(keep_left, y, zero), jnp.where(keep_left, zero, y)], axis=0)

    def pair_matmul(xh, xl, yh, yl):
        bh, bl = block_diag(yh, left), block_diag(yl, left)
        rhs = jnp.concatenate([jnp.concatenate([bh, bl], axis=1),
                               jnp.concatenate([bh, jnp.zeros_like(bl)], axis=1)], axis=0)
        out = jnp.dot(jnp.concatenate([xh, xl], axis=1), rhs, preferred_element_type=F32)
        return out[:, :2 * CHUNK] + out[:, 2 * CHUNK:]

    lanes = lambda a, h: jnp.broadcast_to(a[:, h:h + 1], (CHUNK, hd))

    def prepare_pair(c, p):
        rows = pl.ds(pl.multiple_of(c * CHUNK, CHUNK), CHUNK)
        beta, g_col, g_row = beta_ref[rows, :], gc_ref[rows, :], gct_ref[0, c][p:p + 1, :]
        k, q = k_ref[rows, p * hd:(p + 1) * hd], q_ref[rows, p * hd:(p + 1) * hd]
        kq = lax.dot_general(jnp.concatenate([k, q], axis=0), jnp.concatenate([k, k], axis=0), _NT,
                             preferred_element_type=F32)
        g_a, g_b = lanes(g_col, 2 * p), lanes(g_col, 2 * p + 1)
        b_pk = jnp.where(left, lanes(beta, 2 * p), lanes(beta, 2 * p + 1))
        decay = jnp.exp(jnp.where(tril, jnp.where(left, g_a, g_b) - g_row, -jnp.inf))
        lmat_scr[p] = jnp.where(strict, kq[:CHUNK] * b_pk * decay, 0.0)
        qkd_scr[p] = (kq[CHUNK:] * decay).astype(BF16)
        qf, kf = q.astype(F32), k.astype(F32)
        qdec_scr[p] = jnp.concatenate([qf * jnp.exp(g_a), qf * jnp.exp(g_b)], axis=1).astype(BF16)
        ge_a, ge_b = g_row[:, CHUNK - 1:CHUNK], g_row[:, 2 * CHUNK - 1:2 * CHUNK]
        kdec_scr[p] = jnp.concatenate([kf * jnp.exp(ge_a - g_a), kf * jnp.exp(ge_b - g_b)], axis=1).astype(BF16)

    for p in pairs:
        prepare_pair(0, p)

    levels = []
    power = 2
    while power * 2 < CHUNK:
        levels.append(power)
        power *= 2
    pairs_per_level = -(-n_pairs // len(levels))

    def chunk_step(c, carry):
        rows = pl.ds(pl.multiple_of(c * CHUNK, CHUNK), CHUNK)
        g_rows = gct_ref[0, c]
        b_rows = gct_ref[1, c]
        g_row = [g_rows[p:p + 1, :] for p in pairs]
        k = [k_ref[rows, p * hd:(p + 1) * hd] for p in pairs]
        lmat = [lmat_scr[p] for p in pairs]
        qk_decay = [qkd_scr[p] for p in pairs]
        q_dec = [qdec_scr[p] for p in pairs]
        k_dec = [(kdec_scr[p][:, :hd], kdec_scr[p][:, hd:]) for p in pairs]
        c_next = jnp.minimum(c + 1, n_chunks - 1)
        lsplit = [_split_bf16(lmat[p]) for p in pairs]
        x = [pair_matmul(*lsplit[p], *lsplit[p]) for p in pairs]
        acc = [eye - lmat[p] for p in pairs]
        for level in range(len(levels)):
            for p in pairs:
                sh, sl = _split_bf16(jnp.concatenate([acc[p], x[p]], axis=0))
                both = pair_matmul(sh, sl, sh[CHUNK:], sl[CHUNK:])
                acc[p] = acc[p] + both[:CHUNK]
                x[p] = both[CHUNK:]
                if p // pairs_per_level == level:
                    prepare_pair(c_next, p)
        tmat = [acc[p] + pair_matmul(*_split_bf16(acc[p]), *_split_bf16(x[p])) for p in pairs]
        zb = jnp.zeros((CHUNK, hd), BF16)
        uw = []
        for p in pairs:
            b_row = b_rows[p:p + 1, :]
            t_scaled = jnp.concatenate([tmat[p] * b_row, tmat[p] * (b_row * jnp.exp(g_row[p]))], axis=1)
            v_pk = v_ref[rows, 2 * p * hd:2 * (p + 1) * hd]
            rhs = jnp.concatenate([jnp.concatenate([v_pk[:, :hd], zb, zb, zb], axis=1),
                                   jnp.concatenate([zb, v_pk[:, hd:], zb, zb], axis=1),
                                   jnp.concatenate([zb, zb, k[p], zb], axis=1),
                                   jnp.concatenate([zb, zb, zb, k[p]], axis=1)], axis=0)
            uw.append(jnp.dot(t_scaled.astype(BF16), rhs, preferred_element_type=F32))
        s_old = [state_scr[p] for p in pairs]
        wq = []
        for p in pairs:
            wq.append(jnp.dot(jnp.concatenate([uw[p][:, 2 * hd:].astype(BF16), q_dec[p]], axis=0),
                              block_diag(s_old[p].astype(BF16), state_left), preferred_element_type=F32))
        o_pk = []
        for p in pairs:
            vn = (uw[p][:, :2 * hd] - wq[p][:CHUNK]).astype(BF16)
            vn_diag = jnp.concatenate([jnp.concatenate([vn[:, :hd], zb], axis=1),
                                       jnp.concatenate([zb, vn[:, hd:]], axis=1)], axis=0)
            o_pk.append(wq[p][CHUNK:] + jnp.dot(qk_decay[p], vn_diag, preferred_element_type=F32))
            ge_a, ge_b = g_row[p][:, CHUNK - 1:CHUNK], g_row[p][:, 2 * CHUNK - 1:2 * CHUNK]
            ds_a = lax.dot_general(k_dec[p][0], vn[:, :hd], _TN, preferred_element_type=F32)
            ds_b = lax.dot_general(k_dec[p][1], vn[:, hd:], _TN, preferred_element_type=F32)
            state_scr[p] = jnp.concatenate([s_old[p][:, :hd] * jnp.exp(ge_a) + ds_a,
                                            s_old[p][:, hd:] * jnp.exp(ge_b) + ds_b], axis=1)
        for p in pairs:
            wide_cols = slice(2 * p * hd, 2 * (p + 1) * hd)
            z_pk = z_ref[rows, wide_cols].astype(F32)
            gated = [_rms_rows(o_pk[p][:, h * hd:(h + 1) * hd], ng_ref[...]) * _silu(z_pk[:, h * hd:(h + 1) * hd])
                     for h in range(2)]
            o_ref[rows, wide_cols] = jnp.concatenate(gated, axis=1).astype(o_ref.dtype)
        return carry

    lax.fori_loop(0, n_chunks, chunk_step, 0)


def _delta_call(qkv, z, beta_g, gc_g, gct_g, norm_g, *, n_qk_heads, head_dim):
    s_len = qkv.shape[0]
    n_groups = n_qk_heads // DELTA_PAIRS
    heads = 2 * DELTA_PAIRS
    rows = DELTA_CHUNKS * CHUNK
    qk_w, v_w = DELTA_PAIRS * head_dim, heads * head_dim
    key_dim, value_dim = n_qk_heads * head_dim, 2 * n_qk_heads * head_dim
    k_off, v_off = key_dim // qk_w, 2 * key_dim // v_w
    block_bytes = 2 * rows * (2 * qk_w + 3 * v_w) * 2 + 6 * rows * V7X_LANES * 4 + heads * head_dim * head_dim * 4
    return pl.pallas_call(
        functools.partial(_delta_kernel, n_pairs=DELTA_PAIRS, n_chunks=DELTA_CHUNKS, head_dim=head_dim),
        name="gdn_delta_rule",
        out_shape=jax.ShapeDtypeStruct((s_len, value_dim), BF16),
        grid=(n_groups, s_len // rows),
        in_specs=[
            pl.BlockSpec((rows, qk_w), lambda g, b: (b, g)),
            pl.BlockSpec((rows, qk_w), lambda g, b: (b, k_off + g)),
            pl.BlockSpec((rows, v_w), lambda g, b: (b, v_off + g)),
            pl.BlockSpec((rows, v_w), lambda g, b: (b, g)),
            pl.BlockSpec((None, rows, heads), lambda g, b: (g, b, 0)),
            pl.BlockSpec((None, rows, heads), lambda g, b: (g, b, 0)),
            pl.BlockSpec((None, 2, DELTA_CHUNKS, DELTA_PAIRS, 2 * CHUNK), lambda g, b: (g, 0, b, 0, 0)),
            pl.BlockSpec((1, head_dim), lambda g, b: (0, 0)),
        ],
        out_specs=pl.BlockSpec((rows, v_w), lambda g, b: (b, g)),
        scratch_shapes=[pltpu.VMEM((DELTA_PAIRS, head_dim, 2 * head_dim), F32),
                        pltpu.VMEM((DELTA_PAIRS, CHUNK, 2 * CHUNK), F32),
                        pltpu.VMEM((DELTA_PAIRS, CHUNK, 2 * CHUNK), BF16),
                        pltpu.VMEM((DELTA_PAIRS, CHUNK, 2 * head_dim), BF16),
                        pltpu.VMEM((DELTA_PAIRS, CHUNK, 2 * head_dim), BF16)],
        compiler_params=_params(("arbitrary", "arbitrary"), block_bytes),
    )(qkv, qkv, qkv, z, beta_g, gc_g, gct_g, norm_g.reshape(1, head_dim))


def _diff_attention_layer(x, gain, w_qkv, layer, q_norm, k_norm, lq1, lk1, lq2, lk2, subln, w_o, lambda_init):
    s_len = x.shape[0]
    head_dim = q_norm.shape[0]
    n_heads = w_o.shape[1] // subln.shape[0]
    qk_dim = 2 * n_heads * head_dim
    v_dim = w_qkv.shape[2] - 2 * qk_dim
    tile = min(ATTN_TILE, s_len)
    vec_spec = pl.BlockSpec((1, head_dim), lambda i, j: (0, 0))

    def proj(name, n_cols, first_col, head_gain, scale, transpose):
        head_norm = ((head_gain.reshape(1, -1),), (vec_spec,)) if head_gain is not None else ((), ())
        return _proj_call(
            name, functools.partial(_head_proj_kernel, head_dim=head_dim, scale=scale, transpose=transpose, tile=tile,
                                    w_transposed=False, piece_rows=ROW_SUBTILE),
            x, gain, w_qkv, layer, n_cols, *head_norm, BF16, (),
            col_tile=PLAIN_COL_TILE, first_col_tile=first_col // PLAIN_COL_TILE,
            transposed_tile=tile if transpose else None)

    qt = proj("attn_q_proj", qk_dim, 0, q_norm, head_dim ** -0.5 * LOG2_E, True)
    k = proj("attn_k_proj", qk_dim, qk_dim, k_norm, 1.0, False)
    vt = proj("attn_v_proj", v_dim, 2 * qk_dim, None, 1.0, True)
    slopes = jnp.exp2(-8.0 * jnp.arange(1, n_heads + 1, dtype=F32) / n_heads) * LOG2_E
    o = _attn_call(qt, k, vt, slopes, lq1, lk1, lq2, lk2, subln, n_heads=n_heads, head_dim=head_dim,
                   lambda_init=lambda_init)
    return _mm_res_call(o, w_o, x, layer)


def _gated_deltanet_layer(x, gain, w_in, layer, conv_w, a_log, dt_bias, norm_g, w_out):
    s_len, d = x.shape
    head_dim = norm_g.shape[0]
    n_v_heads = a_log.shape[0]
    n_qk_heads = n_v_heads // 2
    key_dim, value_dim = n_qk_heads * head_dim, n_v_heads * head_dim
    conv_dim = 2 * key_dim + value_dim
    main_cols = conv_dim + value_dim
    n_conv_tiles = conv_dim // COL_TILE
    w_in_t = jnp.swapaxes(w_in, 1, 2)
    cw_spec = pl.BlockSpec((None, conv_w.shape[1], COL_TILE), lambda i, j: (layer, 0, j))
    qkv = _proj_call(
        "gdn_in_proj", functools.partial(_gdn_conv_kernel, n_q_tiles=key_dim // COL_TILE,
                                         n_qk_tiles=2 * key_dim // COL_TILE, head_dim=head_dim,
                                         q_scale=head_dim ** -0.5),
        x, gain, w_in_t, layer, conv_dim, (conv_w,), (cw_spec,), BF16, _conv_scratch(s_len, n_conv_tiles),
        w_transposed=True)
    z = _proj_call(
        "gdn_z_proj", functools.partial(_head_proj_kernel, head_dim=head_dim, scale=1.0, transpose=False, tile=None,
                                        w_transposed=True, piece_rows=min(ROW_TILE, s_len)),
        x, gain, w_in_t, layer, value_dim, (), (), BF16, (), col_tile=PLAIN_COL_TILE,
        first_col_tile=conv_dim // PLAIN_COL_TILE, w_transposed=True)

    gate_cols = 2 * n_v_heads
    assert main_cols % gate_cols == 0
    ba = _proj_call("gdn_gate_proj", functools.partial(_plain_proj_kernel, w_transposed=True), x, gain, w_in_t, layer,
                    gate_cols, (), (), F32, (), col_tile=gate_cols, first_col_tile=main_cols // gate_cols,
                    w_transposed=True)
    on_a_lanes = lambda v: jnp.pad(v, (n_v_heads, 0)).reshape(1, -1)
    beta, gc = _gates_call(ba, on_a_lanes(a_log), on_a_lanes(dt_bias))
    beta, gc = beta[:, :n_v_heads], gc[:, n_v_heads:2 * n_v_heads]

    heads = 2 * DELTA_PAIRS
    n_groups = n_v_heads // heads
    beta_g = beta.reshape(s_len, n_groups, heads).transpose(1, 0, 2)
    gc_g = gc.reshape(s_len, n_groups, heads).transpose(1, 0, 2)
    rows_of = lambda a: a.reshape(s_len // CHUNK, CHUNK, n_groups, DELTA_PAIRS, 2).transpose(
        2, 0, 3, 4, 1).reshape(n_groups, s_len // CHUNK, DELTA_PAIRS, 2 * CHUNK)
    gct_g = jnp.stack([rows_of(gc), rows_of(beta)], axis=1)
    o = _delta_call(qkv, z, beta_g, gc_g, gct_g, norm_g, n_qk_heads=n_qk_heads, head_dim=head_dim)
    return _mm_res_call(o, w_out[layer].astype(BF16), x)


def _conv_ffn_layer(x, gain, w_up, conv_w, conv_b, w_down, layer):
    n_cols = w_up.shape[2]
    n_tiles = pl.cdiv(n_cols, COL_TILE)
    tile_spec = lambda rows: pl.BlockSpec((None, rows, COL_TILE), lambda i, j: (layer, 0, j))
    u = _proj_call("ffn_up_proj", _ffn_up_kernel, x, gain, w_up, layer, n_cols, (conv_w, conv_b[:, None, :]),
                   (tile_spec(conv_w.shape[1]), tile_spec(1)), BF16, _conv_scratch(x.shape[0], n_tiles))
    return _ffn_down_call(u, w_down.astype(BF16), layer, x)


def kernel(x, mixer_norm, ffn_norm, diff_w_qkv, diff_q_norm, diff_k_norm, diff_lambda_q1, diff_lambda_k1,
           diff_lambda_q2, diff_lambda_k2, diff_subln, diff_w_o, gdn_w_in, gdn_conv_w, gdn_A_log, gdn_dt_bias,
           gdn_norm, gdn_w_out, ffn_w_up, ffn_conv_w, ffn_conv_b, ffn_w_down):
    b_sz, s_len, d = x.shape
    depth = mixer_norm.shape[0]
    outs = []
    for b in range(b_sz):
        xb = x.reshape(s_len, d) if b_sz == 1 else x[b]
        for i in range(depth):
            j = i // 2
            if i % 2 == 0:
                lambda_init = 0.8 - 0.6 * math.exp(-0.3 * i)
                xb = _diff_attention_layer(xb, mixer_norm[i], diff_w_qkv, j, diff_q_norm[j], diff_k_norm[j],
                                           diff_lambda_q1[j], diff_lambda_k1[j], diff_lambda_q2[j],
                                           diff_lambda_k2[j], diff_subln[j], diff_w_o, lambda_init)
            else:
                xb = _gated_deltanet_layer(xb, mixer_norm[i], gdn_w_in, j, gdn_conv_w, gdn_A_log[j], gdn_dt_bias[j],
                                           gdn_norm[j], gdn_w_out)
            xb = _conv_ffn_layer(xb, ffn_norm[i], ffn_w_up, ffn_conv_w, ffn_conv_b, ffn_w_down, i)
        outs.append(xb)
    return outs[0].reshape(x.shape) if b_sz == 1 else jnp.stack(outs, axis=0)
```

```python
import functools
import math

import jax
import jax.numpy as jnp
from jax import lax
from jax.experimental import pallas as pl
from jax.experimental.pallas import tpu as pltpu

F32 = jnp.float32
BF16 = jnp.bfloat16

RMS_EPS = 1e-6
LOG2_E = math.log2(math.e)
BIAS_PARTS = 3
CHUNK = 64
V7X_VMEM_BYTES = 64 * 1024 * 1024
V7X_LANES = 128
V7X_SUBLANES = 8
COMPILER_TEMP_BYTES = 12 * 1024 * 1024

ROW_TILE = 1024
ROW_SUBTILE = 256
OUT_ROW_TILE = 512
DOWN_ROW_TILE = 256
DOWN_K_PARTS = 8
COL_TILE = 1024
PLAIN_COL_TILE = 1024
ATTN_TILE = 512
DELTA_PAIRS = 16
DELTA_CHUNKS = 8
GATE_ROWS = 256

_NT = (((1,), (1,)), ((), ()))
_TN = (((0,), (0,)), ((), ()))


def _params(semantics, block_bytes):
    limit = min(int(block_bytes) + COMPILER_TEMP_BYTES, V7X_VMEM_BYTES - (4 << 20))
    return pltpu.CompilerParams(dimension_semantics=semantics, vmem_limit_bytes=limit)


def _rms_rows(x, gain):
    return x * lax.rsqrt(jnp.mean(x * x, axis=-1, keepdims=True) + RMS_EPS) * gain


def _silu(x):
    return x * jax.nn.sigmoid(x)


def _matmul(h, w_bf16, w_transposed):
    if w_transposed:
        return lax.dot_general(h, w_bf16, _NT, preferred_element_type=F32)
    return jnp.dot(h, w_bf16, preferred_element_type=F32)


def _normalize_once(x_ref, g_ref, h_scr):
    @pl.when(pl.program_id(1) == 0)
    def _():
        h_scr[...] = _rms_rows(x_ref[...], g_ref[...]).astype(BF16)


def _first_tile_or_later(step):
    @pl.when(pl.program_id(1) == 0)
    def _():
        step(True)

    @pl.when(pl.program_id(1) != 0)
    def _():
        step(False)


def _normalize_rows(x_ref, g_ref, h_scr, r0, n_rows):
    h_scr[r0:r0 + n_rows, :] = _rms_rows(x_ref[r0:r0 + n_rows, :], g_ref[...]).astype(BF16)


def _pipelined_rows(h_scr, w_bf16, w_transposed, piece_rows, epilogue, before_piece=None):
    n_sub = h_scr.shape[0] // piece_rows

    def piece(r):
        if before_piece is not None:
            before_piece(r)
        return _matmul(h_scr[r * piece_rows:(r + 1) * piece_rows, :], w_bf16, w_transposed)

    u_prev = piece(0)
    for r in range(1, n_sub):
        u_next = piece(r)
        epilogue(r - 1, u_prev)
        u_prev = u_next
    epilogue(n_sub - 1, u_prev)


def _head_proj_kernel(x_ref, g_ref, w_ref, *refs, head_dim, scale, transpose, tile, w_transposed, piece_rows):
    normalize = len(refs) == 3
    gain_ref = refs[0] if normalize else None
    o_ref, h_scr = refs[-2:]

    def epilogue(r, acc):
        if normalize:
            acc = jnp.concatenate([_rms_rows(acc[:, c:c + head_dim], gain_ref[...]) * scale
                                   for c in range(0, acc.shape[1], head_dim)], axis=1)
        r0 = r * piece_rows
        if transpose:
            o_ref[r0 // tile, :, r0 % tile:r0 % tile + piece_rows] = acc.T.astype(o_ref.dtype)
        else:
            o_ref[r0:r0 + piece_rows, :] = acc.astype(o_ref.dtype)

    def step(first):
        fill = (lambda r: _normalize_rows(x_ref, g_ref, h_scr, r * piece_rows, piece_rows)) if first else None
        _pipelined_rows(h_scr, w_ref[...].astype(BF16), w_transposed, piece_rows, epilogue, fill)

    _first_tile_or_later(step)


def _plain_proj_kernel(x_ref, g_ref, w_ref, o_ref, h_scr, *, w_transposed):
    _normalize_once(x_ref, g_ref, h_scr)
    o_ref[...] = _matmul(h_scr[...], w_ref[...].astype(BF16), w_transposed).astype(o_ref.dtype)


def _conv_proj_steps(x_ref, g_ref, h_scr, w_ref, cw_ref, o_ref, carry_scr, u_scr, *, piece_rows, piece_cols,
                     w_transposed, finish):
    i, j = pl.program_id(0), pl.program_id(1)
    rows, kw = h_scr.shape[0], cw_ref.shape[0]
    head = V7X_SUBLANES

    @pl.when(jnp.logical_and(i == 0, j == 0))
    def _():
        carry_scr[...] = jnp.zeros(carry_scr.shape, F32)

    u_scr[0:head, :] = carry_scr[j]
    pieces = [(r0, c0) for r0 in range(0, rows, piece_rows) for c0 in range(0, o_ref.shape[1], piece_cols)]

    def matmul_piece(first, w_bf16, r0, c0):
        if first and c0 == 0:
            _normalize_rows(x_ref, g_ref, h_scr, r0, piece_rows)
        cols = slice(c0, c0 + piece_cols)
        w_piece = w_bf16[cols, :] if w_transposed else w_bf16[:, cols]
        u_scr[head + r0:head + r0 + piece_rows, cols] = _matmul(h_scr[r0:r0 + piece_rows, :], w_piece, w_transposed)

    def epilogue_piece(p0, c0):
        cols = slice(c0, c0 + piece_cols)
        for r0 in range(p0, p0 + piece_rows, ROW_SUBTILE):
            y = u_scr[head + r0:head + r0 + ROW_SUBTILE, cols] * cw_ref[kw - 1:kw, cols]
            for s in range(1, kw):
                y = y + u_scr[head + r0 - s:head + r0 - s + ROW_SUBTILE, cols] * cw_ref[kw - 1 - s:kw - s, cols]
            o_ref[r0:r0 + ROW_SUBTILE, cols] = finish(y, j, cols).astype(o_ref.dtype)

    def step(first):
        w_bf16 = w_ref[...].astype(BF16)
        matmul_piece(first, w_bf16, *pieces[0])
        for prev, cur in zip(pieces[:-1], pieces[1:]):
            matmul_piece(first, w_bf16, *cur)
            epilogue_piece(*prev)
        epilogue_piece(*pieces[-1])
        carry_scr[j] = u_scr[rows:rows + head, :]

    _first_tile_or_later(step)


def _ffn_up_kernel(x_ref, g_ref, w_ref, cw_ref, cb_ref, o_ref, h_scr, carry_scr, u_scr):
    _conv_proj_steps(x_ref, g_ref, h_scr, w_ref, cw_ref, o_ref, carry_scr, u_scr, piece_rows=h_scr.shape[0] // 2,
                     piece_cols=o_ref.shape[1], w_transposed=False,
                     finish=lambda y, jj, cols: y + cb_ref[:, cols])


def _gdn_conv_kernel(x_ref, g_ref, w_ref, cw_ref, o_ref, h_scr, carry_scr, u_scr, *, n_q_tiles, n_qk_tiles, head_dim,
                     q_scale):
    def finish(y, jj, cols):
        y = _silu(y)
        l2_scale = jnp.where(jj < n_q_tiles, q_scale, 1.0)
        heads = []
        for c in range(0, y.shape[1], head_dim):
            blk = y[:, c:c + head_dim]
            inv = lax.rsqrt(jnp.sum(blk * blk, axis=-1, keepdims=True) + RMS_EPS) * l2_scale
            heads.append(blk * jnp.where(jj < n_qk_tiles, inv, 1.0))
        return jnp.concatenate(heads, axis=1)

    _conv_proj_steps(x_ref, g_ref, h_scr, w_ref, cw_ref, o_ref, carry_scr, u_scr, piece_rows=ROW_SUBTILE,
                     piece_cols=o_ref.shape[1], w_transposed=True, finish=finish)


def _proj_call(name, kernel, x, gain, w3, layer, n_cols, extra_inputs, extra_specs, out_dtype, scratch,
               col_tile=COL_TILE, first_col_tile=0, transposed_tile=None, w_transposed=False):
    s_len, d = x.shape
    tm, tn = min(ROW_TILE, s_len), col_tile
    grid = (s_len // tm, pl.cdiv(n_cols, tn))
    if w_transposed:
        w_spec = pl.BlockSpec((None, tn, d), lambda i, j: (layer, first_col_tile + j, 0))
    else:
        w_spec = pl.BlockSpec((None, d, tn), lambda i, j: (layer, 0, first_col_tile + j))
    in_specs = [pl.BlockSpec((tm, d), lambda i, j: (i, 0)), pl.BlockSpec((1, d), lambda i, j: (0, 0)), w_spec]
    in_specs += list(extra_specs)
    if transposed_tile is None:
        out_shape = jax.ShapeDtypeStruct((s_len, n_cols), out_dtype)
        out_spec = pl.BlockSpec((tm, tn), lambda i, j: (i, j))
    else:
        out_shape = jax.ShapeDtypeStruct((s_len // transposed_tile, n_cols, transposed_tile), out_dtype)
        out_spec = pl.BlockSpec((tm // transposed_tile, tn, transposed_tile), lambda i, j: (i, j, 0))
    out_bytes = jnp.dtype(out_dtype).itemsize
    block_bytes = 2 * tm * d * 4 + tm * d * 2 + 2 * d * tn * 4 + d * tn * 2 + 2 * tm * tn * out_bytes + 3 * tm * tn * 4
    return pl.pallas_call(
        kernel,
        name=name,
        out_shape=out_shape,
        grid=grid,
        in_specs=in_specs,
        out_specs=out_spec,
        scratch_shapes=[pltpu.VMEM((tm, d), BF16)] + list(scratch),
        compiler_params=_params(("arbitrary", "arbitrary"), block_bytes),
    )(x, gain.reshape(1, d), w3, *extra_inputs)


def _conv_scratch(s_len, n_tiles):
    tm = min(ROW_TILE, s_len)
    return (pltpu.VMEM((n_tiles, V7X_SUBLANES, COL_TILE), F32), pltpu.VMEM((V7X_SUBLANES + tm, COL_TILE), F32))


def _mm_res_kernel(a_ref, w_ref, r_ref, o_ref, *w_scr):
    if w_scr:
        @pl.when(pl.program_id(0) == 0)
        def _():
            for r0 in range(0, w_ref.shape[0], COL_TILE):
                w_scr[0][r0:r0 + COL_TILE, :] = w_ref[r0:r0 + COL_TILE, :].astype(BF16)
        w = w_scr[0][...]
    else:
        w = w_ref[...]
    o_ref[...] = r_ref[...] + jnp.dot(a_ref[...], w, preferred_element_type=F32)


def _mm_res_call(a, w, res, layer=None):
    s_len, k = a.shape
    n = w.shape[-1]
    tm = min(OUT_ROW_TILE, s_len)
    once = pl.Buffered(1)
    if w.ndim == 3:
        w_spec = pl.BlockSpec((None, k, n), lambda i: (layer, 0, 0), pipeline_mode=once)
        scratch = [pltpu.VMEM((k, n), BF16)]
    else:
        w_spec = pl.BlockSpec((k, n), lambda i: (0, 0), pipeline_mode=once)
        scratch = []
    block_bytes = 2 * tm * k * 2 + k * n * w.dtype.itemsize + len(scratch) * k * n * 2 + 5 * tm * n * 4
    return pl.pallas_call(
        _mm_res_kernel,
        name="out_proj_residual",
        out_shape=jax.ShapeDtypeStruct((s_len, n), F32),
        grid=(s_len // tm,),
        in_specs=[pl.BlockSpec((tm, k), lambda i: (i, 0)), w_spec, pl.BlockSpec((tm, n), lambda i: (i, 0))],
        out_specs=pl.BlockSpec((tm, n), lambda i: (i, 0)),
        scratch_shapes=scratch,
        compiler_params=_params(("arbitrary",), block_bytes),
    )(a, w, res)


def _ffn_down_kernel(uv_ref, ug_ref, w_ref, r_ref, o_ref, g_scr):
    d_ff = w_ref.shape[0]
    mxu_k = 2 * V7X_LANES
    edges = [0] + [round(d_ff * c / DOWN_K_PARTS / mxu_k) * mxu_k for c in range(1, DOWN_K_PARTS)] + [d_ff]
    acc = r_ref[...]
    for k0, k1 in zip(edges[:-1], edges[1:]):
        for c0 in range(k0, k1, COL_TILE):
            cols = slice(c0, min(c0 + COL_TILE, k1))
            g_scr[:, cols] = (_silu(ug_ref[:, cols].astype(F32)) * uv_ref[:, cols].astype(F32)).astype(BF16)
        acc = acc + jnp.dot(g_scr[:, k0:k1], w_ref[k0:k1, :], preferred_element_type=F32)
    o_ref[...] = acc


def _ffn_down_call(u, w_down, layer, res):
    s_len = u.shape[0]
    _, d_ff, n = w_down.shape
    tm = min(DOWN_ROW_TILE, s_len)
    block_bytes = 4 * tm * d_ff * 2 + tm * d_ff * 2 + d_ff * n * 2 + 6 * tm * n * 4
    return pl.pallas_call(
        _ffn_down_kernel,
        name="ffn_down_residual",
        out_shape=jax.ShapeDtypeStruct((s_len, n), F32),
        grid=(s_len // tm,),
        in_specs=[
            pl.BlockSpec((tm, d_ff), lambda i: (i, 0)),
            pl.BlockSpec((tm, d_ff), lambda i: (i, 1)),
            pl.BlockSpec((None, d_ff, n), lambda i: (layer, 0, 0), pipeline_mode=pl.Buffered(1)),
            pl.BlockSpec((tm, n), lambda i: (i, 0)),
        ],
        out_specs=pl.BlockSpec((tm, n), lambda i: (i, 0)),
        scratch_shapes=[pltpu.VMEM((tm, d_ff), BF16)],
        compiler_params=_params(("arbitrary",), block_bytes),
    )(u, u, w_down, res)


def _attn_kernel(slopes_ref, qt_ref, k_ref, vt_ref, lq1_ref, lk1_ref, lq2_ref, lk2_ref, subln_ref, o_ref,
                 s_scr, kb_scr, bias_scr, m_scr, l_scr, acc_scr, *, tile, head_dim, lambda_init):
    h, qi = pl.program_id(0), pl.program_id(1)
    slope = slopes_ref[h]
    hd = head_dim

    @pl.when(qi == 0)
    def _():
        key_bias = slope * lax.broadcasted_iota(jnp.int32, (tile, hd), 0).astype(F32)
        lane = lax.broadcasted_iota(jnp.int32, (tile, hd), 1)
        part, parts = key_bias, []
        for _ in range(BIAS_PARTS):
            parts.append(part.astype(BF16).astype(F32))
            part = part - parts[-1]
        kb_cols = jnp.zeros((tile, hd), F32)
        for c in reversed(range(BIAS_PARTS)):
            kb_cols = jnp.where(lane == c, parts[c], kb_cols)
        kb_scr[...] = kb_cols.astype(BF16)
        key = lax.broadcasted_iota(jnp.int32, (tile, tile), 0)
        qry = lax.broadcasted_iota(jnp.int32, (tile, tile), 1)
        bias_scr[...] = jnp.where((key // CHUNK) <= (qry // CHUNK),
                                  -slope * (jnp.abs(key - qry) + key).astype(F32), -jnp.inf)

    ones_rows = jnp.where(lax.broadcasted_iota(jnp.int32, (hd, tile), 0) < BIAS_PARTS, 1.0, 0.0).astype(BF16)

    def issue_scores(kj, slot):
        k_blk = k_ref[pl.ds(pl.multiple_of(kj * tile, tile), tile), :]
        for m in range(2):
            rows = slice(m * hd, (m + 1) * hd)
            s_scr[slot, m] = jnp.dot(jnp.concatenate([k_blk[:, rows], kb_scr[...]], axis=1),
                                     jnp.concatenate([qt_ref[rows, :], ones_rows], axis=0),
                                     preferred_element_type=F32)

    issue_scores(qi, 1)
    issue_scores(0, 0)

    vt_diag = vt_ref[qi]
    for m in range(2):
        z = s_scr[1, m] + bias_scr[...]
        mx = jnp.max(z, axis=0, keepdims=True)
        p = jnp.exp2(z - mx)
        m_scr[m] = mx
        l_scr[m] = jnp.sum(p, axis=0, keepdims=True)
        acc_scr[m] = jnp.dot(vt_diag, p.astype(BF16), preferred_element_type=F32)

    q_pos = (lax.broadcasted_iota(jnp.int32, (1, tile), 1) + qi * tile).astype(F32)

    def kv_step(kj, slot, issue_next):
        if issue_next:
            issue_scores(jnp.minimum(kj + 1, qi - 1), 1 - slot)
        c = -slope * (q_pos - (kj * tile).astype(F32))
        vt_blk = vt_ref[kj]
        for m in range(2):
            z = s_scr[slot, m]
            m_old = m_scr[m]
            m_new = jnp.maximum(m_old, jnp.max(z, axis=0, keepdims=True) + c)
            p = jnp.exp2(z - (m_new - c))
            alpha = jnp.exp2(m_old - m_new)
            l_scr[m] = alpha * l_scr[m] + jnp.sum(p, axis=0, keepdims=True)
            acc_scr[m] = alpha * acc_scr[m] + jnp.dot(vt_blk, p.astype(BF16), preferred_element_type=F32)
            m_scr[m] = m_new

    def two_steps(t, carry):
        kv_step(2 * t, 0, True)
        kv_step(2 * t + 1, 1, True)
        return carry

    lax.fori_loop(0, qi // 2, two_steps, 0)

    @pl.when(lax.rem(qi, 2) == 1)
    def _():
        kv_step(qi - 1, 0, False)

    lam = (jnp.exp(jnp.sum(lq1_ref[...] * lk1_ref[...], axis=-1, keepdims=True))
           - jnp.exp(jnp.sum(lq2_ref[...] * lk2_ref[...], axis=-1, keepdims=True)) + lambda_init)
    o_t = acc_scr[0] / l_scr[0] - lam * (acc_scr[1] / l_scr[1])
    inv = lax.rsqrt(jnp.mean(o_t * o_t, axis=0, keepdims=True) + RMS_EPS)
    o_ref[...] = ((o_t * inv * subln_ref[...]) * (1.0 - lambda_init)).T.astype(o_ref.dtype)


def _attn_call(qt, k, vt, slopes, lq1, lk1, lq2, lk2, subln, *, n_heads, head_dim, lambda_init):
    n_tiles, _, tile = qt.shape
    s_len = k.shape[0]
    hw = 2 * head_dim
    vec = lambda a: a.reshape(1, -1)
    vec_spec = lambda n: pl.BlockSpec((1, n), lambda h, qi: (0, 0))
    block_bytes = 2 * (2 * s_len * hw * 2) + 4 * tile * hw * 2 + 2 * tile * hw * 4 + 5 * tile * tile * 4 \
        + 6 * tile * tile * 4
    return pl.pallas_call(
        functools.partial(_attn_kernel, tile=tile, head_dim=head_dim, lambda_init=lambda_init),
        name="diff_attention",
        out_shape=jax.ShapeDtypeStruct((s_len, n_heads * hw), BF16),
        grid=(n_heads, n_tiles),
        in_specs=[
            pl.BlockSpec(memory_space=pltpu.SMEM),
            pl.BlockSpec((None, hw, tile), lambda h, qi: (qi, h, 0)),
            pl.BlockSpec((s_len, hw), lambda h, qi: (0, h)),
            pl.BlockSpec((n_tiles, hw, tile), lambda h, qi: (0, h, 0)),
            vec_spec(head_dim), vec_spec(head_dim), vec_spec(head_dim), vec_spec(head_dim),
            pl.BlockSpec((hw, 1), lambda h, qi: (0, 0)),
        ],
        out_specs=pl.BlockSpec((tile, hw), lambda h, qi: (qi, h)),
        scratch_shapes=[
            pltpu.VMEM((2, 2, tile, tile), F32),
            pltpu.VMEM((tile, head_dim), BF16),
            pltpu.VMEM((tile, tile), F32),
            pltpu.VMEM((2, 1, tile), F32),
            pltpu.VMEM((2, 1, tile), F32),
            pltpu.VMEM((2, hw, tile), F32),
        ],
        compiler_params=_params(("arbitrary", "arbitrary"), block_bytes),
    )(slopes, qt, k, vt, vec(lq1), vec(lk1), vec(lq2), vec(lk2), subln.reshape(-1, 1))


def _gates_kernel(ba_ref, alog_ref, dtb_ref, beta_ref, gc_ref):
    rows = ba_ref.shape[0]
    beta_ref[...] = jax.nn.sigmoid(ba_ref[...])
    g = -jnp.exp(alog_ref[...]) * jax.nn.softplus(ba_ref[...] + dtb_ref[...])
    r = lax.broadcasted_iota(jnp.int32, (rows, rows), 0)
    c = lax.broadcasted_iota(jnp.int32, (rows, rows), 1)
    tri = jnp.where(r // CHUNK == c // CHUNK, jnp.where(c <= r, 1.0, 0.0), 0.0).astype(F32)
    gc_ref[...] = jnp.dot(tri, g, preferred_element_type=F32, precision=lax.Precision.HIGHEST)


def _gates_call(ba, a_log_row, dt_bias_row):
    s_len, width = ba.shape
    tm = min(GATE_ROWS, s_len)
    out = jax.ShapeDtypeStruct((s_len, width), F32)
    row_spec = pl.BlockSpec((tm, width), lambda i: (i, 0))
    vec_spec = pl.BlockSpec((1, width), lambda i: (0, 0))
    return pl.pallas_call(
        _gates_kernel,
        name="gdn_gates",
        out_shape=(out, out),
        grid=(s_len // tm,),
        in_specs=[row_spec, vec_spec, vec_spec],
        out_specs=(row_spec, row_spec),
        compiler_params=_params(("arbitrary",), 8 * tm * 2 * V7X_LANES * 4 + 4 * tm * tm * 4),
    )(ba, a_log_row, dt_bias_row)


def _split_bf16(x):
    hi = x.astype(BF16)
    return hi, (x - hi.astype(F32)).astype(BF16)


def _delta_kernel(q_ref, k_ref, v_ref, z_ref, beta_ref, gc_ref, gct_ref, ng_ref, o_ref, state_scr, lmat_scr,
                  qkd_scr, qdec_scr, kdec_scr, *,
                  n_pairs, n_chunks, head_dim):
    hd = head_dim
    pairs = range(n_pairs)

    @pl.when(pl.program_id(1) == 0)
    def _():
        state_scr[...] = jnp.zeros(state_scr.shape, F32)

    ri = lax.broadcasted_iota(jnp.int32, (CHUNK, 2 * CHUNK), 0)
    lane = lax.broadcasted_iota(jnp.int32, (CHUNK, 2 * CHUNK), 1)
    left = lane < CHUNK
    ci = jnp.where(left, lane, lane - CHUNK)
    tril = ci <= ri
    strict = ci < ri
    eye = jnp.where(ci == ri, 1.0, 0.0).astype(F32)
    state_left = lax.broadcasted_iota(jnp.int32, (hd, 2 * hd), 1) < hd

    def block_diag(y, keep_left):
        zero = jnp.zeros_like(y)
        return jnp.concatenate([jnp.where(keep_left, y, zero), jnp.where(keep_left, zero, y)], axis=0)

    def pair_matmul(xh, xl, yh, yl):
        bh, bl = block_diag(yh, left), block_diag(yl, left)
        rhs = jnp.concatenate([jnp.concatenate([bh, bl], axis=1),
                               jnp.concatenate([bh, jnp.zeros_like(bl)], axis=1)], axis=0)
        out = jnp.dot(jnp.concatenate([xh, xl], axis=1), rhs, preferred_element_type=F32)
        return out[:, :2 * CHUNK] + out[:, 2 * CHUNK:]

    lanes = lambda a, h: jnp.broadcast_to(a[:, h:h + 1], (CHUNK, hd))

    def prepare_pair(c, p):
        rows = pl.ds(pl.multiple_of(c * CHUNK, CHUNK), CHUNK)
        beta, g_col, g_row = beta_ref[rows, :], gc_ref[rows, :], gct_ref[0, c][p:p + 1, :]
        k, q = k_ref[rows, p * hd:(p + 1) * hd], q_ref[rows, p * hd:(p + 1) * hd]
        kq = lax.dot_general(jnp.concatenate([k, q], axis=0), jnp.concatenate([k, k], axis=0), _NT,
                             preferred_element_type=F32)
        g_a, g_b = lanes(g_col, 2 * p), lanes(g_col, 2 * p + 1)
        b_pk = jnp.where(left, lanes(beta, 2 * p), lanes(beta, 2 * p + 1))
        decay = jnp.exp(jnp.where(tril, jnp.where(left, g_a, g_b) - g_row, -jnp.inf))
        lmat_scr[p] = jnp.where(strict, kq[:CHUNK] * b_pk * decay, 0.0)
        qkd_scr[p] = (kq[CHUNK:] * decay).astype(BF16)
        qf, kf = q.astype(F32), k.astype(F32)
        qdec_scr[p] = jnp.concatenate([qf * jnp.exp(g_a), qf * jnp.exp(g_b)], axis=1).astype(BF16)
        ge_a, ge_b = g_row[:, CHUNK - 1:CHUNK], g_row[:, 2 * CHUNK - 1:2 * CHUNK]
        kdec_scr[p] = jnp.concatenate([kf * jnp.exp(ge_a - g_a), kf * jnp.exp(ge_b - g_b)], axis=1).astype(BF16)

    for p in pairs:
        prepare_pair(0, p)

    levels = []
    power = 2
    while power * 2 < CHUNK:
        levels.append(power)
        power *= 2
    pairs_per_level = -(-n_pairs // len(levels))

    def chunk_step(c, carry):
        rows = pl.ds(pl.multiple_of(c * CHUNK, CHUNK), CHUNK)
        g_rows = gct_ref[0, c]
        b_rows = gct_ref[1, c]
        g_row = [g_rows[p:p + 1, :] for p in pairs]
        k = [k_ref[rows, p * hd:(p + 1) * hd] for p in pairs]
        lmat = [lmat_scr[p] for p in pairs]
        qk_decay = [qkd_scr[p] for p in pairs]
        q_dec = [qdec_scr[p] for p in pairs]
        k_dec = [(kdec_scr[p][:, :hd], kdec_scr[p][:, hd:]) for p in pairs]
        c_next = jnp.minimum(c + 1, n_chunks - 1)
        lsplit = [_split_bf16(lmat[p]) for p in pairs]
        x = [pair_matmul(*lsplit[p], *lsplit[p]) for p in pairs]
        acc = [eye - lmat[p] for p in pairs]
        for level in range(len(levels)):
            for p in pairs:
                sh, sl = _split_bf16(jnp.concatenate([acc[p], x[p]], axis=0))
                both = pair_matmul(sh, sl, sh[CHUNK:], sl[CHUNK:])
                acc[p] = acc[p] + both[:CHUNK]
                x[p] = both[CHUNK:]
                if p // pairs_per_level == level:
                    prepare_pair(c_next, p)
        tmat = [acc[p] + pair_matmul(*_split_bf16(acc[p]), *_split_bf16(x[p])) for p in pairs]
        zb = jnp.zeros((CHUNK, hd), BF16)
        uw = []
        for p in pairs:
            b_row = b_rows[p:p + 1, :]
            t_scaled = jnp.concatenate([tmat[p] * b_row, tmat[p] * (b_row * jnp.exp(g_row[p]))], axis=1)
            v_pk = v_ref[rows, 2 * p * hd:2 * (p + 1) * hd]
            rhs = jnp.concatenate([jnp.concatenate([v_pk[:, :hd], zb, zb, zb], axis=1),
                                   jnp.concatenate([zb, v_pk[:, hd:], zb, zb], axis=1),
                                   jnp.concatenate([zb, zb, k[p], zb], axis=1),
                                   jnp.concatenate([zb, zb, zb, k[p]], axis=1)], axis=0)
            uw.append(jnp.dot(t_scaled.astype(BF16), rhs, preferred_element_type=F32))
        s_old = [state_scr[p] for p in pairs]
        wq = []
        for p in pairs:
            wq.append(jnp.dot(jnp.concatenate([uw[p][:, 2 * hd:].astype(BF16), q_dec[p]], axis=0),
                              block_diag(s_old[p].astype(BF16), state_left), preferred_element_type=F32))
        o_pk = []
        for p in pairs:
            vn = (uw[p][:, :2 * hd] - wq[p][:CHUNK]).astype(BF16)
            vn_diag = jnp.concatenate([jnp.concatenate([vn[:, :hd], zb], axis=1),
                                       jnp.concatenate([zb, vn[:, hd:]], axis=1)], axis=0)
            o_pk.append(wq[p][CHUNK:] + jnp.dot(qk_decay[p], vn_diag, preferred_element_type=F32))
            ge_a, ge_b = g_row[p][:, CHUNK - 1:CHUNK], g_row[p][:, 2 * CHUNK - 1:2 * CHUNK]
            ds_a = lax.dot_general(k_dec[p][0], vn[:, :hd], _TN, preferred_element_type=F32)
            ds_b = lax.dot_general(k_dec[p][1], vn[:, hd:], _TN, preferred_element_type=F32)
            state_scr[p] = jnp.concatenate([s_old[p][:, :hd] * jnp.exp(ge_a) + ds_a,
                                            s_old[p][:, hd:] * jnp.exp(ge_b) + ds_b], axis=1)
        for p in pairs:
            wide_cols = slice(2 * p * hd, 2 * (p + 1) * hd)
            z_pk = z_ref[rows, wide_cols].astype(F32)
            gated = [_rms_rows(o_pk[p][:, h * hd:(h + 1) * hd], ng_ref[...]) * _silu(z_pk[:, h * hd:(h + 1) * hd])
                     for h in range(2)]
            o_ref[rows, wide_cols] = jnp.concatenate(gated, axis=1).astype(o_ref.dtype)
        return carry

    lax.fori_loop(0, n_chunks, chunk_step, 0)


def _delta_call(qkv, z, beta_g, gc_g, gct_g, norm_g, *, n_qk_heads, head_dim):
    s_len = qkv.shape[0]
    n_groups = n_qk_heads // DELTA_PAIRS
    heads = 2 * DELTA_PAIRS
    rows = DELTA_CHUNKS * CHUNK
    qk_w, v_w = DELTA_PAIRS * head_dim, heads * head_dim
    key_dim, value_dim = n_qk_heads * head_dim, 2 * n_qk_heads * head_dim
    k_off, v_off = key_dim // qk_w, 2 * key_dim // v_w
    block_bytes = 2 * rows * (2 * qk_w + 3 * v_w) * 2 + 6 * rows * V7X_LANES * 4 + heads * head_dim * head_dim * 4
    return pl.pallas_call(
        functools.partial(_delta_kernel, n_pairs=DELTA_PAIRS, n_chunks=DELTA_CHUNKS, head_dim=head_dim),
        name="gdn_delta_rule",
        out_shape=jax.ShapeDtypeStruct((s_len, value_dim), BF16),
        grid=(n_groups, s_len // rows),
        in_specs=[
            pl.BlockSpec((rows, qk_w), lambda g, b: (b, g)),
            pl.BlockSpec((rows, qk_w), lambda g, b: (b, k_off + g)),
            pl.BlockSpec((rows, v_w), lambda g, b: (b, v_off + g)),
            pl.BlockSpec((rows, v_w), lambda g, b: (b, g)),
            pl.BlockSpec((None, rows, heads), lambda g, b: (g, b, 0)),
            pl.BlockSpec((None, rows, heads), lambda g, b: (g, b, 0)),
            pl.BlockSpec((None, 2, DELTA_CHUNKS, DELTA_PAIRS, 2 * CHUNK), lambda g, b: (g, 0, b, 0, 0)),
            pl.BlockSpec((1, head_dim), lambda g, b: (0, 0)),
        ],
        out_specs=pl.BlockSpec((rows, v_w), lambda g, b: (b, g)),
        scratch_shapes=[pltpu.VMEM((DELTA_PAIRS, head_dim, 2 * head_dim), F32),
                        pltpu.VMEM((DELTA_PAIRS, CHUNK, 2 * CHUNK), F32),
                        pltpu.VMEM((DELTA_PAIRS, CHUNK, 2 * CHUNK), BF16),
                        pltpu.VMEM((DELTA_PAIRS, CHUNK, 2 * head_dim), BF16),
                        pltpu.VMEM((DELTA_PAIRS, CHUNK, 2 * head_dim), BF16)],
        compiler_params=_params(("arbitrary", "arbitrary"), block_bytes),
    )(qkv, qkv, qkv, z, beta_g, gc_g, gct_g, norm_g.reshape(1, head_dim))


def _diff_attention_layer(x, gain, w_qkv, layer, q_norm, k_norm, lq1, lk1, lq2, lk2, subln, w_o, lambda_init):
    s_len = x.shape[0]
    head_dim = q_norm.shape[0]
    n_heads = w_o.shape[1] // subln.shape[0]
    qk_dim = 2 * n_heads * head_dim
    v_dim = w_qkv.shape[2] - 2 * qk_dim
    tile = min(ATTN_TILE, s_len)
    vec_spec = pl.BlockSpec((1, head_dim), lambda i, j: (0, 0))

    def proj(name, n_cols, first_col, head_gain, scale, transpose):
        head_norm = ((head_gain.reshape(1, -1),), (vec_spec,)) if head_gain is not None else ((), ())
        return _proj_call(
            name, functools.partial(_head_proj_kernel, head_dim=head_dim, scale=scale, transpose=transpose, tile=tile,
                                    w_transposed=False, piece_rows=ROW_SUBTILE),
            x, gain, w_qkv, layer, n_cols, *head_norm, BF16, (),
            col_tile=PLAIN_COL_TILE, first_col_tile=first_col // PLAIN_COL_TILE,
            transposed_tile=tile if transpose else None)

    qt = proj("attn_q_proj", qk_dim, 0, q_norm, head_dim ** -0.5 * LOG2_E, True)
    k = proj("attn_k_proj", qk_dim, qk_dim, k_norm, 1.0, False)
    vt = proj("attn_v_proj", v_dim, 2 * qk_dim, None, 1.0, True)
    slopes = jnp.exp2(-8.0 * jnp.arange(1, n_heads + 1, dtype=F32) / n_heads) * LOG2_E
    o = _attn_call(qt, k, vt, slopes, lq1, lk1, lq2, lk2, subln, n_heads=n_heads, head_dim=head_dim,
                   lambda_init=lambda_init)
    return _mm_res_call(o, w_o, x, layer)


def _gated_deltanet_layer(x, gain, w_in, layer, conv_w, a_log, dt_bias, norm_g, w_out):
    s_len, d = x.shape
    head_dim = norm_g.shape[0]
    n_v_heads = a_log.shape[0]
    n_qk_heads = n_v_heads // 2
    key_dim, value_dim = n_qk_heads * head_dim, n_v_heads * head_dim
    conv_dim = 2 * key_dim + value_dim
    main_cols = conv_dim + value_dim
    n_conv_tiles = conv_dim // COL_TILE
    w_in_t = jnp.swapaxes(w_in, 1, 2)
    cw_spec = pl.BlockSpec((None, conv_w.shape[1], COL_TILE), lambda i, j: (layer, 0, j))
    qkv = _proj_call(
        "gdn_in_proj", functools.partial(_gdn_conv_kernel, n_q_tiles=key_dim // COL_TILE,
                                         n_qk_tiles=2 * key_dim // COL_TILE, head_dim=head_dim,
                                         q_scale=head_dim ** -0.5),
        x, gain, w_in_t, layer, conv_dim, (conv_w,), (cw_spec,), BF16, _conv_scratch(s_len, n_conv_tiles),
        w_transposed=True)
    z = _proj_call(
        "gdn_z_proj", functools.partial(_head_proj_kernel, head_dim=head_dim, scale=1.0, transpose=False, tile=None,
                                        w_transposed=True, piece_rows=min(ROW_TILE, s_len)),
        x, gain, w_in_t, layer, value_dim, (), (), BF16, (), col_tile=PLAIN_COL_TILE,
        first_col_tile=conv_dim // PLAIN_COL_TILE, w_transposed=True)

    gate_cols = 2 * n_v_heads
    assert main_cols % gate_cols == 0
    ba = _proj_call("gdn_gate_proj", functools.partial(_plain_proj_kernel, w_transposed=True), x, gain, w_in_t, layer,
                    gate_cols, (), (), F32, (), col_tile=gate_cols, first_col_tile=main_cols // gate_cols,
                    w_transposed=True)
    on_a_lanes = lambda v: jnp.pad(v, (n_v_heads, 0)).reshape(1, -1)
    beta, gc = _gates_call(ba, on_a_lanes(a_log), on_a_lanes(dt_bias))
    beta, gc = beta[:, :n_v_heads], gc[:, n_v_heads:2 * n_v_heads]

    heads = 2 * DELTA_PAIRS
    n_groups = n_v_heads // heads
    beta_g = beta.reshape(s_len, n_groups, heads).transpose(1, 0, 2)
    gc_g = gc.reshape(s_len, n_groups, heads).transpose(1, 0, 2)
    rows_of = lambda a: a.reshape(s_len // CHUNK, CHUNK, n_groups, DELTA_PAIRS, 2).transpose(
        2, 0, 3, 4, 1).reshape(n_groups, s_len // CHUNK, DELTA_PAIRS, 2 * CHUNK)
    gct_g = jnp.stack([rows_of(gc), rows_of(beta)], axis=1)
    o = _delta_call(qkv, z, beta_g, gc_g, gct_g, norm_g, n_qk_heads=n_qk_heads, head_dim=head_dim)
    return _mm_res_call(o, w_out[layer].astype(BF16), x)


def _conv_ffn_layer(x, gain, w_up, conv_w, conv_b, w_down, layer):
    n_cols = w_up.shape[2]
    n_tiles = pl.cdiv(n_cols, COL_TILE)
    tile_spec = lambda rows: pl.BlockSpec((None, rows, COL_TILE), lambda i, j: (layer, 0, j))
    u = _proj_call("ffn_up_proj", _ffn_up_kernel, x, gain, w_up, layer, n_cols, (conv_w, conv_b[:, None, :]),
                   (tile_spec(conv_w.shape[1]), tile_spec(1)), BF16, _conv_scratch(x.shape[0], n_tiles))
    return _ffn_down_call(u, w_down.astype(BF16), layer, x)


def kernel(x, mixer_norm, ffn_norm, diff_w_qkv, diff_q_norm, diff_k_norm, diff_lambda_q1, diff_lambda_k1,
           diff_lambda_q2, diff_lambda_k2, diff_subln, diff_w_o, gdn_w_in, gdn_conv_w, gdn_A_log, gdn_dt_bias,
           gdn_norm, gdn_w_out, ffn_w_up, ffn_conv_w, ffn_conv_b, ffn_w_down):
    b_sz, s_len, d = x.shape
    depth = mixer_norm.shape[0]
    outs = []
    for b in range(b_sz):
        xb = x.reshape(s_len, d) if b_sz == 1 else x[b]
        for i in range(depth):
            j = i // 2
            if i % 2 == 0:
                lambda_init = 0.8 - 0.6 * math.exp(-0.3 * i)
                xb = _diff_attention_layer(xb, mixer_norm[i], diff_w_qkv, j, diff_q_norm[j], diff_k_norm[j],
                                           diff_lambda_q1[j], diff_lambda_k1[j], diff_lambda_q2[j],
                                           diff_lambda_k2[j], diff_subln[j], diff_w_o, lambda_init)
            else:
                xb = _gated_deltanet_layer(xb, mixer_norm[i], gdn_w_in, j, gdn_conv_w, gdn_A_log[j], gdn_dt_bias[j],
                                           gdn_norm[j], gdn_w_out)
            xb = _conv_ffn_layer(xb, ffn_norm[i], ffn_w_up, ffn_conv_w, ffn_conv_b, ffn_w_down, i)
        outs.append(xb)
    return outs[0].reshape(x.shape) if b_sz == 1 else jnp.stack(outs, axis=0)
```
